```python
import math
import jax, jax.numpy as jnp
from jax import lax
import numpy as np

D_MODEL = 4096
BATCH = 4
SEQ = 4096
DEPTH = 4

N_MIXERS = 2
N_RWKV = (DEPTH + 1) // 2
N_ATTN = DEPTH // 2

RWKV_HEAD = 64
RWKV_HEADS = D_MODEL // RWKV_HEAD
LORA_DECAY = max(32, int(round(1.8 * D_MODEL ** 0.5 / 32)) * 32)
LORA_AAA = max(32, int(round(1.8 * D_MODEL ** 0.5 / 32)) * 32)
LORA_MV = max(32, int(round(1.3 * D_MODEL ** 0.5 / 32)) * 32)
LORA_GATE = max(32, int(round(0.6 * D_MODEL ** 0.8 / 32)) * 32)
GN_EPS = RWKV_HEAD * 1e-5

ATT_HEAD = 64
ATT_Q_HEADS = D_MODEL // ATT_HEAD
ATT_KV_HEADS = 8
ATT_GROUP = ATT_Q_HEADS // ATT_KV_HEADS
WINDOW = 128
BLOCK = 128
REL_BUCKETS = 32
REL_MAX_DIST = 128

D_FF = 11008
CONV_W = 3

LN_EPS = 1e-5
DEEPNORM_ALPHA = (2 * DEPTH) ** 0.25
DEEPNORM_BETA = (8 * DEPTH) ** -0.25

kernel_name = "hybrid_rwkv7_swa_sink_convffn_deepnorm"


def layer_norm(x, g, b):
    xf = x.astype(jnp.float32)
    mu = jnp.mean(xf, axis=-1, keepdims=True)
    var = jnp.mean(jnp.square(xf - mu), axis=-1, keepdims=True)
    return ((xf - mu) * lax.rsqrt(var + LN_EPS) * g + b).astype(x.dtype)


def token_shift(x):
    return jnp.pad(x, ((0, 0), (1, 0), (0, 0)))[:, :-1]


def rwkv7_time_mix(x, v_first, mix, w_rkv, w_o, w0, w1, w2, a0, a1, a2, g1, g2,
                   k_k, k_a, r_k, gn_g, gn_b, v_lora):
    B, S, D = x.shape
    H, N = RWKV_HEADS, RWKV_HEAD
    f32 = jnp.float32
    xx = token_shift(x) - x
    xr = x + xx * mix[0]
    xw = x + xx * mix[1]
    xk = x + xx * mix[2]
    xv = x + xx * mix[3]
    xa = x + xx * mix[4]
    xg = x + xx * mix[5]
    r, k, v = jnp.einsum('pbsd,pde->pbse', jnp.stack([xr, xk, xv]), w_rkv)
    w_log = -jax.nn.softplus(-(w0 + jnp.tanh(xw @ w1) @ w2)) - 0.5
    decay = jnp.exp(-jnp.exp(w_log.astype(f32)))
    if v_lora is None:
        v_first = v
    else:
        v0, v1, v2 = v_lora
        v = v + (v_first - v) * jax.nn.sigmoid(v0 + (xv @ v1) @ v2)
    a = jax.nn.sigmoid(a0 + (xa @ a1) @ a2)
    g = jax.nn.sigmoid(xg @ g1) @ g2
    kk = (k * k_k).astype(f32).reshape(B, S, H, N)
    kk = kk / jnp.maximum(jnp.linalg.norm(kk, axis=-1, keepdims=True), 1e-12)
    k = k * (1 + (a - 1) * k_a)

    def to_scan(t):
        return t.astype(f32).reshape(B, S, H, N).transpose(1, 0, 2, 3)

    def step(state, inp):
        r_t, w_t, k_t, v_t, kk_t, a_t = inp
        s_kk = jnp.einsum('bhvk,bhk->bhv', state, -kk_t)
        state = (state * w_t[:, :, None, :]
                 + s_kk[..., None] * (kk_t * a_t)[:, :, None, :]
                 + v_t[..., None] * k_t[:, :, None, :])
        y_t = jnp.einsum('bhvk,bhk->bhv', state, r_t)
        return state, y_t

    state0 = jnp.zeros((B, H, N, N), f32)
    xs = (to_scan(r), decay.reshape(B, S, H, N).transpose(1, 0, 2, 3), to_scan(k), to_scan(v),
          kk.transpose(1, 0, 2, 3), to_scan(a))
    _, y = lax.scan(step, state0, xs)
    y = y.transpose(1, 0, 2, 3)
    mu = jnp.mean(y, axis=-1, keepdims=True)
    var = jnp.mean(jnp.square(y - mu), axis=-1, keepdims=True)
    y = ((y - mu) * lax.rsqrt(var + GN_EPS)).reshape(B, S, D) * gn_g + gn_b
    bonus = jnp.sum((r * k * r_k).astype(f32).reshape(B, S, H, N), axis=-1, keepdims=True) \
        * v.astype(f32).reshape(B, S, H, N)
    y = ((y + bonus.reshape(B, S, D)) * g).astype(x.dtype)
    return y @ w_o, v_first


def t5_causal_bucket(dist):
    max_exact = REL_BUCKETS // 2
    d_f = jnp.maximum(dist, max_exact).astype(jnp.float32)
    large = max_exact + (jnp.log(d_f / max_exact) / math.log(REL_MAX_DIST / max_exact)
                         * (REL_BUCKETS - max_exact)).astype(jnp.int32)
    large = jnp.minimum(large, REL_BUCKETS - 1)
    return jnp.where(dist < max_exact, dist, large)


def swa_sink_attention(x, w_qkv, w_o, sinks, rel_bias):
    B, S, D = x.shape
    H, KV, G, hd = ATT_Q_HEADS, ATT_KV_HEADS, ATT_GROUP, ATT_HEAD
    nb = S // BLOCK
    f32 = jnp.float32
    qkv = x @ w_qkv
    q, k, v = jnp.split(qkv, [H * hd, H * hd + KV * hd], axis=-1)
    q = q.reshape(B, nb, BLOCK, KV, G, hd) * (hd ** -0.5)
    k = k.reshape(B, nb, BLOCK, KV, hd)
    v = v.reshape(B, nb, BLOCK, KV, hd)
    prev = lambda t: jnp.pad(t, ((0, 0), (1, 0), (0, 0), (0, 0), (0, 0)))[:, :-1]
    k_band = jnp.concatenate([prev(k), k], axis=2)
    v_band = jnp.concatenate([prev(v), v], axis=2)
    qi = jnp.arange(BLOCK)[:, None]
    kj = jnp.arange(2 * BLOCK)[None, :]
    dist = qi + BLOCK - kj
    in_window = (dist >= 0) & (dist < WINDOW)
    bias = rel_bias[t5_causal_bucket(jnp.maximum(dist, 0))]
    bias = bias.astype(f32).transpose(2, 0, 1).reshape(KV, G, BLOCK, 2 * BLOCK)
    sink = sinks.astype(f32).reshape(KV, G)[None, :, :, None, None]

    def block_fn(args):
        n, qb, kb, vb = args
        s = jnp.einsum('bqhgd,bkhd->bhgqk', qb, kb).astype(f32) + bias
        valid = in_window & ((n * BLOCK - BLOCK + kj) >= 0)
        s = jnp.where(valid, s, -jnp.inf)
        m = jnp.maximum(jnp.max(s, axis=-1, keepdims=True), sink)
        p = jnp.exp(s - m)
        denom = jnp.sum(p, axis=-1, keepdims=True) + jnp.exp(sink - m)
        return jnp.einsum('bhgqk,bkhd->bqhgd', (p / denom).astype(vb.dtype), vb)

    out = lax.map(block_fn, (jnp.arange(nb), jnp.moveaxis(q, 1, 0),
                             jnp.moveaxis(k_band, 1, 0), jnp.moveaxis(v_band, 1, 0)))
    out = jnp.moveaxis(out, 0, 1).reshape(B, S, H * hd)
    return out @ w_o


def conv_gated_ffn(x, w_up, conv_w, conv_b, w_down):
    h = x @ w_up
    h = lax.conv_general_dilated(h, conv_w[:, None, :], window_strides=(1,),
                                 padding=((CONV_W - 1, 0),),
                                 dimension_numbers=('NWC', 'WIO', 'NWC'),
                                 feature_group_count=h.shape[-1]) + conv_b
    gate, up = jnp.split(h, 2, axis=-1)
    return (jax.nn.silu(gate) * up) @ w_down


def setup_inputs(seed: int = 0) -> dict:
    key = jax.random.key(seed)
    ks = iter(jax.random.split(key, 40))
    D, F = D_MODEL, D_FF
    nrm = lambda shape, scale: jax.random.normal(next(ks), shape, jnp.float32) * scale
    nv = max(N_RWKV - 1, 0)
    qkv_w = (ATT_Q_HEADS + 2 * ATT_KV_HEADS) * ATT_HEAD
    return {
        "x": nrm((BATCH, SEQ, D), 1.0),
        "rwkv_mix": jax.random.uniform(next(ks), (N_RWKV, 6, D), jnp.float32),
        "rwkv_w_rkv": nrm((N_RWKV, 3, D, D), D ** -0.5),
        "rwkv_w_o": nrm((N_RWKV, D, D), DEEPNORM_BETA * D ** -0.5),
        "rwkv_w0": jnp.linspace(-6.0, -1.0, D, dtype=jnp.float32)[None] + nrm((N_RWKV, D), 0.1),
        "rwkv_w1": nrm((N_RWKV, D, LORA_DECAY), D ** -0.5),
        "rwkv_w2": nrm((N_RWKV, LORA_DECAY, D), 0.3 * LORA_DECAY ** -0.5),
        "rwkv_a0": nrm((N_RWKV, D), 0.1),
        "rwkv_a1": nrm((N_RWKV, D, LORA_AAA), D ** -0.5),
        "rwkv_a2": nrm((N_RWKV, LORA_AAA, D), 0.3 * LORA_AAA ** -0.5),
        "rwkv_g1": nrm((N_RWKV, D, LORA_GATE), D ** -0.5),
        "rwkv_g2": nrm((N_RWKV, LORA_GATE, D), 2.0 * LORA_GATE ** -0.5),
        "rwkv_k_k": 0.85 + nrm((N_RWKV, D), 0.05),
        "rwkv_k_a": 1.0 + nrm((N_RWKV, D), 0.05),
        "rwkv_r_k": nrm((N_RWKV, D), 0.1),
        "rwkv_gn_g": 1.0 + nrm((N_RWKV, D), 0.05),
        "rwkv_gn_b": nrm((N_RWKV, D), 0.01),
        "rwkv_v0": nrm((nv, D), 0.1),
        "rwkv_v1": nrm((nv, D, LORA_MV), D ** -0.5),
        "rwkv_v2": nrm((nv, LORA_MV, D), 0.3 * LORA_MV ** -0.5),
        "attn_w_qkv": nrm((N_ATTN, D, qkv_w), D ** -0.5),
        "attn_w_o": nrm((N_ATTN, ATT_Q_HEADS * ATT_HEAD, D), DEEPNORM_BETA * D ** -0.5),
        "attn_sinks": nrm((N_ATTN, ATT_Q_HEADS), 0.5),
        "rel_bias": nrm((REL_BUCKETS, ATT_Q_HEADS), 0.5),
        "ffn_w_up": nrm((DEPTH, D, 2 * F), D ** -0.5),
        "ffn_conv_w": nrm((DEPTH, CONV_W, 2 * F), CONV_W ** -0.5),
        "ffn_conv_b": nrm((DEPTH, 2 * F), 0.01),
        "ffn_w_down": nrm((DEPTH, F, D), DEEPNORM_BETA * F ** -0.5),
        "ln1_g": 1.0 + nrm((DEPTH, D), 0.05),
        "ln1_b": nrm((DEPTH, D), 0.01),
        "ln2_g": 1.0 + nrm((DEPTH, D), 0.05),
        "ln2_b": nrm((DEPTH, D), 0.01),
    }


def reference(x, rwkv_mix, rwkv_w_rkv, rwkv_w_o, rwkv_w0, rwkv_w1, rwkv_w2, rwkv_a0, rwkv_a1,
              rwkv_a2, rwkv_g1, rwkv_g2, rwkv_k_k, rwkv_k_a, rwkv_r_k, rwkv_gn_g, rwkv_gn_b,
              rwkv_v0, rwkv_v1, rwkv_v2, attn_w_qkv, attn_w_o, attn_sinks, rel_bias,
              ffn_w_up, ffn_conv_w, ffn_conv_b, ffn_w_down, ln1_g, ln1_b, ln2_g, ln2_b):
    v_first = None
    for i in range(DEPTH):
        j = i // N_MIXERS
        if i % N_MIXERS == 0:
            v_lora = None if j == 0 else (rwkv_v0[j - 1], rwkv_v1[j - 1], rwkv_v2[j - 1])
            h, v_first = rwkv7_time_mix(x, v_first, rwkv_mix[j], rwkv_w_rkv[j], rwkv_w_o[j],
                                        rwkv_w0[j], rwkv_w1[j], rwkv_w2[j], rwkv_a0[j],
                                        rwkv_a1[j], rwkv_a2[j], rwkv_g1[j], rwkv_g2[j],
                                        rwkv_k_k[j], rwkv_k_a[j], rwkv_r_k[j], rwkv_gn_g[j],
                                        rwkv_gn_b[j], v_lora)
        else:
            h = swa_sink_attention(x, attn_w_qkv[j], attn_w_o[j], attn_sinks[j], rel_bias)
        x = layer_norm(DEEPNORM_ALPHA * x + h, ln1_g[i], ln1_b[i])
        f = conv_gated_ffn(x, ffn_w_up[i], ffn_conv_w[i], ffn_conv_b[i], ffn_w_down[i])
        x = layer_norm(DEEPNORM_ALPHA * x + f, ln2_g[i], ln2_b[i])
    return x
```

```python
import functools
import math

import jax
import jax.numpy as jnp
from jax import lax
from jax.experimental import pallas as pl
from jax.experimental.pallas import tpu as pltpu

F32 = jnp.float32
BF16 = jnp.bfloat16

LANES = 128
SUBLANES = 8
MXU_DIM = 256
VMEM_LIMIT_BYTES = 56 * 1024 * 1024

HEAD = 64
HEADS_PER_GROUP = MXU_DIM // HEAD
CHUNK = 64
STACK = HEADS_PER_GROUP * CHUNK

WINDOW = 128
BLOCK = 128
REL_BUCKETS = 32
REL_MAX_DIST = 128
CONV_W = 3
LN_EPS = 1e-5
GN_EPS = HEAD * 1e-5
DEPTH = 4
DEEPNORM_ALPHA = (2 * DEPTH) ** 0.25


def _params(*sem):
    return pltpu.CompilerParams(dimension_semantics=sem, vmem_limit_bytes=VMEM_LIMIT_BYTES)


def _block(n, target, unit):
    if n <= target:
        return n
    b = target - target % unit
    while n % b:
        b -= unit
    return b


def _sigmoid(x):
    return 1.0 / (1.0 + jnp.exp(-x))


def _dot(a, b):
    return jnp.dot(a, b, preferred_element_type=F32)


def _dot_nt(a, b):
    return lax.dot_general(a, b, (((1,), (1,)), ((), ())), preferred_element_type=F32)


def _dot_tn(a, b):
    return lax.dot_general(a, b, (((0,), (0,)), ((), ())), preferred_element_type=F32)


def _mm_body(x_ref, w_ref, o_ref, *, act):
    acc = _dot(x_ref[...], w_ref[...])
    if act == "tanh":
        acc = jnp.tanh(acc)
    elif act == "sigmoid":
        acc = _sigmoid(acc)
    o_ref[...] = acc.astype(o_ref.dtype)


def matmul(x, w, out_dtype, *, act=None, lead=None, bm=1024, bn=512, name="matmul"):
    M, K = x.shape[-2:]
    N = w.shape[1]
    bm, bn = _block(M, bm, SUBLANES), _block(N, bn, LANES)
    assert M % bm == 0 and N % bn == 0
    if lead is None:
        x_spec = pl.BlockSpec((bm, K), lambda i, j: (i, 0))
    else:
        x_spec = pl.BlockSpec((None, bm, K), lambda i, j: (lead, i, 0))
    return pl.pallas_call(
        functools.partial(_mm_body, act=act),
        grid=(M // bm, N // bn),
        in_specs=[x_spec, pl.BlockSpec((K, bn), lambda i, j: (0, j))],
        out_specs=pl.BlockSpec((bm, bn), lambda i, j: (i, j)),
        out_shape=jax.ShapeDtypeStruct((M, N), out_dtype),
        compiler_params=_params("parallel", "parallel"),
        name=name,
    )(x, w)


def _bmm_body(x_ref, w_ref, o_ref):
    o_ref[...] = _dot(x_ref[...], w_ref[...]).astype(o_ref.dtype)


def batched_matmul(x, w, out_dtype, *, nbatch, bm=1024, bn=512, name="bmm"):
    M, K = x.shape[-2:]
    N = w.shape[2]
    bm, bn = _block(M, bm, SUBLANES), _block(N, bn, LANES)
    assert M % bm == 0 and N % bn == 0
    return pl.pallas_call(
        _bmm_body,
        grid=(nbatch, M // bm, N // bn),
        in_specs=[pl.BlockSpec((None, bm, K), lambda p, i, j: (p, i, 0)),
                  pl.BlockSpec((None, K, bn), lambda p, i, j: (p, 0, j))],
        out_specs=pl.BlockSpec((None, bm, bn), lambda p, i, j: (p, i, j)),
        out_shape=jax.ShapeDtypeStruct((nbatch, M, N), out_dtype),
        compiler_params=_params("parallel", "parallel", "parallel"),
        name=name,
    )(x, w)


def _ln_body(x_ref, h_ref, g_ref, b_ref, of_ref, ob_ref):
    y = DEEPNORM_ALPHA * x_ref[...] + h_ref[...]
    mu = jnp.mean(y, axis=-1, keepdims=True)
    d = y - mu
    var = jnp.mean(d * d, axis=-1, keepdims=True)
    out = d * lax.rsqrt(var + LN_EPS) * g_ref[...] + b_ref[...]
    of_ref[...] = out
    ob_ref[...] = out.astype(BF16)


def ln_residual(x, h, g, b, *, br=256):
    T, D = x.shape
    br = min(br, T)
    assert T % br == 0
    row = pl.BlockSpec((br, D), lambda i: (i, 0))
    vec = pl.BlockSpec((1, D), lambda i: (0, 0))
    return pl.pallas_call(
        _ln_body,
        grid=(T // br,),
        in_specs=[row, row, vec, vec],
        out_specs=[row, row],
        out_shape=[jax.ShapeDtypeStruct((T, D), F32), jax.ShapeDtypeStruct((T, D), BF16)],
        compiler_params=_params("parallel"),
        name="ln_residual",
    )(x, h, g.reshape(1, D), b.reshape(1, D))


def _mix_body(x_ref, xp_ref, mix_ref, o_ref, *, blocks_per_seq):
    i = pl.program_id(0)
    x = x_ref[...]
    prev_last = xp_ref[SUBLANES - 1:SUBLANES, :]
    prev_last = jnp.where(i % blocks_per_seq == 0, 0.0, prev_last)
    row = lax.broadcasted_iota(jnp.int32, x.shape, 0)
    shifted = jnp.where(row == 0, prev_last, pltpu.roll(x, 1, 0))
    xx = shifted - x
    for p in range(6):
        o_ref[p] = (x + xx * mix_ref[p:p + 1, :]).astype(BF16)


def token_shift_mix(x, mix, seq, *, bt=512, bd=1024):
    T, D = x.shape
    bt, bd = min(bt, seq), min(bd, D)
    assert seq % bt == 0 and D % bd == 0 and bt % SUBLANES == 0
    halo = bt // SUBLANES
    return pl.pallas_call(
        functools.partial(_mix_body, blocks_per_seq=seq // bt),
        grid=(T // bt, D // bd),
        in_specs=[pl.BlockSpec((bt, bd), lambda i, j: (i, j)),
                  pl.BlockSpec((SUBLANES, bd), lambda i, j: (jnp.maximum(i * halo - 1, 0), j)),
                  pl.BlockSpec((6, bd), lambda i, j: (0, j))],
        out_specs=pl.BlockSpec((6, bt, bd), lambda i, j: (0, i, j)),
        out_shape=jax.ShapeDtypeStruct((6, T, D), BF16),
        compiler_params=_params("parallel", "parallel"),
        name="rwkv_mix",
    )(x, x, mix)


def _split_dot(x, e):
    hi = x.astype(BF16)
    lo = (x - hi.astype(F32)).astype(BF16)
    return _dot(hi, e) + _dot(lo, e)


def _scan_body(*refs, n_chunks, has_vlora):
    if has_vlora:
        r_ref, k_ref, v_ref, wl_ref, al_ref, g_ref, vl_ref, vf_ref, p_ref, o_ref, s_ref = refs
    else:
        r_ref, k_ref, v_ref, wl_ref, al_ref, g_ref, p_ref, o_ref, s_ref = refs

    @pl.when(pl.program_id(2) == 0)
    def _():
        s_ref[...] = jnp.zeros_like(s_ref)

    W = MXU_DIM
    head_of_row = lax.broadcasted_iota(jnp.int32, (STACK, W), 0) // CHUNK
    head_of_lane = lax.broadcasted_iota(jnp.int32, (STACK, W), 1) // HEAD
    head_mask = head_of_row == head_of_lane
    rr = lax.broadcasted_iota(jnp.int32, (STACK, STACK), 0)
    cc = lax.broadcasted_iota(jnp.int32, (STACK, STACK), 1)
    same_head = (rr // CHUNK) == (cc // CHUNK)
    strict_lower = same_head & ((rr % CHUNK) > (cc % CHUNK))
    incl_lower = same_head & ((rr % CHUNK) >= (cc % CHUNK))
    eye = (rr == cc).astype(F32)
    lane_r = lax.broadcasted_iota(jnp.int32, (W, W), 0) // HEAD
    lane_c = lax.broadcasted_iota(jnp.int32, (W, W), 1) // HEAD
    seg_ones = (lane_r == lane_c).astype(BF16)
    tr = lax.broadcasted_iota(jnp.int32, (CHUNK, CHUNK), 0)
    tc = lax.broadcasted_iota(jnp.int32, (CHUNK, CHUNK), 1)
    tri = (tr >= tc).astype(BF16)

    w0 = p_ref[0:1, :]
    a0 = p_ref[1:2, :]
    k_k = p_ref[2:3, :]
    k_a = p_ref[3:4, :]
    r_k = p_ref[4:5, :]
    gn_g = p_ref[5:6, :]
    gn_b = p_ref[6:7, :]
    v0 = p_ref[7:8, :]

    def stack(x):
        return jnp.where(head_mask, jnp.concatenate([x] * HEADS_PER_GROUP, axis=0), 0.0)

    state = s_ref[...]
    for c in range(n_chunks):
        rows = pl.ds(c * CHUNK, CHUNK)
        r = r_ref[rows, :]
        k = k_ref[rows, :]
        v = v_ref[rows, :]
        z = w0 + wl_ref[rows, :]
        w_log = -(jnp.maximum(-z, 0.0) + jnp.log(1.0 + jnp.exp(-jnp.abs(z)))) - 0.5
        log_decay = -jnp.exp(w_log)
        a = _sigmoid(a0 + al_ref[rows, :])
        if has_vlora:
            v = v + (vf_ref[rows, :] - v) * _sigmoid(v0 + vl_ref[rows, :])
        kk = k * k_k
        norm = jnp.sqrt(_split_dot(kk * kk, seg_ones))
        kk = kk / jnp.maximum(norm, 1e-12)
        k = k * (1.0 + (a - 1.0) * k_a)

        cum = _split_dot_left(tri, log_decay)
        cum_last = cum[CHUNK - 1:CHUNK, :]
        e_pos = jnp.exp(cum)
        e_neg = jnp.exp(-cum)
        e_prev = jnp.exp(cum - log_decay)
        e_tail = jnp.exp(cum_last - cum)
        kka = kk * a
        a_t = stack(-kk * e_prev)
        r_t = stack(r * e_pos)
        b_t = stack(kka * e_neg)
        k_t = stack(k * e_neg)
        b_end = stack(kka * e_tail)
        k_end = stack(k * e_tail)
        v_s = stack(v)

        ar = jnp.concatenate([a_t, r_t], axis=0).astype(BF16)
        bk = jnp.concatenate([b_t, k_t], axis=0).astype(BF16)
        p = _dot_nt(ar, bk)
        l_ab = jnp.where(strict_lower, p[:STACK, :STACK], 0.0)
        l_ak = jnp.where(strict_lower, p[:STACK, STACK:], 0.0)
        l_rb = jnp.where(incl_lower, p[STACK:, :STACK], 0.0)
        l_rk = jnp.where(incl_lower, p[STACK:, STACK:], 0.0)

        inv = eye + l_ab
        power = l_ab.astype(BF16)
        for _ in range(int(math.log2(CHUNK)) - 1):
            power = _dot(power, power).astype(BF16)
            inv = inv + _dot(inv.astype(BF16), power)

        ar_s = _dot_nt(ar, state.astype(BF16))
        v_sb = v_s.astype(BF16)
        rhs = ar_s[:STACK] + _dot(l_ak.astype(BF16), v_sb)
        u = _dot(inv.astype(BF16), rhs.astype(BF16))
        uv = jnp.concatenate([u, v_s], axis=0).astype(BF16)
        l_r = jnp.concatenate([l_rb, l_rk], axis=1).astype(BF16)
        y_s = ar_s[STACK:] + _dot(l_r, uv)
        y = y_s[0:CHUNK]
        for h in range(1, HEADS_PER_GROUP):
            y = y + y_s[h * CHUNK:(h + 1) * CHUNK]

        bk_end = jnp.concatenate([b_end, k_end], axis=0).astype(BF16)
        state = state * jnp.exp(cum_last) + _dot_tn(uv, bk_end)

        mu = _split_dot(y, seg_ones) * (1.0 / HEAD)
        d = y - mu
        var = _split_dot(d * d, seg_ones) * (1.0 / HEAD)
        yn = d * lax.rsqrt(var + GN_EPS) * gn_g + gn_b
        bonus = _split_dot(r * k * r_k, seg_ones) * v
        o_ref[rows, :] = ((yn + bonus) * g_ref[rows, :]).astype(o_ref.dtype)

    s_ref[...] = state


def _split_dot_left(e, x):
    hi = x.astype(BF16)
    lo = (x - hi.astype(F32)).astype(BF16)
    return _dot(e, hi) + _dot(e, lo)


def rwkv_scan(r, k, v, wl, al, g, vl, vf, chan, seq, *, bt=256):
    T, D = r.shape
    W = MXU_DIM
    bt = min(bt, seq)
    assert D % W == 0 and seq % bt == 0 and bt % CHUNK == 0
    has_vlora = vl is not None
    nb = seq // bt
    tok = pl.BlockSpec((bt, W), lambda b, gidx, t: (b * nb + t, gidx))
    ins = [r, k, v, wl, al, g] + ([vl, vf] if has_vlora else []) + [chan]
    return pl.pallas_call(
        functools.partial(_scan_body, n_chunks=bt // CHUNK, has_vlora=has_vlora),
        grid=(T // seq, D // W, nb),
        in_specs=[tok] * (len(ins) - 1) + [pl.BlockSpec((SUBLANES, W), lambda b, gidx, t: (0, gidx))],
        out_specs=tok,
        out_shape=jax.ShapeDtypeStruct((T, D), BF16),
        scratch_shapes=[pltpu.VMEM((W, W), F32)],
        compiler_params=_params("parallel", "parallel", "arbitrary"),
        name="rwkv_scan",
    )(*ins)


def _bias_body(bmap_ref, rb_ref, o_ref):
    h = pl.program_id(0)
    bmap = bmap_ref[...]
    acc = jnp.zeros(bmap.shape, F32)
    for b in range(REL_BUCKETS):
        acc = jnp.where(bmap == b, rb_ref[b, h], acc)
    o_ref[0] = acc


def bias_table(rel_bias, bmap):
    H = rel_bias.shape[1]
    return pl.pallas_call(
        _bias_body,
        grid=(H,),
        in_specs=[pl.BlockSpec((BLOCK, 2 * BLOCK), lambda h: (0, 0)),
                  pl.BlockSpec(memory_space=pltpu.SMEM)],
        out_specs=pl.BlockSpec((1, BLOCK, 2 * BLOCK), lambda h: (h, 0, 0)),
        out_shape=jax.ShapeDtypeStruct((H, BLOCK, 2 * BLOCK), F32),
        compiler_params=_params("parallel"),
        name="rel_bias_table",
    )(bmap, rel_bias)


def _attn_body(q_ref, kp_ref, kc_ref, vp_ref, vc_ref, bias_ref, sink_ref, o_ref, *, group):
    n = pl.program_id(1)
    kvh = pl.program_id(2)
    low_kv = (kvh % 2) == 0
    kband = jnp.concatenate([kp_ref[...], kc_ref[...]], axis=0)
    vband = jnp.concatenate([vp_ref[...], vc_ref[...]], axis=0)
    kswap = pltpu.roll(kband, HEAD, 1)
    vswap = pltpu.roll(vband, HEAD, 1)
    k_lo = jnp.where(low_kv, kband, kswap)
    k_hi = jnp.where(low_kv, kswap, kband)
    v_lo = jnp.where(low_kv, vband, vswap)
    v_hi = jnp.where(low_kv, vswap, vband)

    low = lax.broadcasted_iota(jnp.int32, (BLOCK, LANES), 1) < HEAD
    qi = lax.broadcasted_iota(jnp.int32, (BLOCK, 2 * BLOCK), 0)
    kj = lax.broadcasted_iota(jnp.int32, (BLOCK, 2 * BLOCK), 1)
    dist = qi + BLOCK - kj
    valid = (dist >= 0) & (dist < WINDOW) & ((n * BLOCK - BLOCK + kj) >= 0)
    scale = HEAD ** -0.5

    for pair in range(group // 2):
        qp = q_ref[:, pair * LANES:(pair + 1) * LANES]
        halves = []
        for half in range(2):
            g = 2 * pair + half
            in_half = low if half == 0 else jnp.logical_not(low)
            qm = jnp.where(in_half, qp, jnp.zeros_like(qp))
            s = _dot_nt(qm, k_lo if half == 0 else k_hi) * scale + bias_ref[g]
            s = jnp.where(valid, s, -jnp.inf)
            sink = sink_ref[kvh * group + g]
            m = jnp.maximum(jnp.max(s, axis=-1, keepdims=True), sink)
            p = jnp.exp(s - m)
            denom = jnp.sum(p, axis=-1, keepdims=True) + jnp.exp(sink - m)
            halves.append(_dot(p.astype(BF16), v_lo if half == 0 else v_hi) / denom)
        o_ref[:, pair * LANES:(pair + 1) * LANES] = jnp.where(low, halves[0], halves[1]).astype(o_ref.dtype)


def swa_attention(qkv, bias, sinks, seq, n_q_heads, n_kv_heads):
    T = qkv.shape[0]
    group = n_q_heads // n_kv_heads
    assert group % 2 == 0 and n_kv_heads % 2 == 0 and seq % BLOCK == 0
    nb = seq // BLOCK
    gw = group * HEAD
    k_col0 = n_q_heads * HEAD // LANES
    v_col0 = k_col0 + n_kv_heads * HEAD // LANES

    def cur(col0):
        return pl.BlockSpec((BLOCK, LANES), lambda b, n, h: (b * nb + n, col0 + h // 2))

    def prev(col0):
        return pl.BlockSpec((BLOCK, LANES), lambda b, n, h: (b * nb + jnp.maximum(n - 1, 0), col0 + h // 2))

    return pl.pallas_call(
        functools.partial(_attn_body, group=group),
        grid=(T // seq, nb, n_kv_heads),
        in_specs=[pl.BlockSpec((BLOCK, gw), lambda b, n, h: (b * nb + n, h)),
                  prev(k_col0), cur(k_col0), prev(v_col0), cur(v_col0),
                  pl.BlockSpec((group, BLOCK, 2 * BLOCK), lambda b, n, h: (h, 0, 0)),
                  pl.BlockSpec(memory_space=pltpu.SMEM)],
        out_specs=pl.BlockSpec((BLOCK, gw), lambda b, n, h: (b * nb + n, h)),
        out_shape=jax.ShapeDtypeStruct((T, n_q_heads * HEAD), BF16),
        compiler_params=_params("parallel", "parallel", "parallel"),
        name="swa_attention",
    )(qkv, qkv, qkv, qkv, qkv, bias, sinks)


def _bucket_map():
    qi = jnp.arange(BLOCK)[:, None]
    kj = jnp.arange(2 * BLOCK)[None, :]
    dist = jnp.maximum(qi + BLOCK - kj, 0)
    max_exact = REL_BUCKETS // 2
    d_f = jnp.maximum(dist, max_exact).astype(F32)
    large = max_exact + (jnp.log(d_f / max_exact) / math.log(REL_MAX_DIST / max_exact)
                         * (REL_BUCKETS - max_exact)).astype(jnp.int32)
    large = jnp.minimum(large, REL_BUCKETS - 1)
    return jnp.where(dist < max_exact, dist, large).astype(jnp.int32)


def _ffn_up_body(x_ref, xh_ref, wg_ref, wu_ref, cg_ref, cu_ref, bg_ref, bu_ref, o_ref, *, blocks_per_seq):
    first = pl.program_id(0) % blocks_per_seq == 0
    x = x_ref[...]
    xh = xh_ref[...]
    row = lax.broadcasted_iota(jnp.int32, (x.shape[0], wg_ref.shape[1]), 0)

    def conv(w_ref, c_ref, b_ref):
        h = _dot(x, w_ref[...])
        halo = jnp.where(first, 0.0, _dot(xh, w_ref[...]))
        h1 = jnp.where(row >= 1, pltpu.roll(h, 1, 0), halo[SUBLANES - 1:SUBLANES, :])
        h2 = jnp.where(row >= 2, pltpu.roll(h, 2, 0),
                       jnp.where(row == 0, halo[SUBLANES - 2:SUBLANES - 1, :], halo[SUBLANES - 1:SUBLANES, :]))
        return c_ref[0:1, :] * h2 + c_ref[1:2, :] * h1 + c_ref[2:3, :] * h + b_ref[...]

    gate = conv(wg_ref, cg_ref, bg_ref)
    up = conv(wu_ref, cu_ref, bu_ref)
    o_ref[...] = (gate * _sigmoid(gate) * up).astype(o_ref.dtype)


def ffn_up(x, w_up, conv_w, conv_b, seq, *, bm=1024, bn=256):
    T, D = x.shape
    F = w_up.shape[1] // 2
    bm, bn = min(bm, seq), min(bn, F)
    assert seq % bm == 0 and F % bn == 0 and bm % SUBLANES == 0
    nf = F // bn
    halo = bm // SUBLANES
    return pl.pallas_call(
        functools.partial(_ffn_up_body, blocks_per_seq=seq // bm),
        grid=(T // bm, nf),
        in_specs=[pl.BlockSpec((bm, D), lambda i, j: (i, 0)),
                  pl.BlockSpec((SUBLANES, D), lambda i, j: (jnp.maximum(i * halo - 1, 0), 0)),
                  pl.BlockSpec((D, bn), lambda i, j: (0, j)),
                  pl.BlockSpec((D, bn), lambda i, j: (0, j + nf)),
                  pl.BlockSpec((CONV_W, bn), lambda i, j: (0, j)),
                  pl.BlockSpec((CONV_W, bn), lambda i, j: (0, j + nf)),
                  pl.BlockSpec((1, bn), lambda i, j: (0, j)),
                  pl.BlockSpec((1, bn), lambda i, j: (0, j + nf))],
        out_specs=pl.BlockSpec((bm, bn), lambda i, j: (i, j)),
        out_shape=jax.ShapeDtypeStruct((T, F), BF16),
        compiler_params=_params("parallel", "parallel"),
        name="ffn_up",
    )(x, x, w_up, w_up, conv_w, conv_w, conv_b.reshape(1, -1), conv_b.reshape(1, -1))


def _pad_to(x, axis, mult):
    pad = (-x.shape[axis]) % mult
    if pad == 0:
        return x
    widths = [(0, 0)] * x.ndim
    widths[axis] = (0, pad)
    return jnp.pad(x, widths)


def _lora(xs, lead, w_down, w_up, act, name):
    w_down = _pad_to(w_down, 1, LANES).astype(BF16)
    w_up = _pad_to(w_up, 0, LANES).astype(BF16)
    mid = matmul(xs, w_down, BF16, act=act, lead=lead, name=name + "_down")
    return matmul(mid, w_up, F32, name=name + "_up")


def rwkv_layer(x, seq, v_first, mix, w_rkv, w_o, w0, w1, w2, a0, a1, a2, g1, g2,
               k_k, k_a, r_k, gn_g, gn_b, v_lora):
    xs = token_shift_mix(x, mix[jnp.array([0, 2, 3, 1, 4, 5])], seq)
    rkv = batched_matmul(xs, w_rkv.astype(BF16), F32, nbatch=3, name="rwkv_rkv")
    r, k, v = rkv[0], rkv[1], rkv[2]
    wl = _lora(xs, 3, w1, w2, "tanh", "rwkv_w")
    al = _lora(xs, 4, a1, a2, None, "rwkv_a")
    g = _lora(xs, 5, g1, g2, "sigmoid", "rwkv_g")
    zeros = jnp.zeros_like(w0)
    if v_lora is None:
        v_first = v
        vl = vf = None
        v0 = zeros
    else:
        v0, v1, v2 = v_lora
        vl = _lora(xs, 2, v1, v2, None, "rwkv_v")
        vf = v_first
    chan = jnp.stack([w0, a0, k_k, k_a, r_k, gn_g, gn_b, v0])
    y = rwkv_scan(r, k, v, wl, al, g, vl, vf, chan, seq)
    return matmul(y, w_o.astype(BF16), F32, name="rwkv_out"), v_first


def attn_layer(xb, seq, w_qkv, w_o, sinks, bias):
    n_q = w_o.shape[0] // HEAD
    n_kv = (w_qkv.shape[1] // HEAD - n_q) // 2
    qkv = matmul(xb, w_qkv.astype(BF16), BF16, name="attn_qkv")
    o = swa_attention(qkv, bias, sinks, seq, n_q, n_kv)
    return matmul(o, w_o.astype(BF16), F32, name="attn_out")


def ffn_layer(xb, seq, w_up, conv_w, conv_b, w_down):
    act = ffn_up(xb, w_up.astype(BF16), conv_w, conv_b, seq)
    return matmul(act, w_down.astype(BF16), F32, bm=512, bn=512, name="ffn_down")


def kernel(x, rwkv_mix, rwkv_w_rkv, rwkv_w_o, rwkv_w0, rwkv_w1, rwkv_w2, rwkv_a0, rwkv_a1, rwkv_a2, rwkv_g1, rwkv_g2, rwkv_k_k, rwkv_k_a, rwkv_r_k, rwkv_gn_g, rwkv_gn_b, rwkv_v0, rwkv_v1, rwkv_v2, attn_w_qkv, attn_w_o, attn_sinks, rel_bias, ffn_w_up, ffn_conv_w, ffn_conv_b, ffn_w_down, ln1_g, ln1_b, ln2_g, ln2_b):
    B, S, D = x.shape
    depth = ln1_g.shape[0]
    xf = x.reshape(B * S, D)
    xb = xf.astype(BF16)
    bias = bias_table(rel_bias, _bucket_map())
    v_first = None
    for i in range(depth):
        j = i // 2
        if i % 2 == 0:
            v_lora = None if j == 0 else (rwkv_v0[j - 1], rwkv_v1[j - 1], rwkv_v2[j - 1])
            h, v_first = rwkv_layer(xf, S, v_first, rwkv_mix[j], rwkv_w_rkv[j],
                                    rwkv_w_o[j], rwkv_w0[j], rwkv_w1[j], rwkv_w2[j], rwkv_a0[j],
                                    rwkv_a1[j], rwkv_a2[j], rwkv_g1[j], rwkv_g2[j], rwkv_k_k[j],
                                    rwkv_k_a[j], rwkv_r_k[j], rwkv_gn_g[j], rwkv_gn_b[j], v_lora)
        else:
            h = attn_layer(xb, S, attn_w_qkv[j], attn_w_o[j], attn_sinks[j], bias)
        xf, xb = ln_residual(xf, h, ln1_g[i], ln1_b[i])
        f = ffn_layer(xb, S, ffn_w_up[i], ffn_conv_w[i], ffn_conv_b[i], ffn_w_down[i])
        xf, xb = ln_residual(xf, f, ln2_g[i], ln2_b[i])
    return xf.reshape(B, S, D)
```

```python
import functools
import math

import jax
import jax.numpy as jnp
from jax import lax
from jax.experimental import pallas as pl
from jax.experimental.pallas import tpu as pltpu

F32 = jnp.float32
BF16 = jnp.bfloat16

LANES = 128
SUBLANES = 8
MXU_DIM = 256
VMEM_LIMIT_BYTES = 56 * 1024 * 1024

HEAD = 64
HEADS_PER_GROUP = MXU_DIM // HEAD
CHUNK = 64
STACK = HEADS_PER_GROUP * CHUNK

WINDOW = 128
BLOCK = 128
REL_BUCKETS = 32
REL_MAX_DIST = 128
CONV_W = 3
LN_EPS = 1e-5
GN_EPS = HEAD * 1e-5
DEPTH = 4
DEEPNORM_ALPHA = (2 * DEPTH) ** 0.25


def _params(*sem):
    return pltpu.CompilerParams(dimension_semantics=sem, vmem_limit_bytes=VMEM_LIMIT_BYTES)


def _block(n, target, unit):
    if n <= target:
        return n
    b = target - target % unit
    while n % b:
        b -= unit
    return b


def _sigmoid(x):
    return 1.0 / (1.0 + jnp.exp(-x))


def _dot(a, b):
    return jnp.dot(a, b, preferred_element_type=F32)


def _dot_nt(a, b):
    return lax.dot_general(a, b, (((1,), (1,)), ((), ())), preferred_element_type=F32)


def _dot_tn(a, b):
    return lax.dot_general(a, b, (((0,), (0,)), ((), ())), preferred_element_type=F32)


def _mm_body(x_ref, w_ref, o_ref, *, act):
    acc = _dot(x_ref[...], w_ref[...])
    if act == "tanh":
        acc = jnp.tanh(acc)
    elif act == "sigmoid":
        acc = _sigmoid(acc)
    o_ref[...] = acc.astype(o_ref.dtype)


def matmul(x, w, out_dtype, *, act=None, lead=None, bm=1024, bn=512, name="matmul"):
    M, K = x.shape[-2:]
    N = w.shape[1]
    bm, bn = _block(M, bm, SUBLANES), _block(N, bn, LANES)
    assert M % bm == 0 and N % bn == 0
    if lead is None:
        x_spec = pl.BlockSpec((bm, K), lambda i, j: (i, 0))
    else:
        x_spec = pl.BlockSpec((None, bm, K), lambda i, j: (lead, i, 0))
    return pl.pallas_call(
        functools.partial(_mm_body, act=act),
        grid=(M // bm, N // bn),
        in_specs=[x_spec, pl.BlockSpec((K, bn), lambda i, j: (0, j))],
        out_specs=pl.BlockSpec((bm, bn), lambda i, j: (i, j)),
        out_shape=jax.ShapeDtypeStruct((M, N), out_dtype),
        compiler_params=_params("parallel", "parallel"),
        name=name,
    )(x, w)


def _bmm_body(x_ref, w_ref, o_ref):
    o_ref[...] = _dot(x_ref[...], w_ref[...]).astype(o_ref.dtype)


def batched_matmul(x, w, out_dtype, *, nbatch, bm=1024, bn=512, name="bmm"):
    M, K = x.shape[-2:]
    N = w.shape[2]
    bm, bn = _block(M, bm, SUBLANES), _block(N, bn, LANES)
    assert M % bm == 0 and N % bn == 0
    return pl.pallas_call(
        _bmm_body,
        grid=(nbatch, M // bm, N // bn),
        in_specs=[pl.BlockSpec((None, bm, K), lambda p, i, j: (p, i, 0)),
                  pl.BlockSpec((None, K, bn), lambda p, i, j: (p, 0, j))],
        out_specs=pl.BlockSpec((None, bm, bn), lambda p, i, j: (p, i, j)),
        out_shape=jax.ShapeDtypeStruct((nbatch, M, N), out_dtype),
        compiler_params=_params("parallel", "parallel", "parallel"),
        name=name,
    )(x, w)


def _ln_body(x_ref, h_ref, g_ref, b_ref, of_ref, ob_ref):
    y = DEEPNORM_ALPHA * x_ref[...] + h_ref[...]
    mu = jnp.mean(y, axis=-1, keepdims=True)
    d = y - mu
    var = jnp.mean(d * d, axis=-1, keepdims=True)
    out = d * lax.rsqrt(var + LN_EPS) * g_ref[...] + b_ref[...]
    of_ref[...] = out
    ob_ref[...] = out.astype(BF16)


def ln_residual(x, h, g, b, *, br=256):
    T, D = x.shape
    br = min(br, T)
    assert T % br == 0
    row = pl.BlockSpec((br, D), lambda i: (i, 0))
    vec = pl.BlockSpec((1, D), lambda i: (0, 0))
    return pl.pallas_call(
        _ln_body,
        grid=(T // br,),
        in_specs=[row, row, vec, vec],
        out_specs=[row, row],
        out_shape=[jax.ShapeDtypeStruct((T, D), F32), jax.ShapeDtypeStruct((T, D), BF16)],
        compiler_params=_params("parallel"),
        name="ln_residual",
    )(x, h, g.reshape(1, D), b.reshape(1, D))


def _mix_body(x_ref, xp_ref, mix_ref, o_ref, *, blocks_per_seq):
    i = pl.program_id(0)
    x = x_ref[...]
    prev_last = xp_ref[SUBLANES - 1:SUBLANES, :]
    prev_last = jnp.where(i % blocks_per_seq == 0, 0.0, prev_last)
    row = lax.broadcasted_iota(jnp.int32, x.shape, 0)
    shifted = jnp.where(row == 0, prev_last, pltpu.roll(x, 1, 0))
    xx = shifted - x
    for p in range(6):
        o_ref[p] = (x + xx * mix_ref[p:p + 1, :]).astype(BF16)


def token_shift_mix(x, mix, seq, *, bt=512, bd=1024):
    T, D = x.shape
    bt, bd = min(bt, seq), min(bd, D)
    assert seq % bt == 0 and D % bd == 0 and bt % SUBLANES == 0
    halo = bt // SUBLANES
    return pl.pallas_call(
        functools.partial(_mix_body, blocks_per_seq=seq // bt),
        grid=(T // bt, D // bd),
        in_specs=[pl.BlockSpec((bt, bd), lambda i, j: (i, j)),
                  pl.BlockSpec((SUBLANES, bd), lambda i, j: (jnp.maximum(i * halo - 1, 0), j)),
                  pl.BlockSpec((6, bd), lambda i, j: (0, j))],
        out_specs=pl.BlockSpec((6, bt, bd), lambda i, j: (0, i, j)),
        out_shape=jax.ShapeDtypeStruct((6, T, D), BF16),
        compiler_params=_params("parallel", "parallel"),
        name="rwkv_mix",
    )(x, x, mix)


def _scan_body(*refs, n_chunks, has_vlora):
    if has_vlora:
        r_ref, k_ref, v_ref, wl_ref, al_ref, g_ref, vl_ref, vf_ref, p_ref, o_ref, s_ref = refs
    else:
        r_ref, k_ref, v_ref, wl_ref, al_ref, g_ref, p_ref, o_ref, s_ref = refs

    @pl.when(pl.program_id(2) == 0)
    def _():
        s_ref[...] = jnp.zeros_like(s_ref)

    W = MXU_DIM
    head_of_row = lax.broadcasted_iota(jnp.int32, (STACK, W), 0) // CHUNK
    head_of_lane = lax.broadcasted_iota(jnp.int32, (STACK, W), 1) // HEAD
    head_mask = head_of_row == head_of_lane
    rr = lax.broadcasted_iota(jnp.int32, (STACK, STACK), 0)
    cc = lax.broadcasted_iota(jnp.int32, (STACK, STACK), 1)
    same_head = (rr // CHUNK) == (cc // CHUNK)
    strict_lower = same_head & ((rr % CHUNK) > (cc % CHUNK))
    incl_lower = same_head & ((rr % CHUNK) >= (cc % CHUNK))
    eye = (rr == cc).astype(F32)
    lane_r = lax.broadcasted_iota(jnp.int32, (W, W), 0) // HEAD
    lane_c = lax.broadcasted_iota(jnp.int32, (W, W), 1) // HEAD
    seg_ones = (lane_r == lane_c).astype(BF16)
    bt = n_chunks * CHUNK
    tr = lax.broadcasted_iota(jnp.int32, (bt, bt), 0)
    tc = lax.broadcasted_iota(jnp.int32, (bt, bt), 1)
    tri = ((tr // CHUNK == tc // CHUNK) & (tr >= tc)).astype(BF16)

    w0 = p_ref[0:1, :]
    a0 = p_ref[1:2, :]
    k_k = p_ref[2:3, :]
    k_a = p_ref[3:4, :]
    r_k = p_ref[4:5, :]
    gn_g = p_ref[5:6, :]
    gn_b = p_ref[6:7, :]
    v0 = p_ref[7:8, :]

    nc = n_chunks
    chunks = range(nc)

    r = r_ref[...]
    k = k_ref[...]
    v = v_ref[...]
    z = w0 + wl_ref[...]
    w_log = -(jnp.maximum(-z, 0.0) + jnp.log(1.0 + jnp.exp(-jnp.abs(z)))) - 0.5
    log_decay = -jnp.exp(w_log)
    a = _sigmoid(a0 + al_ref[...])
    if has_vlora:
        v = v + (vf_ref[...] - v) * _sigmoid(v0 + vl_ref[...])
    kk = k * k_k
    norm = jnp.sqrt(_dot((kk * kk).astype(BF16), seg_ones))
    kk = kk / jnp.maximum(norm, 1e-12)
    k = k * (1.0 + (a - 1.0) * k_a)

    cum = _split_dot_left(tri, log_decay)
    cum_last = jnp.concatenate(
        [jnp.broadcast_to(cum[(c + 1) * CHUNK - 1:(c + 1) * CHUNK, :], (CHUNK, W)) for c in chunks], axis=0)
    e_pos = jnp.exp(cum)
    e_neg = jnp.exp(-cum)
    e_prev = jnp.exp(cum - log_decay)
    e_tail = jnp.exp(cum_last - cum)
    kka = kk * a
    a_e = -kk * e_prev
    r_e = r * e_pos
    b_e = kka * e_neg
    k_e = k * e_neg
    b_end = kka * e_tail
    k_end = k * e_tail

    def stack(x, c):
        xc = x[c * CHUNK:(c + 1) * CHUNK]
        return jnp.where(head_mask, jnp.concatenate([xc] * HEADS_PER_GROUP, axis=0), 0.0)

    r_s = [stack(r_e, c) for c in chunks]
    a_sb = [stack(a_e, c).astype(BF16) for c in chunks]
    ar = [jnp.concatenate([a_sb[c], r_s[c].astype(BF16)], axis=0) for c in chunks]
    bk = [jnp.concatenate([stack(b_e, c), stack(k_e, c)], axis=0).astype(BF16) for c in chunks]
    p = [_dot_nt(ar[c], bk[c]) for c in chunks]
    l_ab = [jnp.where(strict_lower, p[c][:STACK, :STACK], 0.0) for c in chunks]
    l_ak = [jnp.where(strict_lower, p[c][:STACK, STACK:], 0.0).astype(BF16) for c in chunks]
    l_rb = [jnp.where(incl_lower, p[c][STACK:, :STACK], 0.0).astype(BF16) for c in chunks]
    l_rk = [jnp.where(incl_lower, p[c][STACK:, STACK:], 0.0).astype(BF16) for c in chunks]

    inv = [eye + l_ab[c] for c in chunks]
    power = [l_ab[c].astype(BF16) for c in chunks]
    for _ in range(int(math.log2(CHUNK)) - 1):
        power = [_dot(power[c], power[c]).astype(BF16) for c in chunks]
        inv = [inv[c] + _dot(inv[c].astype(BF16), power[c]) for c in chunks]

    v_sb = [stack(v, c).astype(BF16) for c in chunks]
    x = [jnp.concatenate([a_sb[c], _dot(l_ak[c], v_sb[c]).astype(BF16)], axis=1) for c in chunks]
    wu = [_dot(inv[c].astype(BF16), x[c]).astype(BF16) for c in chunks]
    ry = [_dot(l_rb[c], wu[c]) for c in chunks]
    r_eff = [(r_s[c] + ry[c][:, :W]).astype(BF16) for c in chunks]
    y0 = [ry[c][:, W:] + _dot(l_rk[c], v_sb[c]) for c in chunks]
    gn = [_dot_tn(wu[c], stack(b_end, c).astype(BF16)) for c in chunks]
    g_mat = [gn[c][:W].astype(BF16) for c in chunks]
    n_mat = [gn[c][W:] + _dot_tn(v_sb[c], stack(k_end, c).astype(BF16)) for c in chunks]
    decay_end = [jnp.exp(cum[(c + 1) * CHUNK - 1:(c + 1) * CHUNK, :]) for c in chunks]

    state = s_ref[...]
    ys = []
    for c in chunks:
        sb = state.astype(BF16)
        y_s = _dot_nt(r_eff[c], sb) + y0[c]
        y = y_s[0:CHUNK]
        for h in range(1, HEADS_PER_GROUP):
            y = y + y_s[h * CHUNK:(h + 1) * CHUNK]
        ys.append(y)
        state = state * decay_end[c] + _dot(sb, g_mat[c]) + n_mat[c]
    s_ref[...] = state
    y = jnp.concatenate(ys, axis=0)

    mu = _dot(y.astype(BF16), seg_ones) * (1.0 / HEAD)
    d = y - mu
    var = _dot((d * d).astype(BF16), seg_ones) * (1.0 / HEAD)
    yn = d * lax.rsqrt(var + GN_EPS) * gn_g + gn_b
    bonus = _dot((r * k * r_k).astype(BF16), seg_ones) * v
    o_ref[...] = ((yn + bonus) * g_ref[...]).astype(o_ref.dtype)


def _split_dot_left(e, x):
    hi = x.astype(BF16)
    lo = (x - hi.astype(F32)).astype(BF16)
    return _dot(e, hi) + _dot(e, lo)


def rwkv_scan(r, k, v, wl, al, g, vl, vf, chan, seq, *, bt=256):
    T, D = r.shape
    W = MXU_DIM
    bt = min(bt, seq)
    assert D % W == 0 and seq % bt == 0 and bt % CHUNK == 0
    has_vlora = vl is not None
    nb = seq // bt
    tok = pl.BlockSpec((bt, W), lambda b, gidx, t: (b * nb + t, gidx))
    ins = [r, k, v, wl, al, g] + ([vl, vf] if has_vlora else []) + [chan]
    return pl.pallas_call(
        functools.partial(_scan_body, n_chunks=bt // CHUNK, has_vlora=has_vlora),
        grid=(T // seq, D // W, nb),
        in_specs=[tok] * (len(ins) - 1) + [pl.BlockSpec((SUBLANES, W), lambda b, gidx, t: (0, gidx))],
        out_specs=tok,
        out_shape=jax.ShapeDtypeStruct((T, D), BF16),
        scratch_shapes=[pltpu.VMEM((W, W), F32)],
        compiler_params=_params("parallel", "parallel", "arbitrary"),
        name="rwkv_scan",
    )(*ins)


def _bias_body(bmap_ref, rb_ref, o_ref):
    h = pl.program_id(0)
    bmap = bmap_ref[...]
    acc = jnp.zeros(bmap.shape, F32)
    for b in range(REL_BUCKETS):
        acc = jnp.where(bmap == b, rb_ref[b, h], acc)
    o_ref[0] = acc


def bias_table(rel_bias, bmap):
    H = rel_bias.shape[1]
    return pl.pallas_call(
        _bias_body,
        grid=(H,),
        in_specs=[pl.BlockSpec((BLOCK, 2 * BLOCK), lambda h: (0, 0)),
                  pl.BlockSpec(memory_space=pltpu.SMEM)],
        out_specs=pl.BlockSpec((1, BLOCK, 2 * BLOCK), lambda h: (h, 0, 0)),
        out_shape=jax.ShapeDtypeStruct((H, BLOCK, 2 * BLOCK), F32),
        compiler_params=_params("parallel"),
        name="rel_bias_table",
    )(bmap, rel_bias)


def _attn_body(q_ref, kp_ref, kc_ref, vp_ref, vc_ref, bias_ref, sink_ref, o_ref, *, group):
    n = pl.program_id(1)
    kvh = pl.program_id(2)
    low_kv = (kvh % 2) == 0
    kband = jnp.concatenate([kp_ref[...], kc_ref[...]], axis=0)
    vband = jnp.concatenate([vp_ref[...], vc_ref[...]], axis=0)
    kswap = pltpu.roll(kband, HEAD, 1)
    vswap = pltpu.roll(vband, HEAD, 1)
    k_lo = jnp.where(low_kv, kband, kswap)
    k_hi = jnp.where(low_kv, kswap, kband)
    v_lo = jnp.where(low_kv, vband, vswap)
    v_hi = jnp.where(low_kv, vswap, vband)

    low = lax.broadcasted_iota(jnp.int32, (BLOCK, LANES), 1) < HEAD
    qi = lax.broadcasted_iota(jnp.int32, (BLOCK, 2 * BLOCK), 0)
    kj = lax.broadcasted_iota(jnp.int32, (BLOCK, 2 * BLOCK), 1)
    dist = qi + BLOCK - kj
    valid = (dist >= 0) & (dist < WINDOW) & ((n * BLOCK - BLOCK + kj) >= 0)
    scale = HEAD ** -0.5

    for pair in range(group // 2):
        qp = q_ref[:, pair * LANES:(pair + 1) * LANES]
        halves = []
        for half in range(2):
            g = 2 * pair + half
            in_half = low if half == 0 else jnp.logical_not(low)
            qm = jnp.where(in_half, qp, jnp.zeros_like(qp))
            s = _dot_nt(qm, k_lo if half == 0 else k_hi) * scale + bias_ref[g]
            s = jnp.where(valid, s, -jnp.inf)
            sink = sink_ref[kvh * group + g]
            m = jnp.maximum(jnp.max(s, axis=-1, keepdims=True), sink)
            p = jnp.exp(s - m)
            denom = jnp.sum(p, axis=-1, keepdims=True) + jnp.exp(sink - m)
            halves.append(_dot(p.astype(BF16), v_lo if half == 0 else v_hi) / denom)
        o_ref[:, pair * LANES:(pair + 1) * LANES] = jnp.where(low, halves[0], halves[1]).astype(o_ref.dtype)


def swa_attention(qkv, bias, sinks, seq, n_q_heads, n_kv_heads):
    T = qkv.shape[0]
    group = n_q_heads // n_kv_heads
    assert group % 2 == 0 and n_kv_heads % 2 == 0 and seq % BLOCK == 0
    nb = seq // BLOCK
    gw = group * HEAD
    k_col0 = n_q_heads * HEAD // LANES
    v_col0 = k_col0 + n_kv_heads * HEAD // LANES

    def cur(col0):
        return pl.BlockSpec((BLOCK, LANES), lambda b, n, h: (b * nb + n, col0 + h // 2))

    def prev(col0):
        return pl.BlockSpec((BLOCK, LANES), lambda b, n, h: (b * nb + jnp.maximum(n - 1, 0), col0 + h // 2))

    return pl.pallas_call(
        functools.partial(_attn_body, group=group),
        grid=(T // seq, nb, n_kv_heads),
        in_specs=[pl.BlockSpec((BLOCK, gw), lambda b, n, h: (b * nb + n, h)),
                  prev(k_col0), cur(k_col0), prev(v_col0), cur(v_col0),
                  pl.BlockSpec((group, BLOCK, 2 * BLOCK), lambda b, n, h: (h, 0, 0)),
                  pl.BlockSpec(memory_space=pltpu.SMEM)],
        out_specs=pl.BlockSpec((BLOCK, gw), lambda b, n, h: (b * nb + n, h)),
        out_shape=jax.ShapeDtypeStruct((T, n_q_heads * HEAD), BF16),
        compiler_params=_params("parallel", "parallel", "parallel"),
        name="swa_attention",
    )(qkv, qkv, qkv, qkv, qkv, bias, sinks)


def _bucket_map():
    qi = jnp.arange(BLOCK)[:, None]
    kj = jnp.arange(2 * BLOCK)[None, :]
    dist = jnp.maximum(qi + BLOCK - kj, 0)
    max_exact = REL_BUCKETS // 2
    d_f = jnp.maximum(dist, max_exact).astype(F32)
    large = max_exact + (jnp.log(d_f / max_exact) / math.log(REL_MAX_DIST / max_exact)
                         * (REL_BUCKETS - max_exact)).astype(jnp.int32)
    large = jnp.minimum(large, REL_BUCKETS - 1)
    return jnp.where(dist < max_exact, dist, large).astype(jnp.int32)


def _ffn_up_body(x_ref, wg_ref, wu_ref, cg_ref, cu_ref, bg_ref, bu_ref, o_ref, tail_g_ref, tail_u_ref,
                 raw_g_ref, raw_u_ref, *, blocks_per_seq, sub, rsub):
    first = pl.program_id(0) % blocks_per_seq == 0
    j = pl.program_id(1)
    bm = x_ref.shape[0]
    H = SUBLANES

    @pl.when((pl.program_id(0) == 0) & (j == 0))
    def _():
        tail_g_ref[...] = jnp.zeros_like(tail_g_ref)
        tail_u_ref[...] = jnp.zeros_like(tail_u_ref)

    def matmul_tile(slot, s, rb, halo_g, halo_u):
        cols = pl.ds(s * sub, sub)
        xr = x_ref[pl.ds(rb * rsub, rsub), :]
        raw_g_ref[slot, 0:H, :] = halo_g
        raw_u_ref[slot, 0:H, :] = halo_u
        raw_g_ref[slot, H:, :] = _dot(xr, wg_ref[:, cols])
        raw_u_ref[slot, H:, :] = _dot(xr, wu_ref[:, cols])
        return raw_g_ref[slot, rsub:, :], raw_u_ref[slot, rsub:, :]

    def conv(raw_ref, slot, c_ref, b_ref, cols):
        h = raw_ref[slot, H:, :]
        h1 = raw_ref[slot, H - 1:H - 1 + rsub, :]
        h2 = raw_ref[slot, H - 2:H - 2 + rsub, :]
        return c_ref[0:1, cols] * h2 + c_ref[1:2, cols] * h1 + c_ref[2:3, cols] * h + b_ref[:, cols]

    def finish_tile(slot, s, rb):
        cols = pl.ds(s * sub, sub)
        gate = conv(raw_g_ref, slot, cg_ref, bg_ref, cols)
        up = conv(raw_u_ref, slot, cu_ref, bu_ref, cols)
        o_ref[pl.ds(rb * rsub, rsub), cols] = (gate * _sigmoid(gate) * up).astype(o_ref.dtype)

    n_rows = bm // rsub
    prev = None
    t = 0
    for s in range(wg_ref.shape[1] // sub):
        cols = pl.ds(s * sub, sub)
        halo_g = jnp.where(first, 0.0, tail_g_ref[j, :, cols])
        halo_u = jnp.where(first, 0.0, tail_u_ref[j, :, cols])
        for rb in range(n_rows):
            halo_g, halo_u = matmul_tile(t % 2, s, rb, halo_g, halo_u)
            if prev is not None:
                finish_tile(*prev)
            prev = (t % 2, s, rb)
            t += 1
        tail_g_ref[j, :, cols] = halo_g
        tail_u_ref[j, :, cols] = halo_u
    finish_tile(*prev)


def ffn_up(x, w2, conv_w2, conv_b2, seq, *, bm=1024, bn=512, sub=256, rsub=256):
    T, D = x.shape
    F = w2.shape[1] // 2
    bm, bn = min(bm, seq), min(bn, F)
    sub, rsub = min(sub, bn), min(rsub, bm)
    assert seq % bm == 0 and F % bn == 0 and bn % sub == 0 and bm % rsub == 0 and rsub % SUBLANES == 0
    nf = F // bn
    return pl.pallas_call(
        functools.partial(_ffn_up_body, blocks_per_seq=seq // bm, sub=sub, rsub=rsub),
        grid=(T // bm, nf),
        in_specs=[pl.BlockSpec((bm, D), lambda i, j: (i, 0)),
                  pl.BlockSpec((D, bn), lambda i, j: (0, j)),
                  pl.BlockSpec((D, bn), lambda i, j: (0, j + nf)),
                  pl.BlockSpec((CONV_W, bn), lambda i, j: (0, j)),
                  pl.BlockSpec((CONV_W, bn), lambda i, j: (0, j + nf)),
                  pl.BlockSpec((1, bn), lambda i, j: (0, j)),
                  pl.BlockSpec((1, bn), lambda i, j: (0, j + nf))],
        out_specs=pl.BlockSpec((bm, bn), lambda i, j: (i, j)),
        out_shape=jax.ShapeDtypeStruct((T, F), BF16),
        scratch_shapes=[pltpu.VMEM((nf, SUBLANES, bn), F32), pltpu.VMEM((nf, SUBLANES, bn), F32),
                        pltpu.VMEM((2, SUBLANES + rsub, sub), F32), pltpu.VMEM((2, SUBLANES + rsub, sub), F32)],
        compiler_params=_params("arbitrary", "arbitrary"),
        name="ffn_up",
    )(x, w2, w2, conv_w2, conv_w2, conv_b2, conv_b2)


def _pad_to(x, axis, mult):
    pad = (-x.shape[axis]) % mult
    if pad == 0:
        return x
    widths = [(0, 0)] * x.ndim
    widths[axis] = (0, pad)
    return jnp.pad(x, widths)


def _lora(xs, lead, w_down, w_up, act, name):
    w_down = _pad_to(w_down, 1, LANES).astype(BF16)
    w_up = _pad_to(w_up, 0, LANES).astype(BF16)
    mid = matmul(xs, w_down, BF16, act=act, lead=lead, name=name + "_down")
    return matmul(mid, w_up, F32, name=name + "_up")


def rwkv_layer(x, seq, v_first, mix, w_rkv, w_o, w0, w1, w2, a0, a1, a2, g1, g2,
               k_k, k_a, r_k, gn_g, gn_b, v_lora):
    xs = token_shift_mix(x, mix[jnp.array([0, 2, 3, 1, 4, 5])], seq)
    rkv = batched_matmul(xs, w_rkv.astype(BF16), F32, nbatch=3, name="rwkv_rkv")
    r, k, v = rkv[0], rkv[1], rkv[2]
    wl = _lora(xs, 3, w1, w2, "tanh", "rwkv_w")
    al = _lora(xs, 4, a1, a2, None, "rwkv_a")
    g = _lora(xs, 5, g1, g2, "sigmoid", "rwkv_g")
    zeros = jnp.zeros_like(w0)
    if v_lora is None:
        v_first = v
        vl = vf = None
        v0 = zeros
    else:
        v0, v1, v2 = v_lora
        vl = _lora(xs, 2, v1, v2, None, "rwkv_v")
        vf = v_first
    chan = jnp.stack([w0, a0, k_k, k_a, r_k, gn_g, gn_b, v0])
    y = rwkv_scan(r, k, v, wl, al, g, vl, vf, chan, seq)
    return matmul(y, w_o.astype(BF16), F32, name="rwkv_out"), v_first


def attn_layer(xb, seq, w_qkv, w_o, sinks, bias):
    n_q = w_o.shape[0] // HEAD
    n_kv = (w_qkv.shape[1] // HEAD - n_q) // 2
    qkv = matmul(xb, w_qkv.astype(BF16), BF16, name="attn_qkv")
    o = swa_attention(qkv, bias, sinks, seq, n_q, n_kv)
    return matmul(o, w_o.astype(BF16), F32, name="attn_out")


FFN_COL_BLOCK = 512
FFN_ROW_BLOCK = 1024


def _pad_gate_up(a, f):
    return jnp.concatenate([_pad_to(a[..., :f], a.ndim - 1, FFN_COL_BLOCK),
                            _pad_to(a[..., f:], a.ndim - 1, FFN_COL_BLOCK)], axis=-1)


def ffn_layer(xb, seq, w_up, conv_w, conv_b, w_down):
    f = w_down.shape[0]
    act = ffn_up(xb, _pad_gate_up(w_up, f).astype(BF16), _pad_gate_up(conv_w, f),
                 _pad_gate_up(conv_b.reshape(1, -1), f), seq, bm=FFN_ROW_BLOCK, bn=FFN_COL_BLOCK)
    w_down = _pad_to(w_down, 0, FFN_COL_BLOCK).astype(BF16)
    return matmul(act, w_down, F32, bm=512, bn=512, name="ffn_down")


def kernel(x, rwkv_mix, rwkv_w_rkv, rwkv_w_o, rwkv_w0, rwkv_w1, rwkv_w2, rwkv_a0, rwkv_a1, rwkv_a2, rwkv_g1, rwkv_g2, rwkv_k_k, rwkv_k_a, rwkv_r_k, rwkv_gn_g, rwkv_gn_b, rwkv_v0, rwkv_v1, rwkv_v2, attn_w_qkv, attn_w_o, attn_sinks, rel_bias, ffn_w_up, ffn_conv_w, ffn_conv_b, ffn_w_down, ln1_g, ln1_b, ln2_g, ln2_b):
    B, S, D = x.shape
    depth = ln1_g.shape[0]
    xf = x.reshape(B * S, D)
    xb = xf.astype(BF16)
    bias = bias_table(rel_bias, _bucket_map())
    v_first = None
    for i in range(depth):
        j = i // 2
        if i % 2 == 0:
            v_lora = None if j == 0 else (rwkv_v0[j - 1], rwkv_v1[j - 1], rwkv_v2[j - 1])
            h, v_first = rwkv_layer(xf, S, v_first, rwkv_mix[j], rwkv_w_rkv[j],
                                    rwkv_w_o[j], rwkv_w0[j], rwkv_w1[j], rwkv_w2[j], rwkv_a0[j],
                                    rwkv_a1[j], rwkv_a2[j], rwkv_g1[j], rwkv_g2[j], rwkv_k_k[j],
                                    rwkv_k_a[j], rwkv_r_k[j], rwkv_gn_g[j], rwkv_gn_b[j], v_lora)
        else:
            h = attn_layer(xb, S, attn_w_qkv[j], attn_w_o[j], attn_sinks[j], bias)
        xf, xb = ln_residual(xf, h, ln1_g[i], ln1_b[i])
        f = ffn_layer(xb, S, ffn_w_up[i], ffn_conv_w[i], ffn_conv_b[i], ffn_w_down[i])
        xf, xb = ln_residual(xf, f, ln2_g[i], ln2_b[i])
    return xf.reshape(B, S, D)
```

```python
import functools
import math

import jax
import jax.numpy as jnp
from jax import lax
from jax.experimental import pallas as pl
from jax.experimental.pallas import tpu as pltpu

F32 = jnp.float32
BF16 = jnp.bfloat16

LANES = 128
SUBLANES = 8
MXU_DIM = 256
VMEM_LIMIT_BYTES = 56 * 1024 * 1024

HEAD = 64
HEADS_PER_GROUP = MXU_DIM // HEAD
CHUNK = 64
STACK = HEADS_PER_GROUP * CHUNK

WINDOW = 128
BLOCK = 128
REL_BUCKETS = 32
REL_MAX_DIST = 128
CONV_W = 3
LN_EPS = 1e-5
GN_EPS = HEAD * 1e-5
DEPTH = 4
DEEPNORM_ALPHA = (2 * DEPTH) ** 0.25


def _params(*sem):
    return pltpu.CompilerParams(dimension_semantics=sem, vmem_limit_bytes=VMEM_LIMIT_BYTES)


def _block(n, target, unit):
    if n <= target:
        return n
    b = target - target % unit
    while n % b:
        b -= unit
    return b


def _sigmoid(x):
    return 1.0 / (1.0 + jnp.exp(-x))


def _dot(a, b):
    return jnp.dot(a, b, preferred_element_type=F32)


def _dot_nt(a, b):
    return lax.dot_general(a, b, (((1,), (1,)), ((), ())), preferred_element_type=F32)


def _dot_tn(a, b):
    return lax.dot_general(a, b, (((0,), (0,)), ((), ())), preferred_element_type=F32)


def _mm_body(x_ref, w_ref, o_ref, *, act):
    acc = _dot(x_ref[...], w_ref[...])
    if act == "tanh":
        acc = jnp.tanh(acc)
    elif act == "sigmoid":
        acc = _sigmoid(acc)
    o_ref[...] = acc.astype(o_ref.dtype)


def matmul(x, w, out_dtype, *, act=None, lead=None, bm=1024, bn=512, name="matmul"):
    M, K = x.shape[-2:]
    N = w.shape[1]
    bm, bn = _block(M, bm, SUBLANES), _block(N, bn, LANES)
    assert M % bm == 0 and N % bn == 0
    if lead is None:
        x_spec = pl.BlockSpec((bm, K), lambda i, j: (i, 0))
    else:
        x_spec = pl.BlockSpec((None, bm, K), lambda i, j: (lead, i, 0))
    return pl.pallas_call(
        functools.partial(_mm_body, act=act),
        grid=(M // bm, N // bn),
        in_specs=[x_spec, pl.BlockSpec((K, bn), lambda i, j: (0, j))],
        out_specs=pl.BlockSpec((bm, bn), lambda i, j: (i, j)),
        out_shape=jax.ShapeDtypeStruct((M, N), out_dtype),
        compiler_params=_params("parallel", "parallel"),
        name=name,
    )(x, w)


def _bmm_body(x_ref, w_ref, o_ref):
    o_ref[...] = _dot(x_ref[...], w_ref[...]).astype(o_ref.dtype)


def batched_matmul(x, w, out_dtype, *, nbatch, bm=1024, bn=512, name="bmm"):
    M, K = x.shape[-2:]
    N = w.shape[2]
    bm, bn = _block(M, bm, SUBLANES), _block(N, bn, LANES)
    assert M % bm == 0 and N % bn == 0
    return pl.pallas_call(
        _bmm_body,
        grid=(nbatch, M // bm, N // bn),
        in_specs=[pl.BlockSpec((None, bm, K), lambda p, i, j: (p, i, 0)),
                  pl.BlockSpec((None, K, bn), lambda p, i, j: (p, 0, j))],
        out_specs=pl.BlockSpec((None, bm, bn), lambda p, i, j: (p, i, j)),
        out_shape=jax.ShapeDtypeStruct((nbatch, M, N), out_dtype),
        compiler_params=_params("parallel", "parallel", "parallel"),
        name=name,
    )(x, w)


def _ln_body(x_ref, h_ref, g_ref, b_ref, of_ref, ob_ref):
    y = DEEPNORM_ALPHA * x_ref[...] + h_ref[...].astype(F32)
    mu = jnp.mean(y, axis=-1, keepdims=True)
    d = y - mu
    var = jnp.mean(d * d, axis=-1, keepdims=True)
    out = d * lax.rsqrt(var + LN_EPS) * g_ref[...] + b_ref[...]
    of_ref[...] = out
    ob_ref[...] = out.astype(BF16)


def ln_residual(x, h, g, b, *, br=256):
    T, D = x.shape
    br = min(br, T)
    assert T % br == 0
    row = pl.BlockSpec((br, D), lambda i: (i, 0))
    vec = pl.BlockSpec((1, D), lambda i: (0, 0))
    return pl.pallas_call(
        _ln_body,
        grid=(T // br,),
        in_specs=[row, row, vec, vec],
        out_specs=[row, row],
        out_shape=[jax.ShapeDtypeStruct((T, D), F32), jax.ShapeDtypeStruct((T, D), BF16)],
        compiler_params=_params("parallel"),
        name="ln_residual",
    )(x, h, g.reshape(1, D), b.reshape(1, D))


def _mix_body(x_ref, xp_ref, mix_ref, o_ref, *, blocks_per_seq):
    i = pl.program_id(0)
    x = x_ref[...]
    prev_last = xp_ref[SUBLANES - 1:SUBLANES, :]
    prev_last = jnp.where(i % blocks_per_seq == 0, 0.0, prev_last)
    row = lax.broadcasted_iota(jnp.int32, x.shape, 0)
    shifted = jnp.where(row == 0, prev_last, pltpu.roll(x, 1, 0))
    xx = shifted - x
    for p in range(6):
        o_ref[p] = (x + xx * mix_ref[p:p + 1, :]).astype(BF16)


def token_shift_mix(x, mix, seq, *, bt=512, bd=1024):
    T, D = x.shape
    bt, bd = min(bt, seq), min(bd, D)
    assert seq % bt == 0 and D % bd == 0 and bt % SUBLANES == 0
    halo = bt // SUBLANES
    return pl.pallas_call(
        functools.partial(_mix_body, blocks_per_seq=seq // bt),
        grid=(T // bt, D // bd),
        in_specs=[pl.BlockSpec((bt, bd), lambda i, j: (i, j)),
                  pl.BlockSpec((SUBLANES, bd), lambda i, j: (jnp.maximum(i * halo - 1, 0), j)),
                  pl.BlockSpec((6, bd), lambda i, j: (0, j))],
        out_specs=pl.BlockSpec((6, bt, bd), lambda i, j: (0, i, j)),
        out_shape=jax.ShapeDtypeStruct((6, T, D), BF16),
        compiler_params=_params("parallel", "parallel"),
        name="rwkv_mix",
    )(x, x, mix)


def _scan_body(*refs, n_chunks, has_vlora):
    if has_vlora:
        (r_ref, k_ref, v_ref, mw_ref, ma_ref, mg_ref, w2_ref, a2_ref, g2_ref,
         mv_ref, v2_ref, vf_ref, p_ref, o_ref, s_ref) = refs
    else:
        r_ref, k_ref, v_ref, mw_ref, ma_ref, mg_ref, w2_ref, a2_ref, g2_ref, p_ref, o_ref, s_ref = refs

    @pl.when(pl.program_id(2) == 0)
    def _():
        s_ref[...] = jnp.zeros_like(s_ref)

    W = MXU_DIM
    head_of_row = lax.broadcasted_iota(jnp.int32, (STACK, W), 0) // CHUNK
    head_of_lane = lax.broadcasted_iota(jnp.int32, (STACK, W), 1) // HEAD
    head_mask = head_of_row == head_of_lane
    rr = lax.broadcasted_iota(jnp.int32, (STACK, STACK), 0)
    cc = lax.broadcasted_iota(jnp.int32, (STACK, STACK), 1)
    same_head = (rr // CHUNK) == (cc // CHUNK)
    strict_lower = same_head & ((rr % CHUNK) > (cc % CHUNK))
    incl_lower = same_head & ((rr % CHUNK) >= (cc % CHUNK))
    eye = (rr == cc).astype(F32)
    lane_r = lax.broadcasted_iota(jnp.int32, (W, W), 0) // HEAD
    lane_c = lax.broadcasted_iota(jnp.int32, (W, W), 1) // HEAD
    seg_ones = (lane_r == lane_c).astype(BF16)
    bt = n_chunks * CHUNK
    tr = lax.broadcasted_iota(jnp.int32, (bt, bt), 0)
    tc = lax.broadcasted_iota(jnp.int32, (bt, bt), 1)
    tri = ((tr // CHUNK == tc // CHUNK) & (tr >= tc)).astype(BF16)

    w0 = p_ref[0:1, :]
    a0 = p_ref[1:2, :]
    k_k = p_ref[2:3, :]
    k_a = p_ref[3:4, :]
    r_k = p_ref[4:5, :]
    gn_g = p_ref[5:6, :]
    gn_b = p_ref[6:7, :]
    v0 = p_ref[7:8, :]

    nc = n_chunks
    chunks = range(nc)

    r = r_ref[...]
    k = k_ref[...]
    v = v_ref[...]
    z = w0 + _dot(mw_ref[...], w2_ref[...])
    w_log = -(jnp.maximum(-z, 0.0) + jnp.log(1.0 + jnp.exp(-jnp.abs(z)))) - 0.5
    log_decay = -jnp.exp(w_log)
    a = _sigmoid(a0 + _dot(ma_ref[...], a2_ref[...]))
    if has_vlora:
        v = v + (vf_ref[...] - v) * _sigmoid(v0 + _dot(mv_ref[...], v2_ref[...]))
    kk = k * k_k
    norm = jnp.sqrt(_dot((kk * kk).astype(BF16), seg_ones))
    kk = kk / jnp.maximum(norm, 1e-12)
    k = k * (1.0 + (a - 1.0) * k_a)

    cum = _split_dot_left(tri, log_decay)
    cum_last = jnp.concatenate(
        [jnp.broadcast_to(cum[(c + 1) * CHUNK - 1:(c + 1) * CHUNK, :], (CHUNK, W)) for c in chunks], axis=0)
    e_pos = jnp.exp(cum)
    e_neg = jnp.exp(-cum)
    e_prev = jnp.exp(cum - log_decay)
    e_tail = jnp.exp(cum_last - cum)
    kka = kk * a
    a_e = -kk * e_prev
    r_e = r * e_pos
    b_e = kka * e_neg
    k_e = k * e_neg
    b_end = kka * e_tail
    k_end = k * e_tail

    def stack(x, c):
        xc = x[c * CHUNK:(c + 1) * CHUNK]
        return jnp.where(head_mask, jnp.concatenate([xc] * HEADS_PER_GROUP, axis=0), 0.0)

    r_s = [stack(r_e, c) for c in chunks]
    a_sb = [stack(a_e, c).astype(BF16) for c in chunks]
    ar = [jnp.concatenate([a_sb[c], r_s[c].astype(BF16)], axis=0) for c in chunks]
    bk = [jnp.concatenate([stack(b_e, c), stack(k_e, c)], axis=0).astype(BF16) for c in chunks]
    p = [_dot_nt(ar[c], bk[c]) for c in chunks]
    l_ab = [jnp.where(strict_lower, p[c][:STACK, :STACK], 0.0) for c in chunks]
    l_ak = [jnp.where(strict_lower, p[c][:STACK, STACK:], 0.0).astype(BF16) for c in chunks]
    l_rb = [jnp.where(incl_lower, p[c][STACK:, :STACK], 0.0).astype(BF16) for c in chunks]
    l_rk = [jnp.where(incl_lower, p[c][STACK:, STACK:], 0.0).astype(BF16) for c in chunks]

    inv = [eye + l_ab[c] for c in chunks]
    power = [l_ab[c].astype(BF16) for c in chunks]
    for _ in range(int(math.log2(CHUNK)) - 1):
        power = [_dot(power[c], power[c]).astype(BF16) for c in chunks]
        inv = [inv[c] + _dot(inv[c].astype(BF16), power[c]) for c in chunks]

    v_sb = [stack(v, c).astype(BF16) for c in chunks]
    x = [jnp.concatenate([a_sb[c], _dot(l_ak[c], v_sb[c]).astype(BF16)], axis=1) for c in chunks]
    wu = [_dot(inv[c].astype(BF16), x[c]).astype(BF16) for c in chunks]
    ry = [_dot(l_rb[c], wu[c]) for c in chunks]
    r_eff = [(r_s[c] + ry[c][:, :W]).astype(BF16) for c in chunks]
    y0 = [ry[c][:, W:] + _dot(l_rk[c], v_sb[c]) for c in chunks]
    gn = [_dot_tn(wu[c], stack(b_end, c).astype(BF16)) for c in chunks]
    g_mat = [gn[c][:W].astype(BF16) for c in chunks]
    n_mat = [gn[c][W:] + _dot_tn(v_sb[c], stack(k_end, c).astype(BF16)) for c in chunks]
    decay_end = [jnp.exp(cum[(c + 1) * CHUNK - 1:(c + 1) * CHUNK, :]) for c in chunks]

    state = s_ref[...]
    ys = []
    for c in chunks:
        sb = state.astype(BF16)
        y_s = _dot_nt(r_eff[c], sb) + y0[c]
        y = y_s[0:CHUNK]
        for h in range(1, HEADS_PER_GROUP):
            y = y + y_s[h * CHUNK:(h + 1) * CHUNK]
        ys.append(y)
        state = state * decay_end[c] + _dot(sb, g_mat[c]) + n_mat[c]
    s_ref[...] = state
    y = jnp.concatenate(ys, axis=0)

    mu = _dot(y.astype(BF16), seg_ones) * (1.0 / HEAD)
    d = y - mu
    var = _dot((d * d).astype(BF16), seg_ones) * (1.0 / HEAD)
    yn = d * lax.rsqrt(var + GN_EPS) * gn_g + gn_b
    bonus = _dot((r * k * r_k).astype(BF16), seg_ones) * v
    gate = _dot(mg_ref[...], g2_ref[...])
    o_ref[...] = ((yn + bonus) * gate).astype(o_ref.dtype)


def _split_dot_left(e, x):
    hi = x.astype(BF16)
    lo = (x - hi.astype(F32)).astype(BF16)
    return _dot(e, hi) + _dot(e, lo)


def rwkv_scan(rkv, mids, ups, v_mix, chan, seq, *, bt=256):
    _, T, D = rkv.shape
    W = MXU_DIM
    bt = min(bt, seq)
    assert D % W == 0 and seq % bt == 0 and bt % CHUNK == 0
    has_vlora = v_mix is not None
    nb = seq // bt

    def stacked(p):
        return pl.BlockSpec((None, bt, W), lambda b, gidx, t: (p, b * nb + t, gidx))

    def mid(a):
        return pl.BlockSpec((bt, a.shape[1]), lambda b, gidx, t: (b * nb + t, 0))

    def up(a):
        return pl.BlockSpec((a.shape[0], W), lambda b, gidx, t: (0, gidx))

    ins = [rkv, rkv, rkv, *mids, *ups]
    specs = [stacked(0), stacked(1), stacked(2)] + [mid(a) for a in mids] + [up(a) for a in ups]
    if has_vlora:
        mv, v2, first_rkv = v_mix
        ins += [mv, v2, first_rkv]
        specs += [mid(mv), up(v2), stacked(2)]
    ins.append(chan)
    specs.append(pl.BlockSpec((SUBLANES, W), lambda b, gidx, t: (0, gidx)))
    tok = pl.BlockSpec((bt, W), lambda b, gidx, t: (b * nb + t, gidx))
    return pl.pallas_call(
        functools.partial(_scan_body, n_chunks=bt // CHUNK, has_vlora=has_vlora),
        grid=(T // seq, D // W, nb),
        in_specs=specs,
        out_specs=tok,
        out_shape=jax.ShapeDtypeStruct((T, D), BF16),
        scratch_shapes=[pltpu.VMEM((W, W), F32)],
        compiler_params=_params("parallel", "parallel", "arbitrary"),
        name="rwkv_scan",
    )(*ins)


def _bias_body(bmap_ref, rb_ref, o_ref):
    h = pl.program_id(0)
    bmap = bmap_ref[...]
    acc = jnp.zeros(bmap.shape, F32)
    for b in range(REL_BUCKETS):
        acc = jnp.where(bmap == b, rb_ref[b, h], acc)
    o_ref[0] = acc


def bias_table(rel_bias, bmap):
    H = rel_bias.shape[1]
    return pl.pallas_call(
        _bias_body,
        grid=(H,),
        in_specs=[pl.BlockSpec((BLOCK, 2 * BLOCK), lambda h: (0, 0)),
                  pl.BlockSpec(memory_space=pltpu.SMEM)],
        out_specs=pl.BlockSpec((1, BLOCK, 2 * BLOCK), lambda h: (h, 0, 0)),
        out_shape=jax.ShapeDtypeStruct((H, BLOCK, 2 * BLOCK), F32),
        compiler_params=_params("parallel"),
        name="rel_bias_table",
    )(bmap, rel_bias)


def _attn_body(q_ref, kp_ref, kc_ref, vp_ref, vc_ref, bias_ref, sink_ref, o_ref, *, group):
    n = pl.program_id(1)
    kvh = pl.program_id(2)
    low_kv = (kvh % 2) == 0
    kband = jnp.concatenate([kp_ref[...], kc_ref[...]], axis=0)
    vband = jnp.concatenate([vp_ref[...], vc_ref[...]], axis=0)
    kswap = pltpu.roll(kband, HEAD, 1)
    vswap = pltpu.roll(vband, HEAD, 1)
    k_lo = jnp.where(low_kv, kband, kswap)
    k_hi = jnp.where(low_kv, kswap, kband)
    v_lo = jnp.where(low_kv, vband, vswap)
    v_hi = jnp.where(low_kv, vswap, vband)

    low = lax.broadcasted_iota(jnp.int32, (BLOCK, LANES), 1) < HEAD
    qi = lax.broadcasted_iota(jnp.int32, (BLOCK, 2 * BLOCK), 0)
    kj = lax.broadcasted_iota(jnp.int32, (BLOCK, 2 * BLOCK), 1)
    dist = qi + BLOCK - kj
    valid = (dist >= 0) & (dist < WINDOW) & ((n * BLOCK - BLOCK + kj) >= 0)
    scale = HEAD ** -0.5

    for pair in range(group // 2):
        qp = q_ref[:, pair * LANES:(pair + 1) * LANES]
        halves = []
        for half in range(2):
            g = 2 * pair + half
            in_half = low if half == 0 else jnp.logical_not(low)
            qm = jnp.where(in_half, qp, jnp.zeros_like(qp))
            s = _dot_nt(qm, k_lo if half == 0 else k_hi) * scale + bias_ref[g]
            s = jnp.where(valid, s, -jnp.inf)
            sink = sink_ref[kvh * group + g]
            m = jnp.maximum(jnp.max(s, axis=-1, keepdims=True), sink)
            p = jnp.exp(s - m)
            denom = jnp.sum(p, axis=-1, keepdims=True) + jnp.exp(sink - m)
            halves.append(_dot(p.astype(BF16), v_lo if half == 0 else v_hi) / denom)
        o_ref[:, pair * LANES:(pair + 1) * LANES] = jnp.where(low, halves[0], halves[1]).astype(o_ref.dtype)


def swa_attention(qkv, bias, sinks, seq, n_q_heads, n_kv_heads):
    T = qkv.shape[0]
    group = n_q_heads // n_kv_heads
    assert group % 2 == 0 and n_kv_heads % 2 == 0 and seq % BLOCK == 0
    nb = seq // BLOCK
    gw = group * HEAD
    k_col0 = n_q_heads * HEAD // LANES
    v_col0 = k_col0 + n_kv_heads * HEAD // LANES

    def cur(col0):
        return pl.BlockSpec((BLOCK, LANES), lambda b, n, h: (b * nb + n, col0 + h // 2))

    def prev(col0):
        return pl.BlockSpec((BLOCK, LANES), lambda b, n, h: (b * nb + jnp.maximum(n - 1, 0), col0 + h // 2))

    return pl.pallas_call(
        functools.partial(_attn_body, group=group),
        grid=(T // seq, nb, n_kv_heads),
        in_specs=[pl.BlockSpec((BLOCK, gw), lambda b, n, h: (b * nb + n, h)),
                  prev(k_col0), cur(k_col0), prev(v_col0), cur(v_col0),
                  pl.BlockSpec((group, BLOCK, 2 * BLOCK), lambda b, n, h: (h, 0, 0)),
                  pl.BlockSpec(memory_space=pltpu.SMEM)],
        out_specs=pl.BlockSpec((BLOCK, gw), lambda b, n, h: (b * nb + n, h)),
        out_shape=jax.ShapeDtypeStruct((T, n_q_heads * HEAD), BF16),
        compiler_params=_params("parallel", "parallel", "parallel"),
        name="swa_attention",
    )(qkv, qkv, qkv, qkv, qkv, bias, sinks)


def _bucket_map():
    qi = jnp.arange(BLOCK)[:, None]
    kj = jnp.arange(2 * BLOCK)[None, :]
    dist = jnp.maximum(qi + BLOCK - kj, 0)
    max_exact = REL_BUCKETS // 2
    d_f = jnp.maximum(dist, max_exact).astype(F32)
    large = max_exact + (jnp.log(d_f / max_exact) / math.log(REL_MAX_DIST / max_exact)
                         * (REL_BUCKETS - max_exact)).astype(jnp.int32)
    large = jnp.minimum(large, REL_BUCKETS - 1)
    return jnp.where(dist < max_exact, dist, large).astype(jnp.int32)


def _ffn_up_body(x_ref, wg_ref, wu_ref, cg_ref, cu_ref, bg_ref, bu_ref, o_ref, tail_g_ref, tail_u_ref,
                 raw_g_ref, raw_u_ref, *, blocks_per_seq, sub, rsub):
    first = pl.program_id(0) % blocks_per_seq == 0
    j = pl.program_id(1)
    bm = x_ref.shape[0]
    H = SUBLANES

    @pl.when((pl.program_id(0) == 0) & (j == 0))
    def _():
        tail_g_ref[...] = jnp.zeros_like(tail_g_ref)
        tail_u_ref[...] = jnp.zeros_like(tail_u_ref)

    def matmul_tile(slot, s, rb, halo_g, halo_u):
        cols = pl.ds(s * sub, sub)
        xr = x_ref[pl.ds(rb * rsub, rsub), :]
        raw_g_ref[slot, 0:H, :] = halo_g
        raw_u_ref[slot, 0:H, :] = halo_u
        raw_g_ref[slot, H:, :] = _dot(xr, wg_ref[:, cols])
        raw_u_ref[slot, H:, :] = _dot(xr, wu_ref[:, cols])
        return raw_g_ref[slot, rsub:, :], raw_u_ref[slot, rsub:, :]

    def conv(raw_ref, slot, c_ref, b_ref, cols):
        h = raw_ref[slot, H:, :]
        h1 = raw_ref[slot, H - 1:H - 1 + rsub, :]
        h2 = raw_ref[slot, H - 2:H - 2 + rsub, :]
        return c_ref[0:1, cols] * h2 + c_ref[1:2, cols] * h1 + c_ref[2:3, cols] * h + b_ref[:, cols]

    def finish_tile(slot, s, rb):
        cols = pl.ds(s * sub, sub)
        gate = conv(raw_g_ref, slot, cg_ref, bg_ref, cols)
        up = conv(raw_u_ref, slot, cu_ref, bu_ref, cols)
        o_ref[pl.ds(rb * rsub, rsub), cols] = (gate * _sigmoid(gate) * up).astype(o_ref.dtype)

    n_rows = bm // rsub
    prev = None
    t = 0
    for s in range(wg_ref.shape[1] // sub):
        cols = pl.ds(s * sub, sub)
        halo_g = jnp.where(first, 0.0, tail_g_ref[j, :, cols])
        halo_u = jnp.where(first, 0.0, tail_u_ref[j, :, cols])
        for rb in range(n_rows):
            halo_g, halo_u = matmul_tile(t % 2, s, rb, halo_g, halo_u)
            if prev is not None:
                finish_tile(*prev)
            prev = (t % 2, s, rb)
            t += 1
        tail_g_ref[j, :, cols] = halo_g
        tail_u_ref[j, :, cols] = halo_u
    finish_tile(*prev)


def ffn_up(x, wg, wu, cg, cu, bg, bu, seq, *, bm=1024, bn=512, sub=256, rsub=256):
    T, D = x.shape
    F = wg.shape[1]
    bm, bn = min(bm, seq), min(bn, F)
    sub, rsub = min(sub, bn), min(rsub, bm)
    assert seq % bm == 0 and bn % sub == 0 and bm % rsub == 0 and rsub % SUBLANES == 0
    nf = pl.cdiv(F, bn)
    col = lambda rows: pl.BlockSpec((rows, bn), lambda i, j: (0, j))
    return pl.pallas_call(
        functools.partial(_ffn_up_body, blocks_per_seq=seq // bm, sub=sub, rsub=rsub),
        grid=(T // bm, nf),
        in_specs=[pl.BlockSpec((bm, D), lambda i, j: (i, 0)),
                  col(D), col(D), col(CONV_W), col(CONV_W), col(1), col(1)],
        out_specs=pl.BlockSpec((bm, bn), lambda i, j: (i, j)),
        out_shape=jax.ShapeDtypeStruct((T, F), BF16),
        scratch_shapes=[pltpu.VMEM((nf, SUBLANES, bn), F32), pltpu.VMEM((nf, SUBLANES, bn), F32),
                        pltpu.VMEM((2, SUBLANES + rsub, sub), F32), pltpu.VMEM((2, SUBLANES + rsub, sub), F32)],
        compiler_params=_params("arbitrary", "arbitrary"),
        name="ffn_up",
    )(x, wg, wu, cg, cu, bg, bu)


def _pad_to(x, axis, mult):
    pad = (-x.shape[axis]) % mult
    if pad == 0:
        return x
    widths = [(0, 0)] * x.ndim
    widths[axis] = (0, pad)
    return jnp.pad(x, widths)


def _lora_down(xs, lead, w_down, w_up, act, name):
    w_down = _pad_to(w_down, 1, LANES).astype(BF16)
    w_up = _pad_to(w_up, 0, LANES).astype(BF16)
    return matmul(xs, w_down, BF16, act=act, lead=lead, name=name + "_down"), w_up


def rwkv_layer(x, seq, first_rkv, mix, w_rkv, w_o, w0, w1, w2, a0, a1, a2, g1, g2,
               k_k, k_a, r_k, gn_g, gn_b, v_lora):
    xs = token_shift_mix(x, mix[jnp.array([0, 2, 3, 1, 4, 5])], seq)
    rkv = batched_matmul(xs, w_rkv.astype(BF16), F32, nbatch=3, name="rwkv_rkv")
    mw, w2 = _lora_down(xs, 3, w1, w2, "tanh", "rwkv_w")
    ma, a2 = _lora_down(xs, 4, a1, a2, None, "rwkv_a")
    mg, g2 = _lora_down(xs, 5, g1, g2, "sigmoid", "rwkv_g")
    if v_lora is None:
        v_mix = None
        v0 = jnp.zeros_like(w0)
    else:
        v0, v1, v2 = v_lora
        mv, v2 = _lora_down(xs, 2, v1, v2, None, "rwkv_v")
        v_mix = (mv, v2, first_rkv)
    chan = jnp.stack([w0, a0, k_k, k_a, r_k, gn_g, gn_b, v0])
    y = rwkv_scan(rkv, (mw, ma, mg), (w2, a2, g2), v_mix, chan, seq)
    return matmul(y, w_o.astype(BF16), BF16, name="rwkv_out"), rkv


def attn_layer(xb, seq, w_qkv, w_o, sinks, bias):
    n_q = w_o.shape[0] // HEAD
    n_kv = (w_qkv.shape[1] // HEAD - n_q) // 2
    qkv = matmul(xb, w_qkv.astype(BF16), BF16, name="attn_qkv")
    o = swa_attention(qkv, bias, sinks, seq, n_q, n_kv)
    return matmul(o, w_o.astype(BF16), BF16, name="attn_out")


FFN_COL_BLOCK = 512
FFN_ROW_BLOCK = 1024


def ffn_layer(xb, seq, w_up, conv_w, conv_b, w_down):
    f = w_down.shape[0]
    w_up = w_up.astype(BF16)
    conv_b = conv_b.reshape(1, -1)
    act = ffn_up(xb, w_up[:, :f], w_up[:, f:], conv_w[:, :f], conv_w[:, f:], conv_b[:, :f], conv_b[:, f:],
                 seq, bm=FFN_ROW_BLOCK, bn=FFN_COL_BLOCK)
    return matmul(act, w_down.astype(BF16), BF16, bm=512, bn=512, name="ffn_down")


def kernel(x, rwkv_mix, rwkv_w_rkv, rwkv_w_o, rwkv_w0, rwkv_w1, rwkv_w2, rwkv_a0, rwkv_a1, rwkv_a2, rwkv_g1, rwkv_g2, rwkv_k_k, rwkv_k_a, rwkv_r_k, rwkv_gn_g, rwkv_gn_b, rwkv_v0, rwkv_v1, rwkv_v2, attn_w_qkv, attn_w_o, attn_sinks, rel_bias, ffn_w_up, ffn_conv_w, ffn_conv_b, ffn_w_down, ln1_g, ln1_b, ln2_g, ln2_b):
    B, S, D = x.shape
    depth = ln1_g.shape[0]
    xf = x.reshape(B * S, D)
    xb = xf.astype(BF16)
    bias = bias_table(rel_bias, _bucket_map())
    first_rkv = None
    for i in range(depth):
        j = i // 2
        if i % 2 == 0:
            v_lora = None if j == 0 else (rwkv_v0[j - 1], rwkv_v1[j - 1], rwkv_v2[j - 1])
            h, rkv = rwkv_layer(xf, S, first_rkv, rwkv_mix[j], rwkv_w_rkv[j],
                                rwkv_w_o[j], rwkv_w0[j], rwkv_w1[j], rwkv_w2[j], rwkv_a0[j],
                                rwkv_a1[j], rwkv_a2[j], rwkv_g1[j], rwkv_g2[j], rwkv_k_k[j],
                                rwkv_k_a[j], rwkv_r_k[j], rwkv_gn_g[j], rwkv_gn_b[j], v_lora)
            if v_lora is None:
                first_rkv = rkv
        else:
            h = attn_layer(xb, S, attn_w_qkv[j], attn_w_o[j], attn_sinks[j], bias)
        xf, xb = ln_residual(xf, h, ln1_g[i], ln1_b[i])
        f = ffn_layer(xb, S, ffn_w_up[i], ffn_conv_w[i], ffn_conv_b[i], ffn_w_down[i])
        xf, xb = ln_residual(xf, f, ln2_g[i], ln2_b[i])
    return xf.reshape(B, S, D)
```

```python
import functools
import math

import jax
import jax.numpy as jnp
from jax import lax
from jax.experimental import pallas as pl
from jax.experimental.pallas import tpu as pltpu

F32 = jnp.float32
BF16 = jnp.bfloat16

LANES = 128
SUBLANES = 8
MXU_DIM = 256
VMEM_LIMIT_BYTES = 56 * 1024 * 1024

HEAD = 64
HEADS_PER_GROUP = MXU_DIM // HEAD
CHUNK = 64
STACK = HEADS_PER_GROUP * CHUNK

WINDOW = 128
BLOCK = 128
REL_BUCKETS = 32
REL_MAX_DIST = 128
CONV_W = 3
LN_EPS = 1e-5
GN_EPS = HEAD * 1e-5
DEPTH = 4
DEEPNORM_ALPHA = (2 * DEPTH) ** 0.25


def _params(*sem):
    return pltpu.CompilerParams(dimension_semantics=sem, vmem_limit_bytes=VMEM_LIMIT_BYTES)


def _block(n, target, unit):
    if n <= target:
        return n
    b = target - target % unit
    while n % b:
        b -= unit
    return b


def _sigmoid(x):
    return 1.0 / (1.0 + jnp.exp(-x))


def _dot(a, b):
    return jnp.dot(a, b, preferred_element_type=F32)


def _dot_nt(a, b):
    return lax.dot_general(a, b, (((1,), (1,)), ((), ())), preferred_element_type=F32)


def _dot_tn(a, b):
    return lax.dot_general(a, b, (((0,), (0,)), ((), ())), preferred_element_type=F32)


def _mm_body(x_ref, w_ref, o_ref, *, act):
    acc = _dot(x_ref[...], w_ref[...])
    if act == "tanh":
        acc = jnp.tanh(acc)
    elif act == "sigmoid":
        acc = _sigmoid(acc)
    o_ref[...] = acc.astype(o_ref.dtype)


def matmul(x, w, out_dtype, *, act=None, lead=None, bm=1024, bn=512, name="matmul"):
    M, K = x.shape[-2:]
    N = w.shape[1]
    bm, bn = _block(M, bm, SUBLANES), _block(N, bn, LANES)
    assert M % bm == 0 and N % bn == 0
    if lead is None:
        x_spec = pl.BlockSpec((bm, K), lambda i, j: (i, 0))
    else:
        x_spec = pl.BlockSpec((None, bm, K), lambda i, j: (lead, i, 0))
    return pl.pallas_call(
        functools.partial(_mm_body, act=act),
        grid=(M // bm, N // bn),
        in_specs=[x_spec, pl.BlockSpec((K, bn), lambda i, j: (0, j))],
        out_specs=pl.BlockSpec((bm, bn), lambda i, j: (i, j)),
        out_shape=jax.ShapeDtypeStruct((M, N), out_dtype),
        compiler_params=_params("parallel", "parallel"),
        name=name,
    )(x, w)


def _bmm_body(x_ref, w_ref, o_ref):
    o_ref[...] = _dot(x_ref[...], w_ref[...]).astype(o_ref.dtype)


def batched_matmul(x, w, out_dtype, *, nbatch, bm=1024, bn=512, name="bmm"):
    M, K = x.shape[-2:]
    N = w.shape[2]
    bm, bn = _block(M, bm, SUBLANES), _block(N, bn, LANES)
    assert M % bm == 0 and N % bn == 0
    return pl.pallas_call(
        _bmm_body,
        grid=(nbatch, M // bm, N // bn),
        in_specs=[pl.BlockSpec((None, bm, K), lambda p, i, j: (p, i, 0)),
                  pl.BlockSpec((None, K, bn), lambda p, i, j: (p, 0, j))],
        out_specs=pl.BlockSpec((None, bm, bn), lambda p, i, j: (p, i, j)),
        out_shape=jax.ShapeDtypeStruct((nbatch, M, N), out_dtype),
        compiler_params=_params("parallel", "parallel", "parallel"),
        name=name,
    )(x, w)


def _ln_body(x_ref, h_ref, g_ref, b_ref, of_ref, ob_ref):
    y = DEEPNORM_ALPHA * x_ref[...] + h_ref[...].astype(F32)
    mu = jnp.mean(y, axis=-1, keepdims=True)
    d = y - mu
    var = jnp.mean(d * d, axis=-1, keepdims=True)
    out = d * lax.rsqrt(var + LN_EPS) * g_ref[...] + b_ref[...]
    of_ref[...] = out
    ob_ref[...] = out.astype(BF16)


def ln_residual(x, h, g, b, *, br=256):
    T, D = x.shape
    br = min(br, T)
    assert T % br == 0
    row = pl.BlockSpec((br, D), lambda i: (i, 0))
    vec = pl.BlockSpec((1, D), lambda i: (0, 0))
    return pl.pallas_call(
        _ln_body,
        grid=(T // br,),
        in_specs=[row, row, vec, vec],
        out_specs=[row, row],
        out_shape=[jax.ShapeDtypeStruct((T, D), F32), jax.ShapeDtypeStruct((T, D), BF16)],
        compiler_params=_params("parallel"),
        name="ln_residual",
    )(x, h, g.reshape(1, D), b.reshape(1, D))


def _mix_body(x_ref, xp_ref, mix_ref, o_ref, *, blocks_per_seq):
    i = pl.program_id(0)
    x = x_ref[...]
    prev_last = xp_ref[SUBLANES - 1:SUBLANES, :]
    prev_last = jnp.where(i % blocks_per_seq == 0, 0.0, prev_last)
    row = lax.broadcasted_iota(jnp.int32, x.shape, 0)
    shifted = jnp.where(row == 0, prev_last, pltpu.roll(x, 1, 0))
    xx = shifted - x
    for p in range(6):
        o_ref[p] = (x + xx * mix_ref[p:p + 1, :]).astype(BF16)


def token_shift_mix(x, mix, seq, *, bt=512, bd=1024):
    T, D = x.shape
    bt, bd = min(bt, seq), min(bd, D)
    assert seq % bt == 0 and D % bd == 0 and bt % SUBLANES == 0
    halo = bt // SUBLANES
    return pl.pallas_call(
        functools.partial(_mix_body, blocks_per_seq=seq // bt),
        grid=(T // bt, D // bd),
        in_specs=[pl.BlockSpec((bt, bd), lambda i, j: (i, j)),
                  pl.BlockSpec((SUBLANES, bd), lambda i, j: (jnp.maximum(i * halo - 1, 0), j)),
                  pl.BlockSpec((6, bd), lambda i, j: (0, j))],
        out_specs=pl.BlockSpec((6, bt, bd), lambda i, j: (0, i, j)),
        out_shape=jax.ShapeDtypeStruct((6, T, D), BF16),
        compiler_params=_params("parallel", "parallel"),
        name="rwkv_mix",
    )(x, x, mix)


def _scan_body(*refs, n_chunks, has_vlora):
    if has_vlora:
        (r_ref, k_ref, v_ref, mw_ref, ma_ref, mg_ref, w2_ref, a2_ref, g2_ref,
         mv_ref, v2_ref, vf_ref, p_ref, o_ref, s_ref) = refs
    else:
        r_ref, k_ref, v_ref, mw_ref, ma_ref, mg_ref, w2_ref, a2_ref, g2_ref, p_ref, o_ref, s_ref = refs

    @pl.when(pl.program_id(2) == 0)
    def _():
        s_ref[...] = jnp.zeros_like(s_ref)

    W = MXU_DIM
    head_of_row = lax.broadcasted_iota(jnp.int32, (STACK, W), 0) // CHUNK
    head_of_lane = lax.broadcasted_iota(jnp.int32, (STACK, W), 1) // HEAD
    head_mask = head_of_row == head_of_lane
    rr = lax.broadcasted_iota(jnp.int32, (STACK, STACK), 0)
    cc = lax.broadcasted_iota(jnp.int32, (STACK, STACK), 1)
    same_head = (rr // CHUNK) == (cc // CHUNK)
    strict_lower = same_head & ((rr % CHUNK) > (cc % CHUNK))
    eye = (rr == cc).astype(F32)
    half_mask = ((lax.broadcasted_iota(jnp.int32, (STACK, LANES), 0) // CHUNK) % 2
                 == lax.broadcasted_iota(jnp.int32, (STACK, LANES), 1) // HEAD)
    natural_lower = (lax.broadcasted_iota(jnp.int32, (CHUNK, 4 * CHUNK), 0)
                     >= lax.broadcasted_iota(jnp.int32, (CHUNK, 4 * CHUNK), 1) % CHUNK)
    lane_r = lax.broadcasted_iota(jnp.int32, (W, W), 0) // HEAD
    lane_c = lax.broadcasted_iota(jnp.int32, (W, W), 1) // HEAD
    seg_ones = (lane_r == lane_c).astype(BF16)
    bt = n_chunks * CHUNK
    tr = lax.broadcasted_iota(jnp.int32, (bt, bt), 0)
    tc = lax.broadcasted_iota(jnp.int32, (bt, bt), 1)
    tri = ((tr // CHUNK == tc // CHUNK) & (tr >= tc)).astype(BF16)

    w0 = p_ref[0:1, :]
    a0 = p_ref[1:2, :]
    k_k = p_ref[2:3, :]
    k_a = p_ref[3:4, :]
    r_k = p_ref[4:5, :]
    gn_g = p_ref[5:6, :]
    gn_b = p_ref[6:7, :]
    v0 = p_ref[7:8, :]

    nc = n_chunks
    chunks = range(nc)

    r = r_ref[...]
    k = k_ref[...]
    v = v_ref[...]
    z = w0 + _dot(mw_ref[...], w2_ref[...])
    w_log = -(jnp.maximum(-z, 0.0) + jnp.log(1.0 + jnp.exp(-jnp.abs(z)))) - 0.5
    log_decay = -jnp.exp(w_log)
    a = _sigmoid(a0 + _dot(ma_ref[...], a2_ref[...]))
    if has_vlora:
        v = v + (vf_ref[...] - v) * _sigmoid(v0 + _dot(mv_ref[...], v2_ref[...]))
    kk = k * k_k
    norm = jnp.sqrt(_dot((kk * kk).astype(BF16), seg_ones))
    kk = kk / jnp.maximum(norm, 1e-12)
    k = k * (1.0 + (a - 1.0) * k_a)

    cum = _split_dot_left(tri, log_decay)
    cum_last = jnp.concatenate(
        [jnp.broadcast_to(cum[(c + 1) * CHUNK - 1:(c + 1) * CHUNK, :], (CHUNK, W)) for c in chunks], axis=0)
    e_pos = jnp.exp(cum)
    e_neg = jnp.exp(-cum)
    e_prev = jnp.exp(cum - log_decay)
    e_tail = jnp.exp(cum_last - cum)
    kka = kk * a
    a_e = -kk * e_prev
    r_e = r * e_pos
    b_e = kka * e_neg
    k_e = k * e_neg
    b_end = kka * e_tail
    k_end = k * e_tail

    def rows_of(x, c):
        return x[c * CHUNK:(c + 1) * CHUNK]

    def stack(x, c):
        return jnp.where(head_mask, jnp.concatenate([rows_of(x, c)] * HEADS_PER_GROUP, axis=0), 0.0)

    def stack_pair(x, c):
        xc = rows_of(x, c)
        parts = [xc[:, (h // 2) * LANES:(h // 2 + 1) * LANES] for h in range(HEADS_PER_GROUP)]
        return jnp.where(half_mask, jnp.concatenate(parts, axis=0), 0.0).astype(BF16)

    n_pairs = HEADS_PER_GROUP // 2
    pair_rows = [slice(q * 2 * CHUNK, (q + 1) * 2 * CHUNK) for q in range(n_pairs)]
    pair_lanes = [slice(q * LANES, (q + 1) * LANES) for q in range(n_pairs)]

    a_p = [stack_pair(a_e, c) for c in chunks]
    b_p = [stack_pair(b_e, c) for c in chunks]
    k_p = [stack_pair(k_e, c) for c in chunks]
    v_p = [stack_pair(v, c) for c in chunks]
    p_a = [_dot_nt(a_p[c], jnp.concatenate([b_p[c], k_p[c]], axis=0)) for c in chunks]
    l_ab = [jnp.where(strict_lower, p_a[c][:, :STACK], 0.0) for c in chunks]
    l_ak = [jnp.where(strict_lower, p_a[c][:, STACK:], 0.0).astype(BF16) for c in chunks]

    inv = [eye + l_ab[c] for c in chunks]
    power = [l_ab[c].astype(BF16) for c in chunks]
    for _ in range(int(math.log2(CHUNK)) - 1):
        power = [_dot(power[c], power[c]).astype(BF16) for c in chunks]
        inv = [inv[c] + _dot(inv[c].astype(BF16), power[c]) for c in chunks]

    x = [jnp.concatenate([a_p[c], _dot(l_ak[c], v_p[c]).astype(BF16)], axis=1) for c in chunks]
    wu = [_dot(inv[c].astype(BF16), x[c]).astype(BF16) for c in chunks]
    gn = [_dot_tn(wu[c], stack(b_end, c).astype(BF16)) for c in chunks]
    n_mat = [gn[c][LANES:] + _dot_tn(v_p[c], stack(k_end, c).astype(BF16)) for c in chunks]
    g_mat = [jnp.where(head_mask, jnp.concatenate([gn[c][:LANES]] * n_pairs, axis=0), 0.0).astype(BF16)
             for c in chunks]
    decay_end = [jnp.exp(cum[(c + 1) * CHUNK - 1:(c + 1) * CHUNK, :]) for c in chunks]

    r_eff, y0 = [], []
    zero_pad = jnp.zeros((2 * CHUNK, LANES), BF16)
    for c in chunks:
        re_q, y0_q = [], []
        for q in range(n_pairs):
            r_q = rows_of(r_e, c)[:, pair_lanes[q]]
            bk_q = jnp.concatenate([b_p[c][pair_rows[q]], k_p[c][pair_rows[q]]], axis=0)
            l_r = jnp.where(natural_lower, _dot_nt(r_q.astype(BF16), bk_q), 0.0).astype(BF16)
            rhs = jnp.concatenate([wu[c][pair_rows[q]],
                                   jnp.concatenate([zero_pad, v_p[c][pair_rows[q]]], axis=1)], axis=0)
            ry = _dot(l_r, rhs)
            re_q.append(r_q + ry[:, :LANES])
            y0_q.append(ry[:, LANES:])
        r_eff.append(jnp.concatenate(re_q, axis=1).astype(BF16))
        y0.append(jnp.concatenate(y0_q, axis=1))

    state = s_ref[...]
    ys = []
    for c in chunks:
        sb = state.astype(BF16)
        s_big = jnp.where(head_mask, jnp.concatenate([sb] * n_pairs, axis=0), jnp.zeros((), BF16))
        ys.append(_dot_nt(r_eff[c], s_big) + y0[c])
        state = state * decay_end[c] + _dot(sb, g_mat[c]) + n_mat[c]
    s_ref[...] = state
    y = jnp.concatenate(ys, axis=0)

    mu = _dot(y.astype(BF16), seg_ones) * (1.0 / HEAD)
    d = y - mu
    var = _dot((d * d).astype(BF16), seg_ones) * (1.0 / HEAD)
    yn = d * lax.rsqrt(var + GN_EPS) * gn_g + gn_b
    bonus = _dot((r * k * r_k).astype(BF16), seg_ones) * v
    gate = _dot(mg_ref[...], g2_ref[...])
    o_ref[...] = ((yn + bonus) * gate).astype(o_ref.dtype)


def _split_dot_left(e, x):
    hi = x.astype(BF16)
    lo = (x - hi.astype(F32)).astype(BF16)
    return _dot(e, hi) + _dot(e, lo)


def rwkv_scan(rkv, mids, ups, v_mix, chan, seq, *, bt=256):
    _, T, D = rkv.shape
    W = MXU_DIM
    bt = min(bt, seq)
    assert D % W == 0 and seq % bt == 0 and bt % CHUNK == 0
    has_vlora = v_mix is not None
    nb = seq // bt

    def stacked(p):
        return pl.BlockSpec((None, bt, W), lambda b, gidx, t: (p, b * nb + t, gidx))

    def mid(a):
        return pl.BlockSpec((bt, a.shape[1]), lambda b, gidx, t: (b * nb + t, 0))

    def up(a):
        return pl.BlockSpec((a.shape[0], W), lambda b, gidx, t: (0, gidx))

    ins = [rkv, rkv, rkv, *mids, *ups]
    specs = [stacked(0), stacked(1), stacked(2)] + [mid(a) for a in mids] + [up(a) for a in ups]
    if has_vlora:
        mv, v2, first_rkv = v_mix
        ins += [mv, v2, first_rkv]
        specs += [mid(mv), up(v2), stacked(2)]
    ins.append(chan)
    specs.append(pl.BlockSpec((SUBLANES, W), lambda b, gidx, t: (0, gidx)))
    tok = pl.BlockSpec((bt, W), lambda b, gidx, t: (b * nb + t, gidx))
    return pl.pallas_call(
        functools.partial(_scan_body, n_chunks=bt // CHUNK, has_vlora=has_vlora),
        grid=(T // seq, D // W, nb),
        in_specs=specs,
        out_specs=tok,
        out_shape=jax.ShapeDtypeStruct((T, D), BF16),
        scratch_shapes=[pltpu.VMEM((2 * HEAD, W), F32)],
        compiler_params=_params("parallel", "parallel", "arbitrary"),
        name="rwkv_scan",
    )(*ins)


def _bias_body(bmap_ref, rb_ref, o_ref):
    h = pl.program_id(0)
    bmap = bmap_ref[...]
    acc = jnp.full(bmap.shape, -jnp.inf, F32)
    for b in range(REL_BUCKETS):
        acc = jnp.where(bmap == b, rb_ref[b, h], acc)
    o_ref[0, 0] = acc
    kj = lax.broadcasted_iota(jnp.int32, bmap.shape, 1)
    o_ref[1, 0] = jnp.where(kj >= BLOCK, acc, -jnp.inf)


def bias_table(rel_bias, bmap):
    H = rel_bias.shape[1]
    return pl.pallas_call(
        _bias_body,
        grid=(H,),
        in_specs=[pl.BlockSpec((BLOCK, 2 * BLOCK), lambda h: (0, 0)),
                  pl.BlockSpec(memory_space=pltpu.SMEM)],
        out_specs=pl.BlockSpec((2, 1, BLOCK, 2 * BLOCK), lambda h: (0, h, 0, 0)),
        out_shape=jax.ShapeDtypeStruct((2, H, BLOCK, 2 * BLOCK), F32),
        compiler_params=_params("parallel"),
        name="rel_bias_table",
    )(bmap, rel_bias)


def _attn_body(q_ref, kp_ref, kc_ref, vp_ref, vc_ref, bias_ref, sink_ref, o_ref, *, group, n_kv):
    low = lax.broadcasted_iota(jnp.int32, (BLOCK, LANES), 1) < HEAD
    scale = HEAD ** -0.5

    for kv_pair in range(n_kv // 2):
        lanes = pl.ds(kv_pair * LANES, LANES)
        kband = jnp.concatenate([kp_ref[:, lanes], kc_ref[:, lanes]], axis=0)
        vband = jnp.concatenate([vp_ref[:, lanes], vc_ref[:, lanes]], axis=0)
        kswap = pltpu.roll(kband, HEAD, 1)
        vswap = pltpu.roll(vband, HEAD, 1)
        for e in range(2):
            kvh = 2 * kv_pair + e
            k_at = (kband, kswap) if e == 0 else (kswap, kband)
            v_at = (vband, vswap) if e == 0 else (vswap, vband)
            for pair in range(group // 2):
                q_lanes = pl.ds((kvh * group // 2 + pair) * LANES, LANES)
                qp = q_ref[:, q_lanes] * scale
                halves = []
                for half in range(2):
                    h = kvh * group + 2 * pair + half
                    in_half = low if half == 0 else jnp.logical_not(low)
                    qm = jnp.where(in_half, qp, jnp.zeros_like(qp))
                    s = _dot_nt(qm, k_at[half]) + bias_ref[0, h]
                    sink = sink_ref[h]
                    m = jnp.maximum(jnp.max(s, axis=-1, keepdims=True), sink)
                    p = jnp.exp(s - m)
                    denom = jnp.sum(p, axis=-1, keepdims=True) + jnp.exp(sink - m)
                    halves.append(_dot(p.astype(BF16), v_at[half]) / denom)
                o_ref[:, q_lanes] = jnp.where(low, halves[0], halves[1]).astype(o_ref.dtype)


def swa_attention(qkv, bias, sinks, seq, n_q_heads, n_kv_heads):
    T = qkv.shape[0]
    group = n_q_heads // n_kv_heads
    assert group % 2 == 0 and n_kv_heads % 2 == 0 and seq % BLOCK == 0
    nb = seq // BLOCK
    qw, kw = n_q_heads * HEAD, n_kv_heads * HEAD
    assert qw % kw == 0 and kw % LANES == 0
    k_col, v_col = qw // kw, qw // kw + 1

    def cur(col):
        return pl.BlockSpec((BLOCK, kw), lambda b, n: (b * nb + n, col))

    def prev(col):
        return pl.BlockSpec((BLOCK, kw), lambda b, n: (b * nb + jnp.maximum(n - 1, 0), col))

    return pl.pallas_call(
        functools.partial(_attn_body, group=group, n_kv=n_kv_heads),
        grid=(T // seq, nb),
        in_specs=[pl.BlockSpec((BLOCK, qw), lambda b, n: (b * nb + n, 0)),
                  prev(k_col), cur(k_col), prev(v_col), cur(v_col),
                  pl.BlockSpec((1, n_q_heads, BLOCK, 2 * BLOCK), lambda b, n: (jnp.where(n == 0, 1, 0), 0, 0, 0)),
                  pl.BlockSpec(memory_space=pltpu.SMEM)],
        out_specs=pl.BlockSpec((BLOCK, qw), lambda b, n: (b * nb + n, 0)),
        out_shape=jax.ShapeDtypeStruct((T, qw), BF16),
        compiler_params=_params("parallel", "parallel"),
        name="swa_attention",
    )(qkv, qkv, qkv, qkv, qkv, bias, sinks)


def _bucket_map():
    qi = jnp.arange(BLOCK)[:, None]
    kj = jnp.arange(2 * BLOCK)[None, :]
    signed = qi + BLOCK - kj
    dist = jnp.maximum(signed, 0)
    max_exact = REL_BUCKETS // 2
    d_f = jnp.maximum(dist, max_exact).astype(F32)
    large = max_exact + (jnp.log(d_f / max_exact) / math.log(REL_MAX_DIST / max_exact)
                         * (REL_BUCKETS - max_exact)).astype(jnp.int32)
    large = jnp.minimum(large, REL_BUCKETS - 1)
    bucket = jnp.where(dist < max_exact, dist, large)
    return jnp.where((signed >= 0) & (signed < WINDOW), bucket, -1).astype(jnp.int32)


def _ffn_up_body(x_ref, *refs, blocks_per_seq, sub, rsub, n_sub):
    wg_refs, wu_refs = refs[:n_sub], refs[n_sub:2 * n_sub]
    cg_ref, cu_ref, bg_ref, bu_ref, o_ref, tail_g_ref, tail_u_ref, raw_g_ref, raw_u_ref = refs[2 * n_sub:]
    first = pl.program_id(0) % blocks_per_seq == 0
    j = pl.program_id(1)
    bm = x_ref.shape[0]
    H = SUBLANES

    @pl.when((pl.program_id(0) == 0) & (j == 0))
    def _():
        tail_g_ref[...] = jnp.zeros_like(tail_g_ref)
        tail_u_ref[...] = jnp.zeros_like(tail_u_ref)

    def matmul_tile(slot, s, rb, halo_g, halo_u):
        xr = x_ref[pl.ds(rb * rsub, rsub), :]
        raw_g_ref[slot, 0:H, :] = halo_g
        raw_u_ref[slot, 0:H, :] = halo_u
        raw_g_ref[slot, H:, :] = _dot(xr, wg_refs[s][...])
        raw_u_ref[slot, H:, :] = _dot(xr, wu_refs[s][...])
        return raw_g_ref[slot, rsub:, :], raw_u_ref[slot, rsub:, :]

    def conv(raw_ref, slot, c_ref, b_ref, cols):
        h = raw_ref[slot, H:, :]
        h1 = raw_ref[slot, H - 1:H - 1 + rsub, :]
        h2 = raw_ref[slot, H - 2:H - 2 + rsub, :]
        return c_ref[0:1, cols] * h2 + c_ref[1:2, cols] * h1 + c_ref[2:3, cols] * h + b_ref[:, cols]

    def finish_tile(slot, s, rb):
        cols = pl.ds(s * sub, sub)
        gate = conv(raw_g_ref, slot, cg_ref, bg_ref, cols)
        up = conv(raw_u_ref, slot, cu_ref, bu_ref, cols)
        o_ref[pl.ds(rb * rsub, rsub), cols] = (gate * _sigmoid(gate) * up).astype(o_ref.dtype)

    n_rows = bm // rsub
    prev = None
    t = 0
    for s in range(n_sub):
        cols = pl.ds(s * sub, sub)
        halo_g = jnp.where(first, 0.0, tail_g_ref[j, :, cols])
        halo_u = jnp.where(first, 0.0, tail_u_ref[j, :, cols])
        for rb in range(n_rows):
            halo_g, halo_u = matmul_tile(t % 2, s, rb, halo_g, halo_u)
            if prev is not None:
                finish_tile(*prev)
            prev = (t % 2, s, rb)
            t += 1
        tail_g_ref[j, :, cols] = halo_g
        tail_u_ref[j, :, cols] = halo_u
    finish_tile(*prev)


def ffn_up(x, w2, cg, cu, bg, bu, seq, *, bm=1024, bn=512, sub=256, rsub=256):
    T, D = x.shape
    F = w2.shape[1] // 2
    bm, bn = min(bm, seq), min(bn, F)
    sub, rsub = min(sub, bn), min(rsub, bm)
    assert seq % bm == 0 and bn % sub == 0 and F % sub == 0 and bm % rsub == 0 and rsub % SUBLANES == 0
    nf = pl.cdiv(F, bn)
    n_sub = bn // sub
    last = 2 * F // sub - 1

    def wspec(s, half):
        return pl.BlockSpec((D, sub), lambda i, j: (0, jnp.minimum(half * (F // sub) + j * n_sub + s, last)))

    col = lambda rows: pl.BlockSpec((rows, bn), lambda i, j: (0, j))
    return pl.pallas_call(
        functools.partial(_ffn_up_body, blocks_per_seq=seq // bm, sub=sub, rsub=rsub, n_sub=n_sub),
        grid=(T // bm, nf),
        in_specs=[pl.BlockSpec((bm, D), lambda i, j: (i, 0))]
                 + [wspec(s, 0) for s in range(n_sub)] + [wspec(s, 1) for s in range(n_sub)]
                 + [col(CONV_W), col(CONV_W), col(1), col(1)],
        out_specs=pl.BlockSpec((bm, bn), lambda i, j: (i, j)),
        out_shape=jax.ShapeDtypeStruct((T, F), BF16),
        scratch_shapes=[pltpu.VMEM((nf, SUBLANES, bn), F32), pltpu.VMEM((nf, SUBLANES, bn), F32),
                        pltpu.VMEM((2, SUBLANES + rsub, sub), F32), pltpu.VMEM((2, SUBLANES + rsub, sub), F32)],
        compiler_params=_params("arbitrary", "arbitrary"),
        name="ffn_up",
    )(x, *([w2] * (2 * n_sub)), cg, cu, bg, bu)


def _pad_to(x, axis, mult):
    pad = (-x.shape[axis]) % mult
    if pad == 0:
        return x
    widths = [(0, 0)] * x.ndim
    widths[axis] = (0, pad)
    return jnp.pad(x, widths)


def _lora_down(xs, lead, w_down, w_up, act, name):
    w_down = _pad_to(w_down, 1, LANES).astype(BF16)
    w_up = _pad_to(w_up, 0, LANES).astype(BF16)
    return matmul(xs, w_down, BF16, act=act, lead=lead, name=name + "_down"), w_up


def rwkv_layer(x, seq, first_rkv, mix, w_rkv, w_o, w0, w1, w2, a0, a1, a2, g1, g2,
               k_k, k_a, r_k, gn_g, gn_b, v_lora):
    xs = token_shift_mix(x, mix[jnp.array([0, 2, 3, 1, 4, 5])], seq)
    rkv = batched_matmul(xs, w_rkv.astype(BF16), F32, nbatch=3, name="rwkv_rkv")
    mw, w2 = _lora_down(xs, 3, w1, w2, "tanh", "rwkv_w")
    ma, a2 = _lora_down(xs, 4, a1, a2, None, "rwkv_a")
    mg, g2 = _lora_down(xs, 5, g1, g2, "sigmoid", "rwkv_g")
    if v_lora is None:
        v_mix = None
        v0 = jnp.zeros_like(w0)
    else:
        v0, v1, v2 = v_lora
        mv, v2 = _lora_down(xs, 2, v1, v2, None, "rwkv_v")
        v_mix = (mv, v2, first_rkv)
    chan = jnp.stack([w0, a0, k_k, k_a, r_k, gn_g, gn_b, v0])
    y = rwkv_scan(rkv, (mw, ma, mg), (w2, a2, g2), v_mix, chan, seq)
    return matmul(y, w_o.astype(BF16), BF16, name="rwkv_out"), rkv


def attn_layer(xb, seq, w_qkv, w_o, sinks, bias):
    n_q = w_o.shape[0] // HEAD
    n_kv = (w_qkv.shape[1] // HEAD - n_q) // 2
    qkv = matmul(xb, w_qkv.astype(BF16), BF16, name="attn_qkv")
    o = swa_attention(qkv, bias, sinks, seq, n_q, n_kv)
    return matmul(o, w_o.astype(BF16), BF16, name="attn_out")


FFN_COL_BLOCK = 512
FFN_ROW_BLOCK = 1024
FFN_COL_SUB = MXU_DIM


def ffn_layer(xb, seq, w_up, conv_w, conv_b, w_down):
    f = w_down.shape[0]
    w_up = w_up.astype(BF16)
    conv_b = conv_b.reshape(1, -1)
    act = ffn_up(xb, w_up, conv_w[:, :f], conv_w[:, f:], conv_b[:, :f], conv_b[:, f:],
                 seq, bm=FFN_ROW_BLOCK, bn=FFN_COL_BLOCK, sub=FFN_COL_SUB)
    return matmul(act, w_down.astype(BF16), BF16, bm=512, bn=512, name="ffn_down")


def kernel(x, rwkv_mix, rwkv_w_rkv, rwkv_w_o, rwkv_w0, rwkv_w1, rwkv_w2, rwkv_a0, rwkv_a1, rwkv_a2, rwkv_g1, rwkv_g2, rwkv_k_k, rwkv_k_a, rwkv_r_k, rwkv_gn_g, rwkv_gn_b, rwkv_v0, rwkv_v1, rwkv_v2, attn_w_qkv, attn_w_o, attn_sinks, rel_bias, ffn_w_up, ffn_conv_w, ffn_conv_b, ffn_w_down, ln1_g, ln1_b, ln2_g, ln2_b):
    B, S, D = x.shape
    depth = ln1_g.shape[0]
    xf = x.reshape(B * S, D)
    xb = xf.astype(BF16)
    bias = bias_table(rel_bias, _bucket_map())
    first_rkv = None
    for i in range(depth):
        j = i // 2
        if i % 2 == 0:
            v_lora = None if j == 0 else (rwkv_v0[j - 1], rwkv_v1[j - 1], rwkv_v2[j - 1])
            h, rkv = rwkv_layer(xf, S, first_rkv, rwkv_mix[j], rwkv_w_rkv[j],
                                rwkv_w_o[j], rwkv_w0[j], rwkv_w1[j], rwkv_w2[j], rwkv_a0[j],
                                rwkv_a1[j], rwkv_a2[j], rwkv_g1[j], rwkv_g2[j], rwkv_k_k[j],
                                rwkv_k_a[j], rwkv_r_k[j], rwkv_gn_g[j], rwkv_gn_b[j], v_lora)
            if v_lora is None:
                first_rkv = rkv
        else:
            h = attn_layer(xb, S, attn_w_qkv[j], attn_w_o[j], attn_sinks[j], bias)
        xf, xb = ln_residual(xf, h, ln1_g[i], ln1_b[i])
        f = ffn_layer(xb, S, ffn_w_up[i], ffn_conv_w[i], ffn_conv_b[i], ffn_w_down[i])
        xf, xb = ln_residual(xf, f, ln2_g[i], ln2_b[i])
    return xf.reshape(B, S, D)
```

```python
import functools
import math

import jax
import jax.numpy as jnp
from jax import lax
from jax.experimental import pallas as pl
from jax.experimental.pallas import tpu as pltpu

F32 = jnp.float32
BF16 = jnp.bfloat16

LANES = 128
SUBLANES = 8
MXU_DIM = 256
VMEM_LIMIT_BYTES = 56 * 1024 * 1024

HEAD = 64
HEADS_PER_GROUP = MXU_DIM // HEAD
CHUNK = 64
STACK = HEADS_PER_GROUP * CHUNK

WINDOW = 128
BLOCK = 128
REL_BUCKETS = 32
REL_MAX_DIST = 128
CONV_W = 3
LN_EPS = 1e-5
GN_EPS = HEAD * 1e-5
DEPTH = 4
DEEPNORM_ALPHA = (2 * DEPTH) ** 0.25


def _params(*sem):
    return pltpu.CompilerParams(dimension_semantics=sem, vmem_limit_bytes=VMEM_LIMIT_BYTES)


def _block(n, target, unit):
    if n <= target:
        return n
    b = target - target % unit
    while n % b:
        b -= unit
    return b


def _sigmoid(x):
    return 1.0 / (1.0 + jnp.exp(-x))


def _dot(a, b):
    return jnp.dot(a, b, preferred_element_type=F32)


def _dot_nt(a, b):
    return lax.dot_general(a, b, (((1,), (1,)), ((), ())), preferred_element_type=F32)


def _dot_tn(a, b):
    return lax.dot_general(a, b, (((0,), (0,)), ((), ())), preferred_element_type=F32)


def _mm_body(x_ref, w_ref, o_ref, *, act):
    acc = _dot(x_ref[...], w_ref[...])
    if act == "tanh":
        acc = jnp.tanh(acc)
    elif act == "sigmoid":
        acc = _sigmoid(acc)
    o_ref[...] = acc.astype(o_ref.dtype)


def matmul(x, w, out_dtype, *, act=None, lead=None, layer=None, bm=1024, bn=512, name="matmul"):
    M, K = x.shape[-2:]
    N = w.shape[-1]
    bm, bn = _block(M, bm, SUBLANES), _block(N, bn, LANES)
    assert M % bm == 0 and N % bn == 0
    if lead is None:
        x_spec = pl.BlockSpec((bm, K), lambda i, j: (i, 0))
    else:
        x_spec = pl.BlockSpec((None, bm, K), lambda i, j: (lead, i, 0))
    if layer is None:
        w_spec = pl.BlockSpec((K, bn), lambda i, j: (0, j))
    else:
        w_spec = pl.BlockSpec((None, K, bn), lambda i, j: (layer, 0, j))
    return pl.pallas_call(
        functools.partial(_mm_body, act=act),
        grid=(M // bm, N // bn),
        in_specs=[x_spec, w_spec],
        out_specs=pl.BlockSpec((bm, bn), lambda i, j: (i, j)),
        out_shape=jax.ShapeDtypeStruct((M, N), out_dtype),
        compiler_params=_params("parallel", "parallel"),
        name=name,
    )(x, w)


def _bmm_body(x_ref, w_ref, o_ref):
    o_ref[...] = _dot(x_ref[...], w_ref[...]).astype(o_ref.dtype)


def batched_matmul(x, w, out_dtype, *, nbatch, layer, bm=1024, bn=512, name="bmm"):
    M, K = x.shape[-2:]
    N = w.shape[-1]
    bm, bn = _block(M, bm, SUBLANES), _block(N, bn, LANES)
    assert M % bm == 0 and N % bn == 0
    return pl.pallas_call(
        _bmm_body,
        grid=(nbatch, M // bm, N // bn),
        in_specs=[pl.BlockSpec((None, bm, K), lambda p, i, j: (p, i, 0)),
                  pl.BlockSpec((None, None, K, bn), lambda p, i, j: (layer, p, 0, j))],
        out_specs=pl.BlockSpec((None, bm, bn), lambda p, i, j: (p, i, j)),
        out_shape=jax.ShapeDtypeStruct((nbatch, M, N), out_dtype),
        compiler_params=_params("parallel", "parallel", "parallel"),
        name=name,
    )(x, w)


def _ln_body(x_ref, h_ref, g_ref, b_ref, of_ref, ob_ref):
    y = DEEPNORM_ALPHA * x_ref[...] + h_ref[...].astype(F32)
    mu = jnp.mean(y, axis=-1, keepdims=True)
    d = y - mu
    var = jnp.mean(d * d, axis=-1, keepdims=True)
    out = d * lax.rsqrt(var + LN_EPS) * g_ref[...] + b_ref[...]
    of_ref[...] = out
    ob_ref[...] = out.astype(BF16)


def ln_residual(x, h, g, b, *, br=256):
    T, D = x.shape
    br = min(br, T)
    assert T % br == 0
    row = pl.BlockSpec((br, D), lambda i: (i, 0))
    vec = pl.BlockSpec((1, D), lambda i: (0, 0))
    return pl.pallas_call(
        _ln_body,
        grid=(T // br,),
        in_specs=[row, row, vec, vec],
        out_specs=[row, row],
        out_shape=[jax.ShapeDtypeStruct((T, D), F32), jax.ShapeDtypeStruct((T, D), BF16)],
        compiler_params=_params("parallel"),
        name="ln_residual",
    )(x, h, g.reshape(1, D), b.reshape(1, D))


def _mix_body(x_ref, xp_ref, mix_ref, o_ref, *, blocks_per_seq):
    i = pl.program_id(0)
    x = x_ref[...]
    prev_last = xp_ref[SUBLANES - 1:SUBLANES, :]
    prev_last = jnp.where(i % blocks_per_seq == 0, 0.0, prev_last)
    row = lax.broadcasted_iota(jnp.int32, x.shape, 0)
    shifted = jnp.where(row == 0, prev_last, pltpu.roll(x, 1, 0))
    xx = shifted - x
    for p in range(6):
        o_ref[p] = (x + xx * mix_ref[p:p + 1, :]).astype(BF16)


def token_shift_mix(x, mix, seq, *, bt=512, bd=1024):
    T, D = x.shape
    bt, bd = min(bt, seq), min(bd, D)
    assert seq % bt == 0 and D % bd == 0 and bt % SUBLANES == 0
    halo = bt // SUBLANES
    return pl.pallas_call(
        functools.partial(_mix_body, blocks_per_seq=seq // bt),
        grid=(T // bt, D // bd),
        in_specs=[pl.BlockSpec((bt, bd), lambda i, j: (i, j)),
                  pl.BlockSpec((SUBLANES, bd), lambda i, j: (jnp.maximum(i * halo - 1, 0), j)),
                  pl.BlockSpec((6, bd), lambda i, j: (0, j))],
        out_specs=pl.BlockSpec((6, bt, bd), lambda i, j: (0, i, j)),
        out_shape=jax.ShapeDtypeStruct((6, T, D), BF16),
        compiler_params=_params("parallel", "parallel"),
        name="rwkv_mix",
    )(x, x, mix)


PREP_STAGE, HEAD_STAGE, TAIL_STAGE = "prep", "head", "tail"
HEAD_STAGES_BEFORE_NEXT_GROUP = 8


def _scan_body(*refs, n_chunks, has_vlora, groups):
    shared = (3, 4, 5, 9) if has_vlora else (3, 4, 5)
    pending = []
    for g in range(groups):
        lanes = pl.ds(g * MXU_DIM, MXU_DIM)
        views = [ref if i in shared else ref.at[:, lanes] for i, ref in enumerate(refs[:-1])]
        pending.append(_scan_group(*views, refs[-1].at[g], n_chunks=n_chunks, has_vlora=has_vlora))
    active = []
    while pending or active:
        if pending and all(done >= HEAD_STAGES_BEFORE_NEXT_GROUP for _, done in active):
            active.append([pending.pop(0), 0])
        for entry in list(active):
            try:
                entry[1] += next(entry[0]) == HEAD_STAGE
            except StopIteration:
                active.remove(entry)


def _scan_group(*refs, n_chunks, has_vlora):
    if has_vlora:
        (r_ref, k_ref, v_ref, mw_ref, ma_ref, mg_ref, w2_ref, a2_ref, g2_ref,
         mv_ref, v2_ref, vf_ref, p_ref, o_ref, s_ref) = refs
    else:
        r_ref, k_ref, v_ref, mw_ref, ma_ref, mg_ref, w2_ref, a2_ref, g2_ref, p_ref, o_ref, s_ref = refs

    @pl.when(pl.program_id(2) == 0)
    def _():
        s_ref[...] = jnp.zeros(s_ref.shape, s_ref.dtype)

    W = MXU_DIM
    head_of_row = lax.broadcasted_iota(jnp.int32, (STACK, W), 0) // CHUNK
    head_of_lane = lax.broadcasted_iota(jnp.int32, (STACK, W), 1) // HEAD
    head_mask = head_of_row == head_of_lane
    rr = lax.broadcasted_iota(jnp.int32, (STACK, STACK), 0)
    cc = lax.broadcasted_iota(jnp.int32, (STACK, STACK), 1)
    same_head = (rr // CHUNK) == (cc // CHUNK)
    strict_lower = same_head & ((rr % CHUNK) > (cc % CHUNK))
    eye = (rr == cc).astype(F32)
    half_mask = ((lax.broadcasted_iota(jnp.int32, (STACK, LANES), 0) // CHUNK) % 2
                 == lax.broadcasted_iota(jnp.int32, (STACK, LANES), 1) // HEAD)
    natural_lower = (lax.broadcasted_iota(jnp.int32, (CHUNK, 4 * CHUNK), 0)
                     >= lax.broadcasted_iota(jnp.int32, (CHUNK, 4 * CHUNK), 1) % CHUNK)
    lane_r = lax.broadcasted_iota(jnp.int32, (W, W), 0) // HEAD
    lane_c = lax.broadcasted_iota(jnp.int32, (W, W), 1) // HEAD
    seg_ones = (lane_r == lane_c).astype(BF16)
    bt = n_chunks * CHUNK
    tr = lax.broadcasted_iota(jnp.int32, (bt, bt), 0)
    tc = lax.broadcasted_iota(jnp.int32, (bt, bt), 1)
    tri = ((tr // CHUNK == tc // CHUNK) & (tr >= tc)).astype(BF16)

    w0 = p_ref[0:1, :]
    a0 = p_ref[1:2, :]
    k_k = p_ref[2:3, :]
    k_a = p_ref[3:4, :]
    r_k = p_ref[4:5, :]
    gn_g = p_ref[5:6, :]
    gn_b = p_ref[6:7, :]
    v0 = p_ref[7:8, :]

    nc = n_chunks
    chunks = range(nc)

    r = r_ref[...]
    k = k_ref[...]
    v = v_ref[...]
    z = w0 + _dot(mw_ref[...], w2_ref[...])
    w_log = -(jnp.maximum(-z, 0.0) + jnp.log(1.0 + jnp.exp(-jnp.abs(z)))) - 0.5
    log_decay = -jnp.exp(w_log)
    yield PREP_STAGE
    a = _sigmoid(a0 + _dot(ma_ref[...], a2_ref[...]))
    if has_vlora:
        v = v + (vf_ref[...] - v) * _sigmoid(v0 + _dot(mv_ref[...], v2_ref[...]))
    yield PREP_STAGE
    kk = k * k_k
    norm = jnp.sqrt(_dot((kk * kk).astype(BF16), seg_ones))
    kk = kk / jnp.maximum(norm, 1e-12)
    k = k * (1.0 + (a - 1.0) * k_a)
    yield PREP_STAGE

    cum = _split_dot_left(tri, log_decay)
    cum_last = jnp.concatenate(
        [jnp.broadcast_to(cum[(c + 1) * CHUNK - 1:(c + 1) * CHUNK, :], (CHUNK, W)) for c in chunks], axis=0)
    e_pos = jnp.exp(cum)
    e_neg = jnp.exp(-cum)
    yield PREP_STAGE
    e_prev = jnp.exp(cum - log_decay)
    e_tail = jnp.exp(cum_last - cum)
    kka = kk * a
    yield PREP_STAGE
    a_e = -kk * e_prev
    r_e = r * e_pos
    b_e = kka * e_neg
    yield PREP_STAGE
    k_e = k * e_neg
    b_end = kka * e_tail
    k_end = k * e_tail

    def rows_of(x, c):
        return x[c * CHUNK:(c + 1) * CHUNK]

    def stack(x, c):
        return jnp.where(head_mask, jnp.concatenate([rows_of(x, c)] * HEADS_PER_GROUP, axis=0), 0.0)

    def stack_pair(x, c):
        xc = rows_of(x, c)
        parts = [xc[:, (h // 2) * LANES:(h // 2 + 1) * LANES] for h in range(HEADS_PER_GROUP)]
        return jnp.where(half_mask, jnp.concatenate(parts, axis=0), 0.0).astype(BF16)

    n_pairs = HEADS_PER_GROUP // 2
    pair_rows = [slice(q * 2 * CHUNK, (q + 1) * 2 * CHUNK) for q in range(n_pairs)]
    pair_lanes = [slice(q * LANES, (q + 1) * LANES) for q in range(n_pairs)]

    yield HEAD_STAGE

    a_p = [stack_pair(a_e, c) for c in chunks]
    b_p = [stack_pair(b_e, c) for c in chunks]
    k_p = [stack_pair(k_e, c) for c in chunks]
    v_p = [stack_pair(v, c) for c in chunks]
    p_a = [_dot_nt(a_p[c], jnp.concatenate([b_p[c], k_p[c]], axis=0)) for c in chunks]
    l_ab = [jnp.where(strict_lower, p_a[c][:, :STACK], 0.0) for c in chunks]
    l_ak = [jnp.where(strict_lower, p_a[c][:, STACK:], 0.0).astype(BF16) for c in chunks]
    yield HEAD_STAGE

    l_r = [[jnp.where(natural_lower,
                      _dot_nt(rows_of(r_e, c)[:, pair_lanes[q]].astype(BF16),
                              jnp.concatenate([b_p[c][pair_rows[q]], k_p[c][pair_rows[q]]], axis=0)),
                      0.0).astype(BF16) for q in range(n_pairs)] for c in chunks]
    yield HEAD_STAGE

    inv = [eye + l_ab[c] for c in chunks]
    power = [l_ab[c].astype(BF16) for c in chunks]
    for _ in range(int(math.log2(CHUNK)) - 1):
        power = [_dot(power[c], power[c]).astype(BF16) for c in chunks]
        yield HEAD_STAGE
        inv = [inv[c] + _dot(inv[c].astype(BF16), power[c]) for c in chunks]
        yield HEAD_STAGE

    x = [jnp.concatenate([a_p[c], _dot(l_ak[c], v_p[c]).astype(BF16)], axis=1) for c in chunks]
    yield HEAD_STAGE
    wu = [_dot(inv[c].astype(BF16), x[c]).astype(BF16) for c in chunks]
    yield HEAD_STAGE

    g_mat, n_mat, r_eff, y0 = [], [], [], []
    zero_pad = jnp.zeros((2 * CHUNK, LANES), BF16)
    for c in chunks:
        gn = _dot_tn(wu[c], stack(b_end, c).astype(BF16))
        n_mat.append(gn[LANES:] + _dot_tn(v_p[c], stack(k_end, c).astype(BF16)))
        g_mat.append(jnp.where(head_mask, jnp.concatenate([gn[:LANES]] * n_pairs, axis=0), 0.0).astype(BF16))
        yield TAIL_STAGE
        re_q, y0_q = [], []
        for q in range(n_pairs):
            rhs = jnp.concatenate([wu[c][pair_rows[q]],
                                   jnp.concatenate([zero_pad, v_p[c][pair_rows[q]]], axis=1)], axis=0)
            ry = _dot(l_r[c][q], rhs)
            re_q.append(rows_of(r_e, c)[:, pair_lanes[q]] + ry[:, :LANES])
            y0_q.append(ry[:, LANES:])
        r_eff.append(jnp.concatenate(re_q, axis=1).astype(BF16))
        y0.append(jnp.concatenate(y0_q, axis=1))
        yield TAIL_STAGE
    decay_end = [jnp.exp(cum[(c + 1) * CHUNK - 1:(c + 1) * CHUNK, :]) for c in chunks]

    state = s_ref[...]
    ys = []
    for c in chunks:
        sb = state.astype(BF16)
        s_big = jnp.where(head_mask, jnp.concatenate([sb] * n_pairs, axis=0), jnp.zeros((), BF16))
        ys.append(_dot_nt(r_eff[c], s_big) + y0[c])
        state = state * decay_end[c] + _dot(sb, g_mat[c]) + n_mat[c]
        yield TAIL_STAGE
    s_ref[...] = state
    y = jnp.concatenate(ys, axis=0)

    mu = _dot(y.astype(BF16), seg_ones) * (1.0 / HEAD)
    d = y - mu
    var = _dot((d * d).astype(BF16), seg_ones) * (1.0 / HEAD)
    yield TAIL_STAGE
    yn = d * lax.rsqrt(var + GN_EPS) * gn_g + gn_b
    bonus = _dot((r * k * r_k).astype(BF16), seg_ones) * v
    gate = _dot(mg_ref[...], g2_ref[...])
    o_ref[...] = ((yn + bonus) * gate).astype(o_ref.dtype)
    yield TAIL_STAGE


def _split_dot_left(e, x):
    hi = x.astype(BF16)
    lo = (x - hi.astype(F32)).astype(BF16)
    return _dot(e, hi) + _dot(e, lo)


def rwkv_scan(rkv, mids, ups, v_mix, chan, seq, *, bt=256, groups=4):
    _, T, D = rkv.shape
    groups = min(groups, D // MXU_DIM)
    W = groups * MXU_DIM
    bt = min(bt, seq)
    assert D % W == 0 and seq % bt == 0 and bt % CHUNK == 0
    has_vlora = v_mix is not None
    nb = seq // bt

    def stacked(p):
        return pl.BlockSpec((None, bt, W), lambda b, gidx, t: (p, b * nb + t, gidx))

    def mid(a):
        return pl.BlockSpec((bt, a.shape[1]), lambda b, gidx, t: (b * nb + t, 0))

    def up(a):
        return pl.BlockSpec((a.shape[0], W), lambda b, gidx, t: (0, gidx))

    ins = [rkv, rkv, rkv, *mids, *ups]
    specs = [stacked(0), stacked(1), stacked(2)] + [mid(a) for a in mids] + [up(a) for a in ups]
    if has_vlora:
        mv, v2, first_rkv = v_mix
        ins += [mv, v2, first_rkv]
        specs += [mid(mv), up(v2), stacked(2)]
    ins.append(chan)
    specs.append(pl.BlockSpec((SUBLANES, W), lambda b, gidx, t: (0, gidx)))
    tok = pl.BlockSpec((bt, W), lambda b, gidx, t: (b * nb + t, gidx))
    return pl.pallas_call(
        functools.partial(_scan_body, n_chunks=bt // CHUNK, has_vlora=has_vlora, groups=groups),
        grid=(T // seq, D // W, nb),
        in_specs=specs,
        out_specs=tok,
        out_shape=jax.ShapeDtypeStruct((T, D), BF16),
        scratch_shapes=[pltpu.VMEM((groups, 2 * HEAD, MXU_DIM), F32)],
        compiler_params=_params("parallel", "parallel", "arbitrary"),
        name="rwkv_scan",
    )(*ins)


def _bias_body(bmap_ref, rb_ref, o_ref):
    h = pl.program_id(0)
    bmap = bmap_ref[...]
    acc = jnp.full(bmap.shape, -jnp.inf, F32)
    for b in range(REL_BUCKETS):
        acc = jnp.where(bmap == b, rb_ref[b, h], acc)
    o_ref[0, 0] = acc
    kj = lax.broadcasted_iota(jnp.int32, bmap.shape, 1)
    o_ref[1, 0] = jnp.where(kj >= BLOCK, acc, -jnp.inf)


def bias_table(rel_bias, bmap):
    H = rel_bias.shape[1]
    return pl.pallas_call(
        _bias_body,
        grid=(H,),
        in_specs=[pl.BlockSpec((BLOCK, 2 * BLOCK), lambda h: (0, 0)),
                  pl.BlockSpec(memory_space=pltpu.SMEM)],
        out_specs=pl.BlockSpec((2, 1, BLOCK, 2 * BLOCK), lambda h: (0, h, 0, 0)),
        out_shape=jax.ShapeDtypeStruct((2, H, BLOCK, 2 * BLOCK), F32),
        compiler_params=_params("parallel"),
        name="rel_bias_table",
    )(bmap, rel_bias)


def _attn_body(q_ref, kp_ref, kc_ref, vp_ref, vc_ref, bias_ref, sink_ref, o_ref, *, group, n_kv):
    low = lax.broadcasted_iota(jnp.int32, (BLOCK, LANES), 1) < HEAD
    scale = HEAD ** -0.5

    for kv_pair in range(n_kv // 2):
        lanes = pl.ds(kv_pair * LANES, LANES)
        kband = jnp.concatenate([kp_ref[:, lanes], kc_ref[:, lanes]], axis=0)
        vband = jnp.concatenate([vp_ref[:, lanes], vc_ref[:, lanes]], axis=0)
        kswap = pltpu.roll(kband, HEAD, 1)
        vswap = pltpu.roll(vband, HEAD, 1)
        for e in range(2):
            kvh = 2 * kv_pair + e
            k_at = (kband, kswap) if e == 0 else (kswap, kband)
            v_at = (vband, vswap) if e == 0 else (vswap, vband)
            for pair in range(group // 2):
                q_lanes = pl.ds((kvh * group // 2 + pair) * LANES, LANES)
                qp = q_ref[:, q_lanes] * scale
                halves = []
                for half in range(2):
                    h = kvh * group + 2 * pair + half
                    in_half = low if half == 0 else jnp.logical_not(low)
                    qm = jnp.where(in_half, qp, jnp.zeros_like(qp))
                    s = _dot_nt(qm, k_at[half]) + bias_ref[0, h]
                    sink = sink_ref[h]
                    m = jnp.maximum(jnp.max(s, axis=-1, keepdims=True), sink)
                    p = jnp.exp(s - m)
                    denom = jnp.sum(p, axis=-1, keepdims=True) + jnp.exp(sink - m)
                    halves.append(_dot(p.astype(BF16), v_at[half]) / denom)
                o_ref[:, q_lanes] = jnp.where(low, halves[0], halves[1]).astype(o_ref.dtype)


def swa_attention(qkv, bias, sinks, seq, n_q_heads, n_kv_heads):
    T = qkv.shape[0]
    group = n_q_heads // n_kv_heads
    assert group % 2 == 0 and n_kv_heads % 2 == 0 and seq % BLOCK == 0
    nb = seq // BLOCK
    qw, kw = n_q_heads * HEAD, n_kv_heads * HEAD
    assert qw % kw == 0 and kw % LANES == 0
    k_col, v_col = qw // kw, qw // kw + 1

    def cur(col):
        return pl.BlockSpec((BLOCK, kw), lambda b, n: (b * nb + n, col))

    def prev(col):
        return pl.BlockSpec((BLOCK, kw), lambda b, n: (b * nb + jnp.maximum(n - 1, 0), col))

    return pl.pallas_call(
        functools.partial(_attn_body, group=group, n_kv=n_kv_heads),
        grid=(T // seq, nb),
        in_specs=[pl.BlockSpec((BLOCK, qw), lambda b, n: (b * nb + n, 0)),
                  prev(k_col), cur(k_col), prev(v_col), cur(v_col),
                  pl.BlockSpec((1, n_q_heads, BLOCK, 2 * BLOCK), lambda b, n: (jnp.where(n == 0, 1, 0), 0, 0, 0)),
                  pl.BlockSpec(memory_space=pltpu.SMEM)],
        out_specs=pl.BlockSpec((BLOCK, qw), lambda b, n: (b * nb + n, 0)),
        out_shape=jax.ShapeDtypeStruct((T, qw), BF16),
        compiler_params=_params("parallel", "parallel"),
        name="swa_attention",
    )(qkv, qkv, qkv, qkv, qkv, bias, sinks)


def _bucket_map():
    qi = jnp.arange(BLOCK)[:, None]
    kj = jnp.arange(2 * BLOCK)[None, :]
    signed = qi + BLOCK - kj
    dist = jnp.maximum(signed, 0)
    max_exact = REL_BUCKETS // 2
    d_f = jnp.maximum(dist, max_exact).astype(F32)
    large = max_exact + (jnp.log(d_f / max_exact) / math.log(REL_MAX_DIST / max_exact)
                         * (REL_BUCKETS - max_exact)).astype(jnp.int32)
    large = jnp.minimum(large, REL_BUCKETS - 1)
    bucket = jnp.where(dist < max_exact, dist, large)
    return jnp.where((signed >= 0) & (signed < WINDOW), bucket, -1).astype(jnp.int32)


def _ffn_up_body(x_ref, *refs, blocks_per_seq, sub, rsub, n_sub):
    wg_refs, wu_refs = refs[:n_sub], refs[n_sub:2 * n_sub]
    cg_ref, cu_ref, bg_ref, bu_ref, o_ref, tail_g_ref, tail_u_ref, raw_g_ref, raw_u_ref = refs[2 * n_sub:]
    first = pl.program_id(0) % blocks_per_seq == 0
    j = pl.program_id(1)
    bm = x_ref.shape[0]
    H = SUBLANES

    @pl.when((pl.program_id(0) == 0) & (j == 0))
    def _():
        tail_g_ref[...] = jnp.zeros_like(tail_g_ref)
        tail_u_ref[...] = jnp.zeros_like(tail_u_ref)

    def matmul_tile(slot, s, rb, halo_g, halo_u):
        xr = x_ref[pl.ds(rb * rsub, rsub), :]
        raw_g_ref[slot, 0:H, :] = halo_g
        raw_u_ref[slot, 0:H, :] = halo_u
        raw_g_ref[slot, H:, :] = _dot(xr, wg_refs[s][...])
        raw_u_ref[slot, H:, :] = _dot(xr, wu_refs[s][...])
        return raw_g_ref[slot, rsub:, :], raw_u_ref[slot, rsub:, :]

    def conv(raw_ref, slot, c_ref, b_ref, cols):
        h = raw_ref[slot, H:, :]
        h1 = raw_ref[slot, H - 1:H - 1 + rsub, :]
        h2 = raw_ref[slot, H - 2:H - 2 + rsub, :]
        return c_ref[0:1, cols] * h2 + c_ref[1:2, cols] * h1 + c_ref[2:3, cols] * h + b_ref[:, cols]

    def finish_tile(slot, s, rb):
        cols = pl.ds(s * sub, sub)
        gate = conv(raw_g_ref, slot, cg_ref, bg_ref, cols)
        up = conv(raw_u_ref, slot, cu_ref, bu_ref, cols)
        o_ref[pl.ds(rb * rsub, rsub), cols] = (gate * _sigmoid(gate) * up).astype(o_ref.dtype)

    n_rows = bm // rsub
    prev = None
    t = 0
    for s in range(n_sub):
        cols = pl.ds(s * sub, sub)
        halo_g = jnp.where(first, 0.0, tail_g_ref[j, :, cols])
        halo_u = jnp.where(first, 0.0, tail_u_ref[j, :, cols])
        for rb in range(n_rows):
            halo_g, halo_u = matmul_tile(t % 2, s, rb, halo_g, halo_u)
            if prev is not None:
                finish_tile(*prev)
            prev = (t % 2, s, rb)
            t += 1
        tail_g_ref[j, :, cols] = halo_g
        tail_u_ref[j, :, cols] = halo_u
    finish_tile(*prev)


def ffn_up(x, w2, layer, cg, cu, bg, bu, seq, *, bm=1024, bn=512, sub=256, rsub=256):
    T, D = x.shape
    F = w2.shape[2] // 2
    bm, bn = min(bm, seq), min(bn, F)
    sub, rsub = min(sub, bn), min(rsub, bm)
    assert seq % bm == 0 and bn % sub == 0 and F % sub == 0 and bm % rsub == 0 and rsub % SUBLANES == 0
    nf = pl.cdiv(F, bn)
    n_sub = bn // sub
    last = 2 * F // sub - 1

    def wspec(s, half):
        return pl.BlockSpec((None, D, sub),
                            lambda i, j: (layer, 0, jnp.minimum(half * (F // sub) + j * n_sub + s, last)))

    col = lambda rows: pl.BlockSpec((rows, bn), lambda i, j: (0, j))
    return pl.pallas_call(
        functools.partial(_ffn_up_body, blocks_per_seq=seq // bm, sub=sub, rsub=rsub, n_sub=n_sub),
        grid=(T // bm, nf),
        in_specs=[pl.BlockSpec((bm, D), lambda i, j: (i, 0))]
                 + [wspec(s, 0) for s in range(n_sub)] + [wspec(s, 1) for s in range(n_sub)]
                 + [col(CONV_W), col(CONV_W), col(1), col(1)],
        out_specs=pl.BlockSpec((bm, bn), lambda i, j: (i, j)),
        out_shape=jax.ShapeDtypeStruct((T, F), BF16),
        scratch_shapes=[pltpu.VMEM((nf, SUBLANES, bn), F32), pltpu.VMEM((nf, SUBLANES, bn), F32),
                        pltpu.VMEM((2, SUBLANES + rsub, sub), F32), pltpu.VMEM((2, SUBLANES + rsub, sub), F32)],
        compiler_params=_params("arbitrary", "arbitrary"),
        name="ffn_up",
    )(x, *([w2] * (2 * n_sub)), cg, cu, bg, bu)


def _pad_to(x, axis, mult):
    pad = (-x.shape[axis]) % mult
    if pad == 0:
        return x
    widths = [(0, 0)] * x.ndim
    widths[axis] = (0, pad)
    return jnp.pad(x, widths)


def _lora_down(xs, lead, w_down, w_up, act, name):
    w_down = _pad_to(w_down, 1, LANES).astype(BF16)
    w_up = _pad_to(w_up, 0, LANES).astype(BF16)
    return matmul(xs, w_down, BF16, act=act, lead=lead, name=name + "_down"), w_up


def rwkv_layer(x, seq, first_rkv, layer, mix, w_rkv, w_o, w0, w1, w2, a0, a1, a2, g1, g2,
               k_k, k_a, r_k, gn_g, gn_b, v_lora):
    xs = token_shift_mix(x, mix[jnp.array([0, 2, 3, 1, 4, 5])], seq)
    rkv = batched_matmul(xs, w_rkv, F32, nbatch=3, layer=layer, name="rwkv_rkv")
    mw, w2 = _lora_down(xs, 3, w1, w2, "tanh", "rwkv_w")
    ma, a2 = _lora_down(xs, 4, a1, a2, None, "rwkv_a")
    mg, g2 = _lora_down(xs, 5, g1, g2, "sigmoid", "rwkv_g")
    if v_lora is None:
        v_mix = None
        v0 = jnp.zeros_like(w0)
    else:
        v0, v1, v2 = v_lora
        mv, v2 = _lora_down(xs, 2, v1, v2, None, "rwkv_v")
        v_mix = (mv, v2, first_rkv)
    chan = jnp.stack([w0, a0, k_k, k_a, r_k, gn_g, gn_b, v0])
    y = rwkv_scan(rkv, (mw, ma, mg), (w2, a2, g2), v_mix, chan, seq)
    return matmul(y, w_o, BF16, layer=layer, name="rwkv_out"), rkv


def attn_layer(xb, seq, layer, w_qkv, w_o, sinks, bias):
    n_q = w_o.shape[1] // HEAD
    n_kv = (w_qkv.shape[2] // HEAD - n_q) // 2
    qkv = matmul(xb, w_qkv, BF16, layer=layer, name="attn_qkv")
    o = swa_attention(qkv, bias, sinks, seq, n_q, n_kv)
    return matmul(o, w_o, BF16, layer=layer, name="attn_out")


FFN_COL_BLOCK = 512
FFN_ROW_BLOCK = 1024
FFN_COL_SUB = MXU_DIM


def ffn_layer(xb, seq, layer, w_up, conv_w, conv_b, w_down):
    f = w_down.shape[1]
    conv_b = conv_b.reshape(1, -1)
    act = ffn_up(xb, w_up, layer, conv_w[:, :f], conv_w[:, f:], conv_b[:, :f], conv_b[:, f:],
                 seq, bm=FFN_ROW_BLOCK, bn=FFN_COL_BLOCK, sub=FFN_COL_SUB)
    return matmul(act, w_down, BF16, layer=layer, bm=512, bn=512, name="ffn_down")


def kernel(x, rwkv_mix, rwkv_w_rkv, rwkv_w_o, rwkv_w0, rwkv_w1, rwkv_w2, rwkv_a0, rwkv_a1, rwkv_a2, rwkv_g1, rwkv_g2, rwkv_k_k, rwkv_k_a, rwkv_r_k, rwkv_gn_g, rwkv_gn_b, rwkv_v0, rwkv_v1, rwkv_v2, attn_w_qkv, attn_w_o, attn_sinks, rel_bias, ffn_w_up, ffn_conv_w, ffn_conv_b, ffn_w_down, ln1_g, ln1_b, ln2_g, ln2_b):
    B, S, D = x.shape
    depth = ln1_g.shape[0]
    xf = x.reshape(B * S, D)
    xb = xf.astype(BF16)
    bias = bias_table(rel_bias, _bucket_map())
    w_rkv, w_ro = rwkv_w_rkv.astype(BF16), rwkv_w_o.astype(BF16)
    w_qkv, w_ao = attn_w_qkv.astype(BF16), attn_w_o.astype(BF16)
    w_up, w_down = ffn_w_up.astype(BF16), ffn_w_down.astype(BF16)
    first_rkv = None
    for i in range(depth):
        j = i // 2
        if i % 2 == 0:
            v_lora = None if j == 0 else (rwkv_v0[j - 1], rwkv_v1[j - 1], rwkv_v2[j - 1])
            h, rkv = rwkv_layer(xf, S, first_rkv, j, rwkv_mix[j], w_rkv, w_ro,
                                rwkv_w0[j], rwkv_w1[j], rwkv_w2[j], rwkv_a0[j],
                                rwkv_a1[j], rwkv_a2[j], rwkv_g1[j], rwkv_g2[j], rwkv_k_k[j],
                                rwkv_k_a[j], rwkv_r_k[j], rwkv_gn_g[j], rwkv_gn_b[j], v_lora)
            if v_lora is None:
                first_rkv = rkv
        else:
            h = attn_layer(xb, S, j, w_qkv, w_ao, attn_sinks[j], bias)
        xf, xb = ln_residual(xf, h, ln1_g[i], ln1_b[i])
        f = ffn_layer(xb, S, i, w_up, ffn_conv_w[i], ffn_conv_b[i], w_down)
        xf, xb = ln_residual(xf, f, ln2_g[i], ln2_b[i])
    return xf.reshape(B, S, D)
```

```python
import functools
import math

import jax
import jax.numpy as jnp
from jax import lax
from jax.experimental import pallas as pl
from jax.experimental.pallas import tpu as pltpu

F32 = jnp.float32
BF16 = jnp.bfloat16

LANES = 128
SUBLANES = 8
MXU_DIM = 256
VMEM_LIMIT_BYTES = 56 * 1024 * 1024

HEAD = 64
HEADS_PER_GROUP = MXU_DIM // HEAD
CHUNK = 64
STACK = HEADS_PER_GROUP * CHUNK

WINDOW = 128
BLOCK = 128
REL_BUCKETS = 32
REL_MAX_DIST = 128
CONV_W = 3
LN_EPS = 1e-5
GN_EPS = HEAD * 1e-5
DEPTH = 4
DEEPNORM_ALPHA = (2 * DEPTH) ** 0.25


def _params(*sem):
    return pltpu.CompilerParams(dimension_semantics=sem, vmem_limit_bytes=VMEM_LIMIT_BYTES)


def _block(n, target, unit):
    if n <= target:
        return n
    b = target - target % unit
    while n % b:
        b -= unit
    return b


def _sigmoid(x):
    return 1.0 / (1.0 + jnp.exp(-x))


def _dot(a, b):
    return jnp.dot(a, b, preferred_element_type=F32)


def _dot_nt(a, b):
    return lax.dot_general(a, b, (((1,), (1,)), ((), ())), preferred_element_type=F32)


def _dot_tn(a, b):
    return lax.dot_general(a, b, (((0,), (0,)), ((), ())), preferred_element_type=F32)


def _mm_body(x_ref, w_ref, o_ref, *, act):
    acc = _dot(x_ref[...], w_ref[...])
    if act == "tanh":
        acc = jnp.tanh(acc)
    elif act == "sigmoid":
        acc = _sigmoid(acc)
    o_ref[...] = acc.astype(o_ref.dtype)


def matmul(x, w, out_dtype, *, act=None, lead=None, layer=None, bm=1024, bn=512, name="matmul"):
    M, K = x.shape[-2:]
    N = w.shape[-1]
    bm, bn = _block(M, bm, SUBLANES), _block(N, bn, LANES)
    assert M % bm == 0 and N % bn == 0
    if lead is None:
        x_spec = pl.BlockSpec((bm, K), lambda i, j: (i, 0))
    else:
        x_spec = pl.BlockSpec((None, bm, K), lambda i, j: (lead, i, 0))
    if layer is None:
        w_spec = pl.BlockSpec((K, bn), lambda i, j: (0, j))
    else:
        w_spec = pl.BlockSpec((None, K, bn), lambda i, j: (layer, 0, j))
    return pl.pallas_call(
        functools.partial(_mm_body, act=act),
        grid=(M // bm, N // bn),
        in_specs=[x_spec, w_spec],
        out_specs=pl.BlockSpec((bm, bn), lambda i, j: (i, j)),
        out_shape=jax.ShapeDtypeStruct((M, N), out_dtype),
        compiler_params=_params("parallel", "parallel"),
        name=name,
    )(x, w)


def _bmm_body(x_ref, w_ref, o_ref):
    o_ref[...] = _dot(x_ref[...], w_ref[...]).astype(o_ref.dtype)


def batched_matmul(x, w, out_dtype, *, nbatch, layer, bm=1024, bn=512, name="bmm"):
    M, K = x.shape[-2:]
    N = w.shape[-1]
    bm, bn = _block(M, bm, SUBLANES), _block(N, bn, LANES)
    assert M % bm == 0 and N % bn == 0
    return pl.pallas_call(
        _bmm_body,
        grid=(nbatch, M // bm, N // bn),
        in_specs=[pl.BlockSpec((None, bm, K), lambda p, i, j: (p, i, 0)),
                  pl.BlockSpec((None, None, K, bn), lambda p, i, j: (layer, p, 0, j))],
        out_specs=pl.BlockSpec((None, bm, bn), lambda p, i, j: (p, i, j)),
        out_shape=jax.ShapeDtypeStruct((nbatch, M, N), out_dtype),
        compiler_params=_params("parallel", "parallel", "parallel"),
        name=name,
    )(x, w)


def _ln_body(x_ref, h_ref, g_ref, b_ref, of_ref, ob_ref):
    y = DEEPNORM_ALPHA * x_ref[...] + h_ref[...].astype(F32)
    mu = jnp.mean(y, axis=-1, keepdims=True)
    d = y - mu
    var = jnp.mean(d * d, axis=-1, keepdims=True)
    out = d * lax.rsqrt(var + LN_EPS) * g_ref[...] + b_ref[...]
    of_ref[...] = out
    ob_ref[...] = out.astype(BF16)


def ln_residual(x, h, g, b, *, br=256):
    T, D = x.shape
    br = min(br, T)
    assert T % br == 0
    row = pl.BlockSpec((br, D), lambda i: (i, 0))
    vec = pl.BlockSpec((1, D), lambda i: (0, 0))
    return pl.pallas_call(
        _ln_body,
        grid=(T // br,),
        in_specs=[row, row, vec, vec],
        out_specs=[row, row],
        out_shape=[jax.ShapeDtypeStruct((T, D), F32), jax.ShapeDtypeStruct((T, D), BF16)],
        compiler_params=_params("parallel"),
        name="ln_residual",
    )(x, h, g.reshape(1, D), b.reshape(1, D))


def _mix_body(x_ref, xp_ref, mix_ref, o_ref, *, blocks_per_seq):
    i = pl.program_id(0)
    x = x_ref[...]
    prev_last = xp_ref[SUBLANES - 1:SUBLANES, :]
    prev_last = jnp.where(i % blocks_per_seq == 0, 0.0, prev_last)
    row = lax.broadcasted_iota(jnp.int32, x.shape, 0)
    shifted = jnp.where(row == 0, prev_last, pltpu.roll(x, 1, 0))
    xx = shifted - x
    for p in range(6):
        o_ref[p] = (x + xx * mix_ref[p:p + 1, :]).astype(BF16)


def token_shift_mix(x, mix, seq, *, bt=512, bd=1024):
    T, D = x.shape
    bt, bd = min(bt, seq), min(bd, D)
    assert seq % bt == 0 and D % bd == 0 and bt % SUBLANES == 0
    halo = bt // SUBLANES
    return pl.pallas_call(
        functools.partial(_mix_body, blocks_per_seq=seq // bt),
        grid=(T // bt, D // bd),
        in_specs=[pl.BlockSpec((bt, bd), lambda i, j: (i, j)),
                  pl.BlockSpec((SUBLANES, bd), lambda i, j: (jnp.maximum(i * halo - 1, 0), j)),
                  pl.BlockSpec((6, bd), lambda i, j: (0, j))],
        out_specs=pl.BlockSpec((6, bt, bd), lambda i, j: (0, i, j)),
        out_shape=jax.ShapeDtypeStruct((6, T, D), BF16),
        compiler_params=_params("parallel", "parallel"),
        name="rwkv_mix",
    )(x, x, mix)


PREP_STAGE, HEAD_STAGE, TAIL_STAGE = "prep", "head", "tail"
HEAD_STAGES_BEFORE_NEXT_GROUP = 8


def _scan_body(*refs, n_chunks, has_vlora, groups):
    shared = (3, 4, 5, 9) if has_vlora else (3, 4, 5)
    pending = []
    for g in range(groups):
        lanes = pl.ds(g * MXU_DIM, MXU_DIM)
        views = [ref if i in shared else ref.at[:, lanes] for i, ref in enumerate(refs[:-1])]
        pending.append(_scan_group(*views, refs[-1].at[g], n_chunks=n_chunks, has_vlora=has_vlora))
    active = []
    while pending or active:
        if pending and all(done >= HEAD_STAGES_BEFORE_NEXT_GROUP for _, done in active):
            active.append([pending.pop(0), 0])
        for entry in list(active):
            try:
                entry[1] += next(entry[0]) == HEAD_STAGE
            except StopIteration:
                active.remove(entry)


def _scan_group(*refs, n_chunks, has_vlora):
    if has_vlora:
        (r_ref, k_ref, v_ref, mw_ref, ma_ref, mg_ref, w2_ref, a2_ref, g2_ref,
         mv_ref, v2_ref, vf_ref, p_ref, o_ref, s_ref) = refs
    else:
        r_ref, k_ref, v_ref, mw_ref, ma_ref, mg_ref, w2_ref, a2_ref, g2_ref, p_ref, o_ref, s_ref = refs

    @pl.when(pl.program_id(2) == 0)
    def _():
        s_ref[...] = jnp.zeros(s_ref.shape, s_ref.dtype)

    W = MXU_DIM
    head_of_row = lax.broadcasted_iota(jnp.int32, (STACK, W), 0) // CHUNK
    head_of_lane = lax.broadcasted_iota(jnp.int32, (STACK, W), 1) // HEAD
    head_mask = head_of_row == head_of_lane
    rr = lax.broadcasted_iota(jnp.int32, (STACK, STACK), 0)
    cc = lax.broadcasted_iota(jnp.int32, (STACK, STACK), 1)
    same_head = (rr // CHUNK) == (cc // CHUNK)
    strict_lower = same_head & ((rr % CHUNK) > (cc % CHUNK))
    eye = (rr == cc).astype(F32)
    half_mask = ((lax.broadcasted_iota(jnp.int32, (STACK, LANES), 0) // CHUNK) % 2
                 == lax.broadcasted_iota(jnp.int32, (STACK, LANES), 1) // HEAD)
    natural_lower = (lax.broadcasted_iota(jnp.int32, (CHUNK, 4 * CHUNK), 0)
                     >= lax.broadcasted_iota(jnp.int32, (CHUNK, 4 * CHUNK), 1) % CHUNK)
    lane_r = lax.broadcasted_iota(jnp.int32, (W, W), 0) // HEAD
    lane_c = lax.broadcasted_iota(jnp.int32, (W, W), 1) // HEAD
    seg_ones = (lane_r == lane_c).astype(BF16)
    bt = n_chunks * CHUNK
    tr = lax.broadcasted_iota(jnp.int32, (bt, bt), 0)
    tc = lax.broadcasted_iota(jnp.int32, (bt, bt), 1)
    tri = ((tr // CHUNK == tc // CHUNK) & (tr >= tc)).astype(BF16)

    w0 = p_ref[0:1, :]
    a0 = p_ref[1:2, :]
    k_k = p_ref[2:3, :]
    k_a = p_ref[3:4, :]
    r_k = p_ref[4:5, :]
    gn_g = p_ref[5:6, :]
    gn_b = p_ref[6:7, :]
    v0 = p_ref[7:8, :]

    nc = n_chunks
    chunks = range(nc)

    r = r_ref[...]
    k = k_ref[...]
    v = v_ref[...]
    z = w0 + _dot(mw_ref[...], w2_ref[...])
    w_log = -(jnp.maximum(-z, 0.0) + jnp.log(1.0 + jnp.exp(-jnp.abs(z)))) - 0.5
    log_decay = -jnp.exp(w_log)
    yield PREP_STAGE
    a = _sigmoid(a0 + _dot(ma_ref[...], a2_ref[...]))
    if has_vlora:
        v = v + (vf_ref[...] - v) * _sigmoid(v0 + _dot(mv_ref[...], v2_ref[...]))
    yield PREP_STAGE
    kk = k * k_k
    norm = jnp.sqrt(_dot((kk * kk).astype(BF16), seg_ones))
    kk = kk / jnp.maximum(norm, 1e-12)
    k = k * (1.0 + (a - 1.0) * k_a)
    yield PREP_STAGE

    cum = _split_dot_left(tri, log_decay)
    cum_last = jnp.concatenate(
        [jnp.broadcast_to(cum[(c + 1) * CHUNK - 1:(c + 1) * CHUNK, :], (CHUNK, W)) for c in chunks], axis=0)
    e_pos = jnp.exp(cum)
    e_neg = jnp.exp(-cum)
    yield PREP_STAGE
    e_prev = jnp.exp(cum - log_decay)
    e_tail = jnp.exp(cum_last - cum)
    kka = kk * a
    yield PREP_STAGE
    a_e = -kk * e_prev
    r_e = r * e_pos
    b_e = kka * e_neg
    yield PREP_STAGE
    k_e = k * e_neg
    b_end = kka * e_tail
    k_end = k * e_tail

    def rows_of(x, c):
        return x[c * CHUNK:(c + 1) * CHUNK]

    def stack(x, c):
        return jnp.where(head_mask, jnp.concatenate([rows_of(x, c)] * HEADS_PER_GROUP, axis=0), 0.0)

    def stack_pair(x, c):
        xc = rows_of(x, c)
        parts = [xc[:, (h // 2) * LANES:(h // 2 + 1) * LANES] for h in range(HEADS_PER_GROUP)]
        return jnp.where(half_mask, jnp.concatenate(parts, axis=0), 0.0).astype(BF16)

    n_pairs = HEADS_PER_GROUP // 2
    pair_rows = [slice(q * 2 * CHUNK, (q + 1) * 2 * CHUNK) for q in range(n_pairs)]
    pair_lanes = [slice(q * LANES, (q + 1) * LANES) for q in range(n_pairs)]

    yield HEAD_STAGE

    a_p = [stack_pair(a_e, c) for c in chunks]
    b_p = [stack_pair(b_e, c) for c in chunks]
    k_p = [stack_pair(k_e, c) for c in chunks]
    v_p = [stack_pair(v, c) for c in chunks]
    p_a = [_dot_nt(a_p[c], jnp.concatenate([b_p[c], k_p[c]], axis=0)) for c in chunks]
    l_ab = [jnp.where(strict_lower, p_a[c][:, :STACK], 0.0) for c in chunks]
    l_ak = [jnp.where(strict_lower, p_a[c][:, STACK:], 0.0).astype(BF16) for c in chunks]
    yield HEAD_STAGE

    l_r = [[jnp.where(natural_lower,
                      _dot_nt(rows_of(r_e, c)[:, pair_lanes[q]].astype(BF16),
                              jnp.concatenate([b_p[c][pair_rows[q]], k_p[c][pair_rows[q]]], axis=0)),
                      0.0).astype(BF16) for q in range(n_pairs)] for c in chunks]
    yield HEAD_STAGE

    inv = [eye + l_ab[c] for c in chunks]
    power = [l_ab[c].astype(BF16) for c in chunks]
    for _ in range(int(math.log2(CHUNK)) - 1):
        power = [_dot(power[c], power[c]).astype(BF16) for c in chunks]
        yield HEAD_STAGE
        inv = [inv[c] + _dot(inv[c].astype(BF16), power[c]) for c in chunks]
        yield HEAD_STAGE

    x = [jnp.concatenate([a_p[c], _dot(l_ak[c], v_p[c]).astype(BF16)], axis=1) for c in chunks]
    yield HEAD_STAGE
    wu = [_dot(inv[c].astype(BF16), x[c]).astype(BF16) for c in chunks]
    yield HEAD_STAGE

    g_mat, n_mat, r_eff, y0 = [], [], [], []
    zero_pad = jnp.zeros((2 * CHUNK, LANES), BF16)
    for c in chunks:
        gn = _dot_tn(wu[c], stack(b_end, c).astype(BF16))
        n_mat.append(gn[LANES:] + _dot_tn(v_p[c], stack(k_end, c).astype(BF16)))
        g_mat.append(jnp.where(head_mask, jnp.concatenate([gn[:LANES]] * n_pairs, axis=0), 0.0).astype(BF16))
        yield TAIL_STAGE
        re_q, y0_q = [], []
        for q in range(n_pairs):
            rhs = jnp.concatenate([wu[c][pair_rows[q]],
                                   jnp.concatenate([zero_pad, v_p[c][pair_rows[q]]], axis=1)], axis=0)
            ry = _dot(l_r[c][q], rhs)
            re_q.append(rows_of(r_e, c)[:, pair_lanes[q]] + ry[:, :LANES])
            y0_q.append(ry[:, LANES:])
        r_eff.append(jnp.concatenate(re_q, axis=1).astype(BF16))
        y0.append(jnp.concatenate(y0_q, axis=1))
        yield TAIL_STAGE
    decay_end = [jnp.exp(cum[(c + 1) * CHUNK - 1:(c + 1) * CHUNK, :]) for c in chunks]

    state = s_ref[...]
    ys = []
    for c in chunks:
        sb = state.astype(BF16)
        s_big = jnp.where(head_mask, jnp.concatenate([sb] * n_pairs, axis=0), jnp.zeros((), BF16))
        ys.append(_dot_nt(r_eff[c], s_big) + y0[c])
        state = state * decay_end[c] + _dot(sb, g_mat[c]) + n_mat[c]
        yield TAIL_STAGE
    s_ref[...] = state
    y = jnp.concatenate(ys, axis=0)

    mu = _dot(y.astype(BF16), seg_ones) * (1.0 / HEAD)
    d = y - mu
    var = _dot((d * d).astype(BF16), seg_ones) * (1.0 / HEAD)
    yield TAIL_STAGE
    yn = d * lax.rsqrt(var + GN_EPS) * gn_g + gn_b
    bonus = _dot((r * k * r_k).astype(BF16), seg_ones) * v
    gate = _dot(mg_ref[...], g2_ref[...])
    o_ref[...] = ((yn + bonus) * gate).astype(o_ref.dtype)
    yield TAIL_STAGE


def _split_dot_left(e, x):
    hi = x.astype(BF16)
    lo = (x - hi.astype(F32)).astype(BF16)
    return _dot(e, hi) + _dot(e, lo)


def rwkv_scan(rkv, mids, ups, v_mix, chan, seq, *, bt=256, groups=4):
    _, T, D = rkv.shape
    groups = min(groups, D // MXU_DIM)
    W = groups * MXU_DIM
    bt = min(bt, seq)
    assert D % W == 0 and seq % bt == 0 and bt % CHUNK == 0
    has_vlora = v_mix is not None
    nb = seq // bt

    def stacked(p):
        return pl.BlockSpec((None, bt, W), lambda b, gidx, t: (p, b * nb + t, gidx))

    def mid(a):
        return pl.BlockSpec((bt, a.shape[1]), lambda b, gidx, t: (b * nb + t, 0))

    def up(a):
        return pl.BlockSpec((a.shape[0], W), lambda b, gidx, t: (0, gidx))

    ins = [rkv, rkv, rkv, *mids, *ups]
    specs = [stacked(0), stacked(1), stacked(2)] + [mid(a) for a in mids] + [up(a) for a in ups]
    if has_vlora:
        mv, v2, first_rkv = v_mix
        ins += [mv, v2, first_rkv]
        specs += [mid(mv), up(v2), stacked(2)]
    ins.append(chan)
    specs.append(pl.BlockSpec((SUBLANES, W), lambda b, gidx, t: (0, gidx)))
    tok = pl.BlockSpec((bt, W), lambda b, gidx, t: (b * nb + t, gidx))
    return pl.pallas_call(
        functools.partial(_scan_body, n_chunks=bt // CHUNK, has_vlora=has_vlora, groups=groups),
        grid=(T // seq, D // W, nb),
        in_specs=specs,
        out_specs=tok,
        out_shape=jax.ShapeDtypeStruct((T, D), BF16),
        scratch_shapes=[pltpu.VMEM((groups, 2 * HEAD, MXU_DIM), F32)],
        compiler_params=_params("parallel", "parallel", "arbitrary"),
        name="rwkv_scan",
    )(*ins)


def _bias_body(bmap_ref, rb_ref, o_ref):
    h = pl.program_id(0)
    bmap = bmap_ref[...]
    acc = jnp.full(bmap.shape, -jnp.inf, F32)
    for b in range(REL_BUCKETS):
        acc = jnp.where(bmap == b, rb_ref[b, h], acc)
    o_ref[0, 0] = acc
    kj = lax.broadcasted_iota(jnp.int32, bmap.shape, 1)
    o_ref[1, 0] = jnp.where(kj >= BLOCK, acc, -jnp.inf)


def bias_table(rel_bias, bmap):
    H = rel_bias.shape[1]
    return pl.pallas_call(
        _bias_body,
        grid=(H,),
        in_specs=[pl.BlockSpec((BLOCK, 2 * BLOCK), lambda h: (0, 0)),
                  pl.BlockSpec(memory_space=pltpu.SMEM)],
        out_specs=pl.BlockSpec((2, 1, BLOCK, 2 * BLOCK), lambda h: (0, h, 0, 0)),
        out_shape=jax.ShapeDtypeStruct((2, H, BLOCK, 2 * BLOCK), F32),
        compiler_params=_params("parallel"),
        name="rel_bias_table",
    )(bmap, rel_bias)


def _attn_body(q_ref, kp_ref, kc_ref, vp_ref, vc_ref, bias_ref, sink_ref, o_ref, *, group, n_kv):
    low = lax.broadcasted_iota(jnp.int32, (BLOCK, LANES), 1) < HEAD
    scale = HEAD ** -0.5

    for kv_pair in range(n_kv // 2):
        lanes = pl.ds(kv_pair * LANES, LANES)
        kband = jnp.concatenate([kp_ref[:, lanes], kc_ref[:, lanes]], axis=0)
        vband = jnp.concatenate([vp_ref[:, lanes], vc_ref[:, lanes]], axis=0)
        kswap = pltpu.roll(kband, HEAD, 1)
        vswap = pltpu.roll(vband, HEAD, 1)
        for e in range(2):
            kvh = 2 * kv_pair + e
            k_at = (kband, kswap) if e == 0 else (kswap, kband)
            v_at = (vband, vswap) if e == 0 else (vswap, vband)
            for pair in range(group // 2):
                q_lanes = pl.ds((kvh * group // 2 + pair) * LANES, LANES)
                qp = q_ref[:, q_lanes] * scale
                halves = []
                for half in range(2):
                    h = kvh * group + 2 * pair + half
                    in_half = low if half == 0 else jnp.logical_not(low)
                    qm = jnp.where(in_half, qp, jnp.zeros_like(qp))
                    s = _dot_nt(qm, k_at[half]) + bias_ref[0, h]
                    sink = sink_ref[h]
                    m = jnp.maximum(jnp.max(s, axis=-1, keepdims=True), sink)
                    p = jnp.exp(s - m)
                    denom = jnp.sum(p, axis=-1, keepdims=True) + jnp.exp(sink - m)
                    halves.append(_dot(p.astype(BF16), v_at[half]) / denom)
                o_ref[:, q_lanes] = jnp.where(low, halves[0], halves[1]).astype(o_ref.dtype)


def swa_attention(qkv, bias, sinks, seq, n_q_heads, n_kv_heads):
    T = qkv.shape[0]
    group = n_q_heads // n_kv_heads
    assert group % 2 == 0 and n_kv_heads % 2 == 0 and seq % BLOCK == 0
    nb = seq // BLOCK
    qw, kw = n_q_heads * HEAD, n_kv_heads * HEAD
    assert qw % kw == 0 and kw % LANES == 0
    k_col, v_col = qw // kw, qw // kw + 1

    def cur(col):
        return pl.BlockSpec((BLOCK, kw), lambda b, n: (b * nb + n, col))

    def prev(col):
        return pl.BlockSpec((BLOCK, kw), lambda b, n: (b * nb + jnp.maximum(n - 1, 0), col))

    return pl.pallas_call(
        functools.partial(_attn_body, group=group, n_kv=n_kv_heads),
        grid=(T // seq, nb),
        in_specs=[pl.BlockSpec((BLOCK, qw), lambda b, n: (b * nb + n, 0)),
                  prev(k_col), cur(k_col), prev(v_col), cur(v_col),
                  pl.BlockSpec((1, n_q_heads, BLOCK, 2 * BLOCK), lambda b, n: (jnp.where(n == 0, 1, 0), 0, 0, 0)),
                  pl.BlockSpec(memory_space=pltpu.SMEM)],
        out_specs=pl.BlockSpec((BLOCK, qw), lambda b, n: (b * nb + n, 0)),
        out_shape=jax.ShapeDtypeStruct((T, qw), BF16),
        compiler_params=_params("parallel", "parallel"),
        name="swa_attention",
    )(qkv, qkv, qkv, qkv, qkv, bias, sinks)


def _bucket_map():
    qi = jnp.arange(BLOCK)[:, None]
    kj = jnp.arange(2 * BLOCK)[None, :]
    signed = qi + BLOCK - kj
    dist = jnp.maximum(signed, 0)
    max_exact = REL_BUCKETS // 2
    d_f = jnp.maximum(dist, max_exact).astype(F32)
    large = max_exact + (jnp.log(d_f / max_exact) / math.log(REL_MAX_DIST / max_exact)
                         * (REL_BUCKETS - max_exact)).astype(jnp.int32)
    large = jnp.minimum(large, REL_BUCKETS - 1)
    bucket = jnp.where(dist < max_exact, dist, large)
    return jnp.where((signed >= 0) & (signed < WINDOW), bucket, -1).astype(jnp.int32)


def _ffn_up_body(x_ref, *refs, blocks_per_seq, sub, rsub, n_sub):
    wg_refs, wu_refs = refs[:n_sub], refs[n_sub:2 * n_sub]
    cg_ref, cu_ref, bg_ref, bu_ref, o_ref, tail_g_ref, tail_u_ref, raw_g_ref, raw_u_ref = refs[2 * n_sub:]
    first = pl.program_id(0) % blocks_per_seq == 0
    j = pl.program_id(1)
    bm = x_ref.shape[0]
    H = SUBLANES

    @pl.when((pl.program_id(0) == 0) & (j == 0))
    def _():
        tail_g_ref[...] = jnp.zeros_like(tail_g_ref)
        tail_u_ref[...] = jnp.zeros_like(tail_u_ref)

    def matmul_tile(slot, s, rb, halo_g, halo_u):
        xr = x_ref[pl.ds(rb * rsub, rsub), :]
        raw_g_ref[slot, 0:H, :] = halo_g
        raw_u_ref[slot, 0:H, :] = halo_u
        raw_g_ref[slot, H:, :] = _dot(xr, wg_refs[s][...])
        raw_u_ref[slot, H:, :] = _dot(xr, wu_refs[s][...])
        return raw_g_ref[slot, rsub:, :], raw_u_ref[slot, rsub:, :]

    def conv(raw_ref, slot, c_ref, b_ref, cols):
        h = raw_ref[slot, H:, :]
        h1 = raw_ref[slot, H - 1:H - 1 + rsub, :]
        h2 = raw_ref[slot, H - 2:H - 2 + rsub, :]
        return c_ref[0:1, cols] * h2 + c_ref[1:2, cols] * h1 + c_ref[2:3, cols] * h + b_ref[:, cols]

    def finish_tile(slot, s, rb):
        cols = pl.ds(s * sub, sub)
        gate = conv(raw_g_ref, slot, cg_ref, bg_ref, cols)
        up = conv(raw_u_ref, slot, cu_ref, bu_ref, cols)
        o_ref[pl.ds(rb * rsub, rsub), cols] = (gate * _sigmoid(gate) * up).astype(o_ref.dtype)

    n_rows = bm // rsub
    prev = None
    t = 0
    for s in range(n_sub):
        cols = pl.ds(s * sub, sub)
        halo_g = jnp.where(first, 0.0, tail_g_ref[j, :, cols])
        halo_u = jnp.where(first, 0.0, tail_u_ref[j, :, cols])
        for rb in range(n_rows):
            halo_g, halo_u = matmul_tile(t % 2, s, rb, halo_g, halo_u)
            if prev is not None:
                finish_tile(*prev)
            prev = (t % 2, s, rb)
            t += 1
        tail_g_ref[j, :, cols] = halo_g
        tail_u_ref[j, :, cols] = halo_u
    finish_tile(*prev)


def ffn_up(x, w2, layer, cg, cu, bg, bu, seq, *, bm=1024, bn=512, sub=256, rsub=256):
    T, D = x.shape
    F = w2.shape[2] // 2
    bm, bn = min(bm, seq), min(bn, F)
    sub, rsub = min(sub, bn), min(rsub, bm)
    assert seq % bm == 0 and bn % sub == 0 and F % sub == 0 and bm % rsub == 0 and rsub % SUBLANES == 0
    nf = pl.cdiv(F, bn)
    n_sub = bn // sub
    last = 2 * F // sub - 1

    def wspec(s, half):
        return pl.BlockSpec((None, D, sub),
                            lambda i, j: (layer, 0, jnp.minimum(half * (F // sub) + j * n_sub + s, last)))

    col = lambda rows: pl.BlockSpec((rows, bn), lambda i, j: (0, j))
    return pl.pallas_call(
        functools.partial(_ffn_up_body, blocks_per_seq=seq // bm, sub=sub, rsub=rsub, n_sub=n_sub),
        grid=(T // bm, nf),
        in_specs=[pl.BlockSpec((bm, D), lambda i, j: (i, 0))]
                 + [wspec(s, 0) for s in range(n_sub)] + [wspec(s, 1) for s in range(n_sub)]
                 + [col(CONV_W), col(CONV_W), col(1), col(1)],
        out_specs=pl.BlockSpec((bm, bn), lambda i, j: (i, j)),
        out_shape=jax.ShapeDtypeStruct((T, F), BF16),
        scratch_shapes=[pltpu.VMEM((nf, SUBLANES, bn), F32), pltpu.VMEM((nf, SUBLANES, bn), F32),
                        pltpu.VMEM((2, SUBLANES + rsub, sub), F32), pltpu.VMEM((2, SUBLANES + rsub, sub), F32)],
        compiler_params=_params("arbitrary", "arbitrary"),
        name="ffn_up",
    )(x, *([w2] * (2 * n_sub)), cg, cu, bg, bu)


def _pad_to(x, axis, mult):
    pad = (-x.shape[axis]) % mult
    if pad == 0:
        return x
    widths = [(0, 0)] * x.ndim
    widths[axis] = (0, pad)
    return jnp.pad(x, widths)


WIDE_COL_BLOCK = 1024


def _lora_down(xs, lead, w_down, w_up, act, name):
    w_down = _pad_to(w_down, 1, LANES).astype(BF16)
    w_up = _pad_to(w_up, 0, LANES).astype(BF16)
    return matmul(xs, w_down, BF16, act=act, lead=lead, name=name + "_down"), w_up


def rwkv_layer(x, seq, first_rkv, layer, mix, w_rkv, w_o, w0, w1, w2, a0, a1, a2, g1, g2,
               k_k, k_a, r_k, gn_g, gn_b, v_lora):
    xs = token_shift_mix(x, mix[jnp.array([0, 2, 3, 1, 4, 5])], seq)
    rkv = batched_matmul(xs, w_rkv, F32, nbatch=3, layer=layer, bn=WIDE_COL_BLOCK, name="rwkv_rkv")
    mw, w2 = _lora_down(xs, 3, w1, w2, "tanh", "rwkv_w")
    ma, a2 = _lora_down(xs, 4, a1, a2, None, "rwkv_a")
    mg, g2 = _lora_down(xs, 5, g1, g2, "sigmoid", "rwkv_g")
    if v_lora is None:
        v_mix = None
        v0 = jnp.zeros_like(w0)
    else:
        v0, v1, v2 = v_lora
        mv, v2 = _lora_down(xs, 2, v1, v2, None, "rwkv_v")
        v_mix = (mv, v2, first_rkv)
    chan = jnp.stack([w0, a0, k_k, k_a, r_k, gn_g, gn_b, v0])
    y = rwkv_scan(rkv, (mw, ma, mg), (w2, a2, g2), v_mix, chan, seq)
    return matmul(y, w_o, BF16, layer=layer, bn=WIDE_COL_BLOCK, name="rwkv_out"), rkv


def attn_layer(xb, seq, layer, w_qkv, w_o, sinks, bias):
    n_q = w_o.shape[1] // HEAD
    n_kv = (w_qkv.shape[2] // HEAD - n_q) // 2
    qkv = matmul(xb, w_qkv, BF16, layer=layer, bn=WIDE_COL_BLOCK, name="attn_qkv")
    o = swa_attention(qkv, bias, sinks, seq, n_q, n_kv)
    return matmul(o, w_o, BF16, layer=layer, bn=WIDE_COL_BLOCK, name="attn_out")


FFN_COL_BLOCK = 512
FFN_ROW_BLOCK = 1024
FFN_COL_SUB = MXU_DIM


def ffn_layer(xb, seq, layer, w_up, conv_w, conv_b, w_down):
    f = w_down.shape[1]
    conv_b = conv_b.reshape(1, -1)
    act = ffn_up(xb, w_up, layer, conv_w[:, :f], conv_w[:, f:], conv_b[:, :f], conv_b[:, f:],
                 seq, bm=FFN_ROW_BLOCK, bn=FFN_COL_BLOCK, sub=FFN_COL_SUB)
    return matmul(act, w_down, BF16, layer=layer, bm=512, bn=512, name="ffn_down")


def kernel(x, rwkv_mix, rwkv_w_rkv, rwkv_w_o, rwkv_w0, rwkv_w1, rwkv_w2, rwkv_a0, rwkv_a1, rwkv_a2, rwkv_g1, rwkv_g2, rwkv_k_k, rwkv_k_a, rwkv_r_k, rwkv_gn_g, rwkv_gn_b, rwkv_v0, rwkv_v1, rwkv_v2, attn_w_qkv, attn_w_o, attn_sinks, rel_bias, ffn_w_up, ffn_conv_w, ffn_conv_b, ffn_w_down, ln1_g, ln1_b, ln2_g, ln2_b):
    B, S, D = x.shape
    depth = ln1_g.shape[0]
    xf = x.reshape(B * S, D)
    xb = xf.astype(BF16)
    bias = bias_table(rel_bias, _bucket_map())
    w_rkv, w_ro = rwkv_w_rkv.astype(BF16), rwkv_w_o.astype(BF16)
    w_qkv, w_ao = attn_w_qkv.astype(BF16), attn_w_o.astype(BF16)
    w_up, w_down = ffn_w_up.astype(BF16), ffn_w_down.astype(BF16)
    first_rkv = None
    for i in range(depth):
        j = i // 2
        if i % 2 == 0:
            v_lora = None if j == 0 else (rwkv_v0[j - 1], rwkv_v1[j - 1], rwkv_v2[j - 1])
            h, rkv = rwkv_layer(xf, S, first_rkv, j, rwkv_mix[j], w_rkv, w_ro,
                                rwkv_w0[j], rwkv_w1[j], rwkv_w2[j], rwkv_a0[j],
                                rwkv_a1[j], rwkv_a2[j], rwkv_g1[j], rwkv_g2[j], rwkv_k_k[j],
                                rwkv_k_a[j], rwkv_r_k[j], rwkv_gn_g[j], rwkv_gn_b[j], v_lora)
            if v_lora is None:
                first_rkv = rkv
        else:
            h = attn_layer(xb, S, j, w_qkv, w_ao, attn_sinks[j], bias)
        xf, xb = ln_residual(xf, h, ln1_g[i], ln1_b[i])
        f = ffn_layer(xb, S, i, w_up, ffn_conv_w[i], ffn_conv_b[i], w_down)
        xf, xb = ln_residual(xf, f, ln2_g[i], ln2_b[i])
    return xf.reshape(B, S, D)
```

```python
import functools
import math

import jax
import jax.numpy as jnp
from jax import lax
from jax.experimental import pallas as pl
from jax.experimental.pallas import tpu as pltpu

F32 = jnp.float32
BF16 = jnp.bfloat16

LANES = 128
SUBLANES = 8
MXU_DIM = 256
VMEM_LIMIT_BYTES = 56 * 1024 * 1024

HEAD = 64
HEADS_PER_GROUP = MXU_DIM // HEAD
CHUNK = 64
STACK = HEADS_PER_GROUP * CHUNK

WINDOW = 128
BLOCK = 128
REL_BUCKETS = 32
REL_MAX_DIST = 128
CONV_W = 3
LN_EPS = 1e-5
GN_EPS = HEAD * 1e-5
DEPTH = 4
DEEPNORM_ALPHA = (2 * DEPTH) ** 0.25


def _params(*sem):
    return pltpu.CompilerParams(dimension_semantics=sem, vmem_limit_bytes=VMEM_LIMIT_BYTES)


def _block(n, target, unit):
    if n <= target:
        return n
    b = target - target % unit
    while n % b:
        b -= unit
    return b


def _sigmoid(x):
    return 1.0 / (1.0 + jnp.exp(-x))


def _dot(a, b):
    return jnp.dot(a, b, preferred_element_type=F32)


def _dot_nt(a, b):
    return lax.dot_general(a, b, (((1,), (1,)), ((), ())), preferred_element_type=F32)


def _dot_tn(a, b):
    return lax.dot_general(a, b, (((0,), (0,)), ((), ())), preferred_element_type=F32)


def _mm_body(x_ref, w_ref, o_ref, *, act):
    acc = _dot(x_ref[...], w_ref[...])
    if act == "tanh":
        acc = jnp.tanh(acc)
    elif act == "sigmoid":
        acc = _sigmoid(acc)
    o_ref[...] = acc.astype(o_ref.dtype)


def matmul(x, w, out_dtype, *, act=None, lead=None, layer=None, bm=1024, bn=512, name="matmul"):
    M, K = x.shape[-2:]
    N = w.shape[-1]
    bm, bn = _block(M, bm, SUBLANES), _block(N, bn, LANES)
    assert M % bm == 0 and N % bn == 0
    if lead is None:
        x_spec = pl.BlockSpec((bm, K), lambda i, j: (i, 0))
    else:
        x_spec = pl.BlockSpec((None, bm, K), lambda i, j: (lead, i, 0))
    if layer is None:
        w_spec = pl.BlockSpec((K, bn), lambda i, j: (0, j))
    else:
        w_spec = pl.BlockSpec((None, K, bn), lambda i, j: (layer, 0, j))
    return pl.pallas_call(
        functools.partial(_mm_body, act=act),
        grid=(M // bm, N // bn),
        in_specs=[x_spec, w_spec],
        out_specs=pl.BlockSpec((bm, bn), lambda i, j: (i, j)),
        out_shape=jax.ShapeDtypeStruct((M, N), out_dtype),
        compiler_params=_params("parallel", "parallel"),
        name=name,
    )(x, w)


def _bmm_body(x_ref, w_ref, o_ref):
    o_ref[...] = _dot(x_ref[...], w_ref[...]).astype(o_ref.dtype)


def batched_matmul(x, w, out_dtype, *, nbatch, layer, bm=1024, bn=512, name="bmm"):
    M, K = x.shape[-2:]
    N = w.shape[-1]
    bm, bn = _block(M, bm, SUBLANES), _block(N, bn, LANES)
    assert M % bm == 0 and N % bn == 0
    return pl.pallas_call(
        _bmm_body,
        grid=(nbatch, M // bm, N // bn),
        in_specs=[pl.BlockSpec((None, bm, K), lambda p, i, j: (p, i, 0)),
                  pl.BlockSpec((None, None, K, bn), lambda p, i, j: (layer, p, 0, j))],
        out_specs=pl.BlockSpec((None, bm, bn), lambda p, i, j: (p, i, j)),
        out_shape=jax.ShapeDtypeStruct((nbatch, M, N), out_dtype),
        compiler_params=_params("parallel", "parallel", "parallel"),
        name=name,
    )(x, w)


def _ln_body(x_ref, h_ref, g_ref, b_ref, of_ref, ob_ref):
    y = DEEPNORM_ALPHA * x_ref[...] + h_ref[...].astype(F32)
    mu = jnp.mean(y, axis=-1, keepdims=True)
    d = y - mu
    var = jnp.mean(d * d, axis=-1, keepdims=True)
    out = d * lax.rsqrt(var + LN_EPS) * g_ref[...] + b_ref[...]
    of_ref[...] = out
    ob_ref[...] = out.astype(BF16)


def ln_residual(x, h, g, b, *, br=256):
    T, D = x.shape
    br = min(br, T)
    assert T % br == 0
    row = pl.BlockSpec((br, D), lambda i: (i, 0))
    vec = pl.BlockSpec((1, D), lambda i: (0, 0))
    return pl.pallas_call(
        _ln_body,
        grid=(T // br,),
        in_specs=[row, row, vec, vec],
        out_specs=[row, row],
        out_shape=[jax.ShapeDtypeStruct((T, D), F32), jax.ShapeDtypeStruct((T, D), BF16)],
        compiler_params=_params("parallel"),
        name="ln_residual",
    )(x, h, g.reshape(1, D), b.reshape(1, D))


def _mix_body(x_ref, xp_ref, mix_ref, o_ref, *, blocks_per_seq):
    i = pl.program_id(0)
    x = x_ref[...]
    prev_last = xp_ref[SUBLANES - 1:SUBLANES, :]
    prev_last = jnp.where(i % blocks_per_seq == 0, 0.0, prev_last)
    row = lax.broadcasted_iota(jnp.int32, x.shape, 0)
    shifted = jnp.where(row == 0, prev_last, pltpu.roll(x, 1, 0))
    xx = shifted - x
    for p in range(6):
        o_ref[p] = (x + xx * mix_ref[p:p + 1, :]).astype(BF16)


def token_shift_mix(x, mix, seq, *, bt=512, bd=1024):
    T, D = x.shape
    bt, bd = min(bt, seq), min(bd, D)
    assert seq % bt == 0 and D % bd == 0 and bt % SUBLANES == 0
    halo = bt // SUBLANES
    return pl.pallas_call(
        functools.partial(_mix_body, blocks_per_seq=seq // bt),
        grid=(T // bt, D // bd),
        in_specs=[pl.BlockSpec((bt, bd), lambda i, j: (i, j)),
                  pl.BlockSpec((SUBLANES, bd), lambda i, j: (jnp.maximum(i * halo - 1, 0), j)),
                  pl.BlockSpec((6, bd), lambda i, j: (0, j))],
        out_specs=pl.BlockSpec((6, bt, bd), lambda i, j: (0, i, j)),
        out_shape=jax.ShapeDtypeStruct((6, T, D), BF16),
        compiler_params=_params("parallel", "parallel"),
        name="rwkv_mix",
    )(x, x, mix)


PREP_STAGE, HEAD_STAGE, TAIL_STAGE = "prep", "head", "tail"
HEAD_STAGES_BEFORE_NEXT_GROUP = 11


def _scan_body(*refs, n_chunks, has_vlora, groups):
    n_in = len(refs) - 9
    ins, o_ref = refs[:n_in], refs[n_in]
    s_ref, stash = refs[n_in + 1], refs[n_in + 2:]
    t = pl.program_id(2)
    last = pl.num_programs(2) - 1
    write_slot = t % 2
    read_slot = 1 - write_slot
    shared = (3, 4, 5, 9) if has_vlora else (3, 4, 5)
    lanes = [pl.ds(g * MXU_DIM, MXU_DIM) for g in range(groups)]

    def heads():
        return [_scan_head(*[ref if i in shared else ref.at[:, lanes[g]] for i, ref in enumerate(ins)],
                           *[ref.at[write_slot, g] for ref in stash], n_chunks=n_chunks, has_vlora=has_vlora)
                for g in range(groups)]

    def tail():
        return _scan_tail([s_ref.at[g] for g in range(groups)],
                          [[ref.at[read_slot, g] for ref in stash] for g in range(groups)],
                          [ins[-1].at[:, lanes[g]] for g in range(groups)],
                          [o_ref.at[:, lanes[g]] for g in range(groups)], n_chunks=n_chunks)

    def run(pending, tail_gen):
        active = []
        while pending or active or tail_gen is not None:
            if pending and all(done >= HEAD_STAGES_BEFORE_NEXT_GROUP for _, done in active):
                active.append([pending.pop(0), 0])
            for entry in list(active):
                try:
                    entry[1] += next(entry[0]) == HEAD_STAGE
                except StopIteration:
                    active.remove(entry)
            if tail_gen is not None and next(tail_gen, None) is None:
                tail_gen = None

    @pl.when(t == 0)
    def _():
        s_ref[...] = jnp.zeros_like(s_ref)
        run(heads(), None)

    @pl.when((t > 0) & (t < last))
    def _():
        run(heads(), tail())

    @pl.when(t == last)
    def _():
        run([], tail())


def _scan_masks():
    W = MXU_DIM
    head_mask = (lax.broadcasted_iota(jnp.int32, (STACK, W), 0) // CHUNK
                 == lax.broadcasted_iota(jnp.int32, (STACK, W), 1) // HEAD)
    seg_ones = (lax.broadcasted_iota(jnp.int32, (W, W), 0) // HEAD
                == lax.broadcasted_iota(jnp.int32, (W, W), 1) // HEAD).astype(BF16)
    return head_mask, seg_ones


def _scan_tail(s_refs, stashes, p_refs, o_refs, *, n_chunks):
    head_mask, seg_ones = _scan_masks()
    groups = range(len(s_refs))
    chunks = range(n_chunks)
    n_pairs = HEADS_PER_GROUP // 2
    states = [s_refs[g][...] for g in groups]
    ys = [[] for _ in groups]
    for c in chunks:
        rows = pl.ds(c * CHUNK, CHUNK)
        for g in groups:
            g_v, n_v, re_v, y0_v, dec_v = stashes[g][:5]
            sb = states[g].astype(BF16)
            s_big = jnp.where(head_mask, jnp.concatenate([sb] * n_pairs, axis=0), jnp.zeros((), BF16))
            ys[g].append(_dot_nt(re_v[rows, :], s_big) + y0_v[rows, :])
            states[g] = states[g] * dec_v[c, 0:1, :] + _dot(sb, g_v[c]) + n_v[c]
        yield TAIL_STAGE
    for g in groups:
        s_refs[g][...] = states[g]
    for g in groups:
        bonus_v, gate_v = stashes[g][5:]
        gn_g = p_refs[g][5:6, :]
        gn_b = p_refs[g][6:7, :]
        y = jnp.concatenate(ys[g], axis=0)
        mu = _dot(y.astype(BF16), seg_ones) * (1.0 / HEAD)
        d = y - mu
        var = _dot((d * d).astype(BF16), seg_ones) * (1.0 / HEAD)
        yield TAIL_STAGE
        yn = d * lax.rsqrt(var + GN_EPS) * gn_g + gn_b
        o_refs[g][...] = ((yn + bonus_v[...]) * gate_v[...]).astype(o_refs[g].dtype)
        yield TAIL_STAGE


def _scan_head(*refs, n_chunks, has_vlora):
    stash_refs = refs[-7:]
    if has_vlora:
        (r_ref, k_ref, v_ref, mw_ref, ma_ref, mg_ref, w2_ref, a2_ref, g2_ref,
         mv_ref, v2_ref, vf_ref, p_ref) = refs[:-7]
    else:
        r_ref, k_ref, v_ref, mw_ref, ma_ref, mg_ref, w2_ref, a2_ref, g2_ref, p_ref = refs[:-7]
    g_s, n_s, re_s, y0_s, dec_s, bonus_s, gate_s = stash_refs

    W = MXU_DIM
    head_of_row = lax.broadcasted_iota(jnp.int32, (STACK, W), 0) // CHUNK
    head_of_lane = lax.broadcasted_iota(jnp.int32, (STACK, W), 1) // HEAD
    head_mask = head_of_row == head_of_lane
    rr = lax.broadcasted_iota(jnp.int32, (STACK, STACK), 0)
    cc = lax.broadcasted_iota(jnp.int32, (STACK, STACK), 1)
    same_head = (rr // CHUNK) == (cc // CHUNK)
    strict_lower = same_head & ((rr % CHUNK) > (cc % CHUNK))
    eye = (rr == cc).astype(F32)
    half_mask = ((lax.broadcasted_iota(jnp.int32, (STACK, LANES), 0) // CHUNK) % 2
                 == lax.broadcasted_iota(jnp.int32, (STACK, LANES), 1) // HEAD)
    natural_lower = (lax.broadcasted_iota(jnp.int32, (CHUNK, 4 * CHUNK), 0)
                     >= lax.broadcasted_iota(jnp.int32, (CHUNK, 4 * CHUNK), 1) % CHUNK)
    lane_r = lax.broadcasted_iota(jnp.int32, (W, W), 0) // HEAD
    lane_c = lax.broadcasted_iota(jnp.int32, (W, W), 1) // HEAD
    seg_ones = (lane_r == lane_c).astype(BF16)
    bt = n_chunks * CHUNK
    tr = lax.broadcasted_iota(jnp.int32, (bt, bt), 0)
    tc = lax.broadcasted_iota(jnp.int32, (bt, bt), 1)
    tri = ((tr // CHUNK == tc // CHUNK) & (tr >= tc)).astype(BF16)

    w0 = p_ref[0:1, :]
    a0 = p_ref[1:2, :]
    k_k = p_ref[2:3, :]
    k_a = p_ref[3:4, :]
    r_k = p_ref[4:5, :]
    v0 = p_ref[7:8, :]

    nc = n_chunks
    chunks = range(nc)

    r = r_ref[...]
    k = k_ref[...]
    v = v_ref[...]
    z = w0 + _dot(mw_ref[...], w2_ref[...])
    w_log = -(jnp.maximum(-z, 0.0) + jnp.log(1.0 + jnp.exp(-jnp.abs(z)))) - 0.5
    log_decay = -jnp.exp(w_log)
    yield PREP_STAGE
    a = _sigmoid(a0 + _dot(ma_ref[...], a2_ref[...]))
    if has_vlora:
        v = v + (vf_ref[...] - v) * _sigmoid(v0 + _dot(mv_ref[...], v2_ref[...]))
    yield PREP_STAGE
    kk = k * k_k
    norm = jnp.sqrt(_dot((kk * kk).astype(BF16), seg_ones))
    kk = kk / jnp.maximum(norm, 1e-12)
    k = k * (1.0 + (a - 1.0) * k_a)
    yield PREP_STAGE

    cum = _split_dot_left(tri, log_decay)
    cum_last = jnp.concatenate(
        [jnp.broadcast_to(cum[(c + 1) * CHUNK - 1:(c + 1) * CHUNK, :], (CHUNK, W)) for c in chunks], axis=0)
    e_pos = jnp.exp(cum)
    e_neg = jnp.exp(-cum)
    yield PREP_STAGE
    e_prev = jnp.exp(cum - log_decay)
    e_tail = jnp.exp(cum_last - cum)
    kka = kk * a
    yield PREP_STAGE
    a_e = -kk * e_prev
    r_e = r * e_pos
    b_e = kka * e_neg
    yield PREP_STAGE
    k_e = k * e_neg
    b_end = kka * e_tail
    k_end = k * e_tail

    def rows_of(x, c):
        return x[c * CHUNK:(c + 1) * CHUNK]

    def stack(x, c):
        return jnp.where(head_mask, jnp.concatenate([rows_of(x, c)] * HEADS_PER_GROUP, axis=0), 0.0)

    def stack_pair(x, c):
        xc = rows_of(x, c)
        parts = [xc[:, (h // 2) * LANES:(h // 2 + 1) * LANES] for h in range(HEADS_PER_GROUP)]
        return jnp.where(half_mask, jnp.concatenate(parts, axis=0), 0.0).astype(BF16)

    n_pairs = HEADS_PER_GROUP // 2
    pair_rows = [slice(q * 2 * CHUNK, (q + 1) * 2 * CHUNK) for q in range(n_pairs)]
    pair_lanes = [slice(q * LANES, (q + 1) * LANES) for q in range(n_pairs)]

    yield HEAD_STAGE

    a_p = [stack_pair(a_e, c) for c in chunks]
    b_p = [stack_pair(b_e, c) for c in chunks]
    k_p = [stack_pair(k_e, c) for c in chunks]
    v_p = [stack_pair(v, c) for c in chunks]
    p_a = [_dot_nt(a_p[c], jnp.concatenate([b_p[c], k_p[c]], axis=0)) for c in chunks]
    l_ab = [jnp.where(strict_lower, p_a[c][:, :STACK], 0.0) for c in chunks]
    l_ak = [jnp.where(strict_lower, p_a[c][:, STACK:], 0.0).astype(BF16) for c in chunks]
    yield HEAD_STAGE

    l_r = [[jnp.where(natural_lower,
                      _dot_nt(rows_of(r_e, c)[:, pair_lanes[q]].astype(BF16),
                              jnp.concatenate([b_p[c][pair_rows[q]], k_p[c][pair_rows[q]]], axis=0)),
                      0.0).astype(BF16) for q in range(n_pairs)] for c in chunks]
    yield HEAD_STAGE

    inv = [eye + l_ab[c] for c in chunks]
    power = [l_ab[c].astype(BF16) for c in chunks]
    for _ in range(int(math.log2(CHUNK)) - 1):
        power = [_dot(power[c], power[c]).astype(BF16) for c in chunks]
        yield HEAD_STAGE
        inv = [inv[c] + _dot(inv[c].astype(BF16), power[c]) for c in chunks]
        yield HEAD_STAGE

    x = [jnp.concatenate([a_p[c], _dot(l_ak[c], v_p[c]).astype(BF16)], axis=1) for c in chunks]
    yield HEAD_STAGE
    wu = [_dot(inv[c].astype(BF16), x[c]).astype(BF16) for c in chunks]
    yield HEAD_STAGE

    zero_pad = jnp.zeros((2 * CHUNK, LANES), BF16)
    for c in chunks:
        gn = _dot_tn(wu[c], stack(b_end, c).astype(BF16))
        n_s[c] = gn[LANES:] + _dot_tn(v_p[c], stack(k_end, c).astype(BF16))
        g_s[c] = jnp.where(head_mask, jnp.concatenate([gn[:LANES]] * n_pairs, axis=0), 0.0).astype(BF16)
        dec_s[c] = jnp.broadcast_to(jnp.exp(cum[(c + 1) * CHUNK - 1:(c + 1) * CHUNK, :]), (SUBLANES, W))
        yield HEAD_STAGE
        re_q, y0_q = [], []
        for q in range(n_pairs):
            rhs = jnp.concatenate([wu[c][pair_rows[q]],
                                   jnp.concatenate([zero_pad, v_p[c][pair_rows[q]]], axis=1)], axis=0)
            ry = _dot(l_r[c][q], rhs)
            re_q.append(rows_of(r_e, c)[:, pair_lanes[q]] + ry[:, :LANES])
            y0_q.append(ry[:, LANES:])
        re_s[pl.ds(c * CHUNK, CHUNK), :] = jnp.concatenate(re_q, axis=1).astype(BF16)
        y0_s[pl.ds(c * CHUNK, CHUNK), :] = jnp.concatenate(y0_q, axis=1)
        yield HEAD_STAGE
    bonus_s[...] = _dot((r * k * r_k).astype(BF16), seg_ones) * v
    gate_s[...] = _dot(mg_ref[...], g2_ref[...])
    yield HEAD_STAGE


def _split_dot_left(e, x):
    hi = x.astype(BF16)
    lo = (x - hi.astype(F32)).astype(BF16)
    return _dot(e, hi) + _dot(e, lo)


def rwkv_scan(rkv, mids, ups, v_mix, chan, seq, *, bt=256, groups=4):
    _, T, D = rkv.shape
    groups = min(groups, D // MXU_DIM)
    W = groups * MXU_DIM
    bt = min(bt, seq)
    assert D % W == 0 and seq % bt == 0 and bt % CHUNK == 0
    has_vlora = v_mix is not None
    nb = seq // bt

    def row_in(b, t):
        return b * nb + jnp.minimum(t, nb - 1)

    def stacked(p):
        return pl.BlockSpec((None, bt, W), lambda b, gidx, t: (p, row_in(b, t), gidx))

    def mid(a):
        return pl.BlockSpec((bt, a.shape[1]), lambda b, gidx, t: (row_in(b, t), 0))

    def up(a):
        return pl.BlockSpec((a.shape[0], W), lambda b, gidx, t: (0, gidx))

    ins = [rkv, rkv, rkv, *mids, *ups]
    specs = [stacked(0), stacked(1), stacked(2)] + [mid(a) for a in mids] + [up(a) for a in ups]
    if has_vlora:
        mv, v2, first_rkv = v_mix
        ins += [mv, v2, first_rkv]
        specs += [mid(mv), up(v2), stacked(2)]
    ins.append(chan)
    specs.append(pl.BlockSpec((SUBLANES, W), lambda b, gidx, t: (0, gidx)))
    nc = bt // CHUNK
    G = MXU_DIM
    stash = [pltpu.VMEM((2, groups, nc, G, G), BF16),
             pltpu.VMEM((2, groups, nc, 2 * HEAD, G), F32),
             pltpu.VMEM((2, groups, bt, G), BF16),
             pltpu.VMEM((2, groups, bt, G), F32),
             pltpu.VMEM((2, groups, nc, SUBLANES, G), F32),
             pltpu.VMEM((2, groups, bt, G), F32),
             pltpu.VMEM((2, groups, bt, G), F32)]
    return pl.pallas_call(
        functools.partial(_scan_body, n_chunks=nc, has_vlora=has_vlora, groups=groups),
        grid=(T // seq, D // W, nb + 1),
        in_specs=specs,
        out_specs=pl.BlockSpec((bt, W), lambda b, gidx, t: (b * nb + jnp.maximum(t - 1, 0), gidx)),
        out_shape=jax.ShapeDtypeStruct((T, D), BF16),
        scratch_shapes=[pltpu.VMEM((groups, 2 * HEAD, G), F32)] + stash,
        compiler_params=_params("parallel", "parallel", "arbitrary"),
        name="rwkv_scan",
    )(*ins)


def _bias_body(bmap_ref, rb_ref, o_ref):
    h = pl.program_id(0)
    bmap = bmap_ref[...]
    acc = jnp.full(bmap.shape, -jnp.inf, F32)
    for b in range(REL_BUCKETS):
        acc = jnp.where(bmap == b, rb_ref[b, h], acc)
    o_ref[0, 0] = acc
    kj = lax.broadcasted_iota(jnp.int32, bmap.shape, 1)
    o_ref[1, 0] = jnp.where(kj >= BLOCK, acc, -jnp.inf)


def bias_table(rel_bias, bmap):
    H = rel_bias.shape[1]
    return pl.pallas_call(
        _bias_body,
        grid=(H,),
        in_specs=[pl.BlockSpec((BLOCK, 2 * BLOCK), lambda h: (0, 0)),
                  pl.BlockSpec(memory_space=pltpu.SMEM)],
        out_specs=pl.BlockSpec((2, 1, BLOCK, 2 * BLOCK), lambda h: (0, h, 0, 0)),
        out_shape=jax.ShapeDtypeStruct((2, H, BLOCK, 2 * BLOCK), F32),
        compiler_params=_params("parallel"),
        name="rel_bias_table",
    )(bmap, rel_bias)


def _attn_body(q_ref, kp_ref, kc_ref, vp_ref, vc_ref, bias_ref, sink_ref, o_ref, *, group, n_kv):
    low = lax.broadcasted_iota(jnp.int32, (BLOCK, LANES), 1) < HEAD
    scale = HEAD ** -0.5

    for kv_pair in range(n_kv // 2):
        lanes = pl.ds(kv_pair * LANES, LANES)
        kband = jnp.concatenate([kp_ref[:, lanes], kc_ref[:, lanes]], axis=0)
        vband = jnp.concatenate([vp_ref[:, lanes], vc_ref[:, lanes]], axis=0)
        kswap = pltpu.roll(kband, HEAD, 1)
        vswap = pltpu.roll(vband, HEAD, 1)
        for e in range(2):
            kvh = 2 * kv_pair + e
            k_at = (kband, kswap) if e == 0 else (kswap, kband)
            v_at = (vband, vswap) if e == 0 else (vswap, vband)
            for pair in range(group // 2):
                q_lanes = pl.ds((kvh * group // 2 + pair) * LANES, LANES)
                qp = q_ref[:, q_lanes] * scale
                halves = []
                for half in range(2):
                    h = kvh * group + 2 * pair + half
                    in_half = low if half == 0 else jnp.logical_not(low)
                    qm = jnp.where(in_half, qp, jnp.zeros_like(qp))
                    s = _dot_nt(qm, k_at[half]) + bias_ref[0, h]
                    sink = sink_ref[h]
                    m = jnp.maximum(jnp.max(s, axis=-1, keepdims=True), sink)
                    p = jnp.exp(s - m)
                    denom = jnp.sum(p, axis=-1, keepdims=True) + jnp.exp(sink - m)
                    halves.append(_dot(p.astype(BF16), v_at[half]) / denom)
                o_ref[:, q_lanes] = jnp.where(low, halves[0], halves[1]).astype(o_ref.dtype)


def swa_attention(qkv, bias, sinks, seq, n_q_heads, n_kv_heads):
    T = qkv.shape[0]
    group = n_q_heads // n_kv_heads
    assert group % 2 == 0 and n_kv_heads % 2 == 0 and seq % BLOCK == 0
    nb = seq // BLOCK
    qw, kw = n_q_heads * HEAD, n_kv_heads * HEAD
    assert qw % kw == 0 and kw % LANES == 0
    k_col, v_col = qw // kw, qw // kw + 1

    def cur(col):
        return pl.BlockSpec((BLOCK, kw), lambda b, n: (b * nb + n, col))

    def prev(col):
        return pl.BlockSpec((BLOCK, kw), lambda b, n: (b * nb + jnp.maximum(n - 1, 0), col))

    return pl.pallas_call(
        functools.partial(_attn_body, group=group, n_kv=n_kv_heads),
        grid=(T // seq, nb),
        in_specs=[pl.BlockSpec((BLOCK, qw), lambda b, n: (b * nb + n, 0)),
                  prev(k_col), cur(k_col), prev(v_col), cur(v_col),
                  pl.BlockSpec((1, n_q_heads, BLOCK, 2 * BLOCK), lambda b, n: (jnp.where(n == 0, 1, 0), 0, 0, 0)),
                  pl.BlockSpec(memory_space=pltpu.SMEM)],
        out_specs=pl.BlockSpec((BLOCK, qw), lambda b, n: (b * nb + n, 0)),
        out_shape=jax.ShapeDtypeStruct((T, qw), BF16),
        compiler_params=_params("parallel", "parallel"),
        name="swa_attention",
    )(qkv, qkv, qkv, qkv, qkv, bias, sinks)


def _bucket_map():
    qi = jnp.arange(BLOCK)[:, None]
    kj = jnp.arange(2 * BLOCK)[None, :]
    signed = qi + BLOCK - kj
    dist = jnp.maximum(signed, 0)
    max_exact = REL_BUCKETS // 2
    d_f = jnp.maximum(dist, max_exact).astype(F32)
    large = max_exact + (jnp.log(d_f / max_exact) / math.log(REL_MAX_DIST / max_exact)
                         * (REL_BUCKETS - max_exact)).astype(jnp.int32)
    large = jnp.minimum(large, REL_BUCKETS - 1)
    bucket = jnp.where(dist < max_exact, dist, large)
    return jnp.where((signed >= 0) & (signed < WINDOW), bucket, -1).astype(jnp.int32)


def _ffn_up_body(x_ref, *refs, blocks_per_seq, sub, rsub, n_sub):
    wg_refs, wu_refs = refs[:n_sub], refs[n_sub:2 * n_sub]
    cg_ref, cu_ref, bg_ref, bu_ref, o_ref, tail_g_ref, tail_u_ref, raw_g_ref, raw_u_ref = refs[2 * n_sub:]
    first = pl.program_id(0) % blocks_per_seq == 0
    j = pl.program_id(1)
    bm = x_ref.shape[0]
    H = SUBLANES

    @pl.when((pl.program_id(0) == 0) & (j == 0))
    def _():
        tail_g_ref[...] = jnp.zeros_like(tail_g_ref)
        tail_u_ref[...] = jnp.zeros_like(tail_u_ref)

    def matmul_tile(slot, s, rb, halo_g, halo_u):
        xr = x_ref[pl.ds(rb * rsub, rsub), :]
        raw_g_ref[slot, 0:H, :] = halo_g
        raw_u_ref[slot, 0:H, :] = halo_u
        raw_g_ref[slot, H:, :] = _dot(xr, wg_refs[s][...])
        raw_u_ref[slot, H:, :] = _dot(xr, wu_refs[s][...])
        return raw_g_ref[slot, rsub:, :], raw_u_ref[slot, rsub:, :]

    def conv(raw_ref, slot, c_ref, b_ref, cols):
        h = raw_ref[slot, H:, :]
        h1 = raw_ref[slot, H - 1:H - 1 + rsub, :]
        h2 = raw_ref[slot, H - 2:H - 2 + rsub, :]
        return c_ref[0:1, cols] * h2 + c_ref[1:2, cols] * h1 + c_ref[2:3, cols] * h + b_ref[:, cols]

    def finish_tile(slot, s, rb):
        cols = pl.ds(s * sub, sub)
        gate = conv(raw_g_ref, slot, cg_ref, bg_ref, cols)
        up = conv(raw_u_ref, slot, cu_ref, bu_ref, cols)
        o_ref[pl.ds(rb * rsub, rsub), cols] = (gate * _sigmoid(gate) * up).astype(o_ref.dtype)

    n_rows = bm // rsub
    prev = None
    t = 0
    for s in range(n_sub):
        cols = pl.ds(s * sub, sub)
        halo_g = jnp.where(first, 0.0, tail_g_ref[j, :, cols])
        halo_u = jnp.where(first, 0.0, tail_u_ref[j, :, cols])
        for rb in range(n_rows):
            halo_g, halo_u = matmul_tile(t % 2, s, rb, halo_g, halo_u)
            if prev is not None:
                finish_tile(*prev)
            prev = (t % 2, s, rb)
            t += 1
        tail_g_ref[j, :, cols] = halo_g
        tail_u_ref[j, :, cols] = halo_u
    finish_tile(*prev)


def ffn_up(x, w2, layer, cg, cu, bg, bu, seq, *, bm=1024, bn=512, sub=256, rsub=256):
    T, D = x.shape
    F = w2.shape[2] // 2
    bm, bn = min(bm, seq), min(bn, F)
    sub, rsub = min(sub, bn), min(rsub, bm)
    assert seq % bm == 0 and bn % sub == 0 and F % sub == 0 and bm % rsub == 0 and rsub % SUBLANES == 0
    nf = pl.cdiv(F, bn)
    n_sub = bn // sub
    last = 2 * F // sub - 1

    def wspec(s, half):
        return pl.BlockSpec((None, D, sub),
                            lambda i, j: (layer, 0, jnp.minimum(half * (F // sub) + j * n_sub + s, last)))

    col = lambda rows: pl.BlockSpec((rows, bn), lambda i, j: (0, j))
    return pl.pallas_call(
        functools.partial(_ffn_up_body, blocks_per_seq=seq // bm, sub=sub, rsub=rsub, n_sub=n_sub),
        grid=(T // bm, nf),
        in_specs=[pl.BlockSpec((bm, D), lambda i, j: (i, 0))]
                 + [wspec(s, 0) for s in range(n_sub)] + [wspec(s, 1) for s in range(n_sub)]
                 + [col(CONV_W), col(CONV_W), col(1), col(1)],
        out_specs=pl.BlockSpec((bm, bn), lambda i, j: (i, j)),
        out_shape=jax.ShapeDtypeStruct((T, F), BF16),
        scratch_shapes=[pltpu.VMEM((nf, SUBLANES, bn), F32), pltpu.VMEM((nf, SUBLANES, bn), F32),
                        pltpu.VMEM((2, SUBLANES + rsub, sub), F32), pltpu.VMEM((2, SUBLANES + rsub, sub), F32)],
        compiler_params=_params("arbitrary", "arbitrary"),
        name="ffn_up",
    )(x, *([w2] * (2 * n_sub)), cg, cu, bg, bu)


def _pad_to(x, axis, mult):
    pad = (-x.shape[axis]) % mult
    if pad == 0:
        return x
    widths = [(0, 0)] * x.ndim
    widths[axis] = (0, pad)
    return jnp.pad(x, widths)


WIDE_COL_BLOCK = 1024
SCAN_TIME_BLOCK = 4 * CHUNK


def _lora_down(xs, lead, w_down, w_up, act, name):
    w_down = _pad_to(w_down, 1, LANES).astype(BF16)
    w_up = _pad_to(w_up, 0, LANES).astype(BF16)
    return matmul(xs, w_down, BF16, act=act, lead=lead, name=name + "_down"), w_up


def rwkv_layer(x, seq, first_rkv, layer, mix, w_rkv, w_o, w0, w1, w2, a0, a1, a2, g1, g2,
               k_k, k_a, r_k, gn_g, gn_b, v_lora):
    xs = token_shift_mix(x, mix[jnp.array([0, 2, 3, 1, 4, 5])], seq)
    rkv = batched_matmul(xs, w_rkv, F32, nbatch=3, layer=layer, bn=WIDE_COL_BLOCK, name="rwkv_rkv")
    mw, w2 = _lora_down(xs, 3, w1, w2, "tanh", "rwkv_w")
    ma, a2 = _lora_down(xs, 4, a1, a2, None, "rwkv_a")
    mg, g2 = _lora_down(xs, 5, g1, g2, "sigmoid", "rwkv_g")
    if v_lora is None:
        v_mix = None
        v0 = jnp.zeros_like(w0)
    else:
        v0, v1, v2 = v_lora
        mv, v2 = _lora_down(xs, 2, v1, v2, None, "rwkv_v")
        v_mix = (mv, v2, first_rkv)
    chan = jnp.stack([w0, a0, k_k, k_a, r_k, gn_g, gn_b, v0])
    y = rwkv_scan(rkv, (mw, ma, mg), (w2, a2, g2), v_mix, chan, seq, bt=SCAN_TIME_BLOCK)
    return matmul(y, w_o, BF16, layer=layer, bn=WIDE_COL_BLOCK, name="rwkv_out"), rkv


def attn_layer(xb, seq, layer, w_qkv, w_o, sinks, bias):
    n_q = w_o.shape[1] // HEAD
    n_kv = (w_qkv.shape[2] // HEAD - n_q) // 2
    qkv = matmul(xb, w_qkv, BF16, layer=layer, bn=WIDE_COL_BLOCK, name="attn_qkv")
    o = swa_attention(qkv, bias, sinks, seq, n_q, n_kv)
    return matmul(o, w_o, BF16, layer=layer, bn=WIDE_COL_BLOCK, name="attn_out")


FFN_COL_BLOCK = 512
FFN_ROW_BLOCK = 1024
FFN_COL_SUB = MXU_DIM


def ffn_layer(xb, seq, layer, w_up, conv_w, conv_b, w_down):
    f = w_down.shape[1]
    conv_b = conv_b.reshape(1, -1)
    act = ffn_up(xb, w_up, layer, conv_w[:, :f], conv_w[:, f:], conv_b[:, :f], conv_b[:, f:],
                 seq, bm=FFN_ROW_BLOCK, bn=FFN_COL_BLOCK, sub=FFN_COL_SUB)
    return matmul(act, w_down, BF16, layer=layer, bm=512, bn=512, name="ffn_down")


def kernel(x, rwkv_mix, rwkv_w_rkv, rwkv_w_o, rwkv_w0, rwkv_w1, rwkv_w2, rwkv_a0, rwkv_a1, rwkv_a2, rwkv_g1, rwkv_g2, rwkv_k_k, rwkv_k_a, rwkv_r_k, rwkv_gn_g, rwkv_gn_b, rwkv_v0, rwkv_v1, rwkv_v2, attn_w_qkv, attn_w_o, attn_sinks, rel_bias, ffn_w_up, ffn_conv_w, ffn_conv_b, ffn_w_down, ln1_g, ln1_b, ln2_g, ln2_b):
    B, S, D = x.shape
    depth = ln1_g.shape[0]
    xf = x.reshape(B * S, D)
    xb = xf.astype(BF16)
    bias = bias_table(rel_bias, _bucket_map())
    w_rkv, w_ro = rwkv_w_rkv.astype(BF16), rwkv_w_o.astype(BF16)
    w_qkv, w_ao = attn_w_qkv.astype(BF16), attn_w_o.astype(BF16)
    w_up, w_down = ffn_w_up.astype(BF16), ffn_w_down.astype(BF16)
    first_rkv = None
    for i in range(depth):
        j = i // 2
        if i % 2 == 0:
            v_lora = None if j == 0 else (rwkv_v0[j - 1], rwkv_v1[j - 1], rwkv_v2[j - 1])
            h, rkv = rwkv_layer(xf, S, first_rkv, j, rwkv_mix[j], w_rkv, w_ro,
                                rwkv_w0[j], rwkv_w1[j], rwkv_w2[j], rwkv_a0[j],
                                rwkv_a1[j], rwkv_a2[j], rwkv_g1[j], rwkv_g2[j], rwkv_k_k[j],
                                rwkv_k_a[j], rwkv_r_k[j], rwkv_gn_g[j], rwkv_gn_b[j], v_lora)
            if v_lora is None:
                first_rkv = rkv
        else:
            h = attn_layer(xb, S, j, w_qkv, w_ao, attn_sinks[j], bias)
        xf, xb = ln_residual(xf, h, ln1_g[i], ln1_b[i])
        f = ffn_layer(xb, S, i, w_up, ffn_conv_w[i], ffn_conv_b[i], w_down)
        xf, xb = ln_residual(xf, f, ln2_g[i], ln2_b[i])
    return xf.reshape(B, S, D)
```

```python
import functools
import math

import jax
import jax.numpy as jnp
from jax import lax
from jax.experimental import pallas as pl
from jax.experimental.pallas import tpu as pltpu

F32 = jnp.float32
BF16 = jnp.bfloat16

LANES = 128
SUBLANES = 8
MXU_DIM = 256
VMEM_LIMIT_BYTES = 56 * 1024 * 1024

HEAD = 64
HEADS_PER_GROUP = MXU_DIM // HEAD
CHUNK = 64
STACK = HEADS_PER_GROUP * CHUNK

WINDOW = 128
BLOCK = 128
REL_BUCKETS = 32
REL_MAX_DIST = 128
CONV_W = 3
LN_EPS = 1e-5
GN_EPS = HEAD * 1e-5
DEPTH = 4
DEEPNORM_ALPHA = (2 * DEPTH) ** 0.25


def _params(*sem):
    return pltpu.CompilerParams(dimension_semantics=sem, vmem_limit_bytes=VMEM_LIMIT_BYTES)


def _block(n, target, unit):
    if n <= target:
        return n
    b = target - target % unit
    while n % b:
        b -= unit
    return b


def _sigmoid(x):
    return 1.0 / (1.0 + jnp.exp(-x))


def _dot(a, b):
    return jnp.dot(a, b, preferred_element_type=F32)


def _dot_nt(a, b):
    return lax.dot_general(a, b, (((1,), (1,)), ((), ())), preferred_element_type=F32)


def _dot_tn(a, b):
    return lax.dot_general(a, b, (((0,), (0,)), ((), ())), preferred_element_type=F32)


def _mm_body(x_ref, w_ref, o_ref):
    o_ref[...] = _dot(x_ref[...], w_ref[...]).astype(o_ref.dtype)


def matmul(x, w, out_dtype, *, layer, bm=1024, bn=512, name="matmul"):
    M, K = x.shape
    N = w.shape[-1]
    bm, bn = _block(M, bm, SUBLANES), _block(N, bn, LANES)
    assert M % bm == 0 and N % bn == 0
    return pl.pallas_call(
        _mm_body,
        grid=(M // bm, N // bn),
        in_specs=[pl.BlockSpec((bm, K), lambda i, j: (i, 0)),
                  pl.BlockSpec((None, K, bn), lambda i, j: (layer, 0, j))],
        out_specs=pl.BlockSpec((bm, bn), lambda i, j: (i, j)),
        out_shape=jax.ShapeDtypeStruct((M, N), out_dtype),
        compiler_params=_params("parallel", "parallel"),
        name=name,
    )(x, w)


def batched_matmul(x, w, out_dtype, *, nbatch, layer, bm=1024, bn=512, name="bmm"):
    M, K = x.shape[-2:]
    N = w.shape[-1]
    bm, bn = _block(M, bm, SUBLANES), _block(N, bn, LANES)
    assert M % bm == 0 and N % bn == 0
    return pl.pallas_call(
        _mm_body,
        grid=(nbatch, M // bm, N // bn),
        in_specs=[pl.BlockSpec((None, bm, K), lambda p, i, j: (p, i, 0)),
                  pl.BlockSpec((None, None, K, bn), lambda p, i, j: (layer, p, 0, j))],
        out_specs=pl.BlockSpec((None, bm, bn), lambda p, i, j: (p, i, j)),
        out_shape=jax.ShapeDtypeStruct((nbatch, M, N), out_dtype),
        compiler_params=_params("parallel", "parallel", "parallel"),
        name=name,
    )(x, w)


def _ln_body(x_ref, h_ref, g_ref, b_ref, of_ref, ob_ref):
    y = DEEPNORM_ALPHA * x_ref[...] + h_ref[...].astype(F32)
    mu = jnp.mean(y, axis=-1, keepdims=True)
    d = y - mu
    var = jnp.mean(d * d, axis=-1, keepdims=True)
    out = d * lax.rsqrt(var + LN_EPS) * g_ref[...] + b_ref[...]
    of_ref[...] = out
    ob_ref[...] = out.astype(BF16)


def ln_residual(x, h, g, b, *, br=256):
    T, D = x.shape
    br = min(br, T)
    assert T % br == 0
    row = pl.BlockSpec((br, D), lambda i: (i, 0))
    vec = pl.BlockSpec((1, D), lambda i: (0, 0))
    return pl.pallas_call(
        _ln_body,
        grid=(T // br,),
        in_specs=[row, row, vec, vec],
        out_specs=[row, row],
        out_shape=[jax.ShapeDtypeStruct((T, D), F32), jax.ShapeDtypeStruct((T, D), BF16)],
        compiler_params=_params("parallel"),
        name="ln_residual",
    )(x, h, g.reshape(1, D), b.reshape(1, D))


MIX_R, MIX_W, MIX_K, MIX_V, MIX_A, MIX_G = range(6)


def _mix_body(x_ref, xp_ref, mix_ref, *refs, blocks_per_seq, lora_rows, lora_acts):
    n = len(lora_rows)
    w_refs, o_ref, mid_refs, acc_refs = refs[:n], refs[n], refs[n + 1:2 * n + 1], refs[2 * n + 1:]
    i, j = pl.program_id(0), pl.program_id(1)
    x = x_ref[...]
    prev_last = xp_ref[SUBLANES - 1:SUBLANES, :]
    prev_last = jnp.where(i % blocks_per_seq == 0, 0.0, prev_last)
    row = lax.broadcasted_iota(jnp.int32, x.shape, 0)
    shifted = jnp.where(row == 0, prev_last, pltpu.roll(x, 1, 0))
    xx = shifted - x

    def mixed(r):
        return (x + xx * mix_ref[r:r + 1, :]).astype(BF16)

    for p, r in enumerate((MIX_R, MIX_K, MIX_V)):
        o_ref[p] = mixed(r)

    @pl.when(j == 0)
    def _():
        for acc in acc_refs:
            acc[...] = jnp.zeros_like(acc)

    for q in range(n):
        acc_refs[q][...] += _dot(mixed(lora_rows[q]), w_refs[q][...])

    @pl.when(j == pl.num_programs(1) - 1)
    def _():
        for q in range(n):
            acc = acc_refs[q][...]
            if lora_acts[q] == "tanh":
                acc = jnp.tanh(acc)
            elif lora_acts[q] == "sigmoid":
                acc = _sigmoid(acc)
            mid_refs[q][...] = acc.astype(BF16)


def token_shift_mix(x, mix, loras, seq, *, bt=512, bd=1024):
    T, D = x.shape
    bt, bd = min(bt, seq), min(bd, D)
    assert seq % bt == 0 and D % bd == 0 and bt % SUBLANES == 0
    halo = bt // SUBLANES
    widths = [w.shape[1] for _, w, _ in loras]
    outs = pl.pallas_call(
        functools.partial(_mix_body, blocks_per_seq=seq // bt, lora_rows=tuple(r for r, _, _ in loras),
                          lora_acts=tuple(a for _, _, a in loras)),
        grid=(T // bt, D // bd),
        in_specs=[pl.BlockSpec((bt, bd), lambda i, j: (i, j)),
                  pl.BlockSpec((SUBLANES, bd), lambda i, j: (jnp.maximum(i * halo - 1, 0), j)),
                  pl.BlockSpec((6, bd), lambda i, j: (0, j))]
                 + [pl.BlockSpec((bd, n), lambda i, j: (j, 0)) for n in widths],
        out_specs=[pl.BlockSpec((3, bt, bd), lambda i, j: (0, i, j))]
                  + [pl.BlockSpec((bt, n), lambda i, j: (i, 0)) for n in widths],
        out_shape=[jax.ShapeDtypeStruct((3, T, D), BF16)] + [jax.ShapeDtypeStruct((T, n), BF16) for n in widths],
        scratch_shapes=[pltpu.VMEM((bt, n), F32) for n in widths],
        compiler_params=_params("parallel", "arbitrary"),
        name="rwkv_mix",
    )(x, x, mix, *[w for _, w, _ in loras])
    return outs[0], outs[1:]


PREP_STAGE, HEAD_STAGE, TAIL_STAGE = "prep", "head", "tail"
HEAD_STAGES_BEFORE_NEXT_GROUP = 11


def _scan_body(*refs, n_chunks, has_vlora, groups):
    n_in = len(refs) - 9
    ins, o_ref = refs[:n_in], refs[n_in]
    s_ref, stash = refs[n_in + 1], refs[n_in + 2:]
    t = pl.program_id(2)
    last = pl.num_programs(2) - 1
    write_slot = t % 2
    read_slot = 1 - write_slot
    shared = (3, 4, 5, 9) if has_vlora else (3, 4, 5)
    lanes = [pl.ds(g * MXU_DIM, MXU_DIM) for g in range(groups)]

    def heads():
        return [_scan_head(*[ref if i in shared else ref.at[:, lanes[g]] for i, ref in enumerate(ins)],
                           *[ref.at[write_slot, g] for ref in stash], n_chunks=n_chunks, has_vlora=has_vlora)
                for g in range(groups)]

    def tail():
        return _scan_tail([s_ref.at[g] for g in range(groups)],
                          [[ref.at[read_slot, g] for ref in stash] for g in range(groups)],
                          [ins[-1].at[:, lanes[g]] for g in range(groups)],
                          [o_ref.at[:, lanes[g]] for g in range(groups)], n_chunks=n_chunks)

    def run(pending, tail_gen):
        active = []
        while pending or active or tail_gen is not None:
            if pending and all(done >= HEAD_STAGES_BEFORE_NEXT_GROUP for _, done in active):
                active.append([pending.pop(0), 0])
            for entry in list(active):
                try:
                    entry[1] += next(entry[0]) == HEAD_STAGE
                except StopIteration:
                    active.remove(entry)
            if tail_gen is not None and next(tail_gen, None) is None:
                tail_gen = None

    @pl.when(t == 0)
    def _():
        s_ref[...] = jnp.zeros_like(s_ref)
        run(heads(), None)

    @pl.when((t > 0) & (t < last))
    def _():
        run(heads(), tail())

    @pl.when(t == last)
    def _():
        run([], tail())


def _scan_masks():
    W = MXU_DIM
    head_mask = (lax.broadcasted_iota(jnp.int32, (STACK, W), 0) // CHUNK
                 == lax.broadcasted_iota(jnp.int32, (STACK, W), 1) // HEAD)
    seg_ones = (lax.broadcasted_iota(jnp.int32, (W, W), 0) // HEAD
                == lax.broadcasted_iota(jnp.int32, (W, W), 1) // HEAD).astype(BF16)
    return head_mask, seg_ones


def _scan_tail(s_refs, stashes, p_refs, o_refs, *, n_chunks):
    head_mask, seg_ones = _scan_masks()
    groups = range(len(s_refs))
    chunks = range(n_chunks)
    n_pairs = HEADS_PER_GROUP // 2
    states = [s_refs[g][...] for g in groups]
    ys = [[] for _ in groups]
    for c in chunks:
        rows = pl.ds(c * CHUNK, CHUNK)
        for g in groups:
            g_v, n_v, re_v, y0_v, dec_v = stashes[g][:5]
            sb = states[g].astype(BF16)
            s_big = jnp.where(head_mask, jnp.concatenate([sb] * n_pairs, axis=0), jnp.zeros((), BF16))
            ys[g].append(_dot_nt(re_v[rows, :], s_big) + y0_v[rows, :])
            states[g] = states[g] * dec_v[c, 0:1, :] + _dot(sb, g_v[c]) + n_v[c]
        yield TAIL_STAGE
    for g in groups:
        s_refs[g][...] = states[g]
    for g in groups:
        bonus_v, gate_v = stashes[g][5:]
        gn_g = p_refs[g][5:6, :]
        gn_b = p_refs[g][6:7, :]
        y = jnp.concatenate(ys[g], axis=0)
        mu = _dot(y.astype(BF16), seg_ones) * (1.0 / HEAD)
        d = y - mu
        var = _dot((d * d).astype(BF16), seg_ones) * (1.0 / HEAD)
        yield TAIL_STAGE
        yn = d * lax.rsqrt(var + GN_EPS) * gn_g + gn_b
        o_refs[g][...] = ((yn + bonus_v[...]) * gate_v[...]).astype(o_refs[g].dtype)
        yield TAIL_STAGE


def _scan_head(*refs, n_chunks, has_vlora):
    stash_refs = refs[-7:]
    if has_vlora:
        (r_ref, k_ref, v_ref, mw_ref, ma_ref, mg_ref, w2_ref, a2_ref, g2_ref,
         mv_ref, v2_ref, vf_ref, p_ref) = refs[:-7]
    else:
        r_ref, k_ref, v_ref, mw_ref, ma_ref, mg_ref, w2_ref, a2_ref, g2_ref, p_ref = refs[:-7]
    g_s, n_s, re_s, y0_s, dec_s, bonus_s, gate_s = stash_refs

    W = MXU_DIM
    head_of_row = lax.broadcasted_iota(jnp.int32, (STACK, W), 0) // CHUNK
    head_of_lane = lax.broadcasted_iota(jnp.int32, (STACK, W), 1) // HEAD
    head_mask = head_of_row == head_of_lane
    rr = lax.broadcasted_iota(jnp.int32, (STACK, STACK), 0)
    cc = lax.broadcasted_iota(jnp.int32, (STACK, STACK), 1)
    same_head = (rr // CHUNK) == (cc // CHUNK)
    strict_lower = same_head & ((rr % CHUNK) > (cc % CHUNK))
    eye = (rr == cc).astype(F32)
    half_mask = ((lax.broadcasted_iota(jnp.int32, (STACK, LANES), 0) // CHUNK) % 2
                 == lax.broadcasted_iota(jnp.int32, (STACK, LANES), 1) // HEAD)
    natural_lower = (lax.broadcasted_iota(jnp.int32, (CHUNK, 4 * CHUNK), 0)
                     >= lax.broadcasted_iota(jnp.int32, (CHUNK, 4 * CHUNK), 1) % CHUNK)
    lane_r = lax.broadcasted_iota(jnp.int32, (W, W), 0) // HEAD
    lane_c = lax.broadcasted_iota(jnp.int32, (W, W), 1) // HEAD
    seg_ones = (lane_r == lane_c).astype(BF16)
    bt = n_chunks * CHUNK
    tr = lax.broadcasted_iota(jnp.int32, (bt, bt), 0)
    tc = lax.broadcasted_iota(jnp.int32, (bt, bt), 1)
    tri = ((tr // CHUNK == tc // CHUNK) & (tr >= tc)).astype(BF16)

    w0 = p_ref[0:1, :]
    a0 = p_ref[1:2, :]
    k_k = p_ref[2:3, :]
    k_a = p_ref[3:4, :]
    r_k = p_ref[4:5, :]
    v0 = p_ref[7:8, :]

    nc = n_chunks
    chunks = range(nc)

    r = r_ref[...]
    k = k_ref[...]
    v = v_ref[...]
    z = w0 + _dot(mw_ref[...], w2_ref[...])
    w_log = -(jnp.maximum(-z, 0.0) + jnp.log(1.0 + jnp.exp(-jnp.abs(z)))) - 0.5
    log_decay = -jnp.exp(w_log)
    yield PREP_STAGE
    a = _sigmoid(a0 + _dot(ma_ref[...], a2_ref[...]))
    if has_vlora:
        v = v + (vf_ref[...] - v) * _sigmoid(v0 + _dot(mv_ref[...], v2_ref[...]))
    yield PREP_STAGE
    kk = k * k_k
    norm = jnp.sqrt(_dot((kk * kk).astype(BF16), seg_ones))
    kk = kk / jnp.maximum(norm, 1e-12)
    k = k * (1.0 + (a - 1.0) * k_a)
    yield PREP_STAGE

    cum = _split_dot_left(tri, log_decay)
    cum_last = jnp.concatenate(
        [jnp.broadcast_to(cum[(c + 1) * CHUNK - 1:(c + 1) * CHUNK, :], (CHUNK, W)) for c in chunks], axis=0)
    e_pos = jnp.exp(cum)
    e_neg = jnp.exp(-cum)
    yield PREP_STAGE
    e_prev = jnp.exp(cum - log_decay)
    e_tail = jnp.exp(cum_last - cum)
    kka = kk * a
    yield PREP_STAGE
    a_e = -kk * e_prev
    r_e = r * e_pos
    b_e = kka * e_neg
    yield PREP_STAGE
    k_e = k * e_neg
    b_end = kka * e_tail
    k_end = k * e_tail

    def rows_of(x, c):
        return x[c * CHUNK:(c + 1) * CHUNK]

    def stack(x, c):
        return jnp.where(head_mask, jnp.concatenate([rows_of(x, c)] * HEADS_PER_GROUP, axis=0), 0.0)

    def stack_pair(x, c):
        xc = rows_of(x, c)
        parts = [xc[:, (h // 2) * LANES:(h // 2 + 1) * LANES] for h in range(HEADS_PER_GROUP)]
        return jnp.where(half_mask, jnp.concatenate(parts, axis=0), 0.0).astype(BF16)

    n_pairs = HEADS_PER_GROUP // 2
    pair_rows = [slice(q * 2 * CHUNK, (q + 1) * 2 * CHUNK) for q in range(n_pairs)]
    pair_lanes = [slice(q * LANES, (q + 1) * LANES) for q in range(n_pairs)]

    yield HEAD_STAGE

    a_p = [stack_pair(a_e, c) for c in chunks]
    b_p = [stack_pair(b_e, c) for c in chunks]
    k_p = [stack_pair(k_e, c) for c in chunks]
    v_p = [stack_pair(v, c) for c in chunks]
    p_a = [_dot_nt(a_p[c], jnp.concatenate([b_p[c], k_p[c]], axis=0)) for c in chunks]
    l_ab = [jnp.where(strict_lower, p_a[c][:, :STACK], 0.0) for c in chunks]
    l_ak = [jnp.where(strict_lower, p_a[c][:, STACK:], 0.0).astype(BF16) for c in chunks]
    yield HEAD_STAGE

    l_r = [[jnp.where(natural_lower,
                      _dot_nt(rows_of(r_e, c)[:, pair_lanes[q]].astype(BF16),
                              jnp.concatenate([b_p[c][pair_rows[q]], k_p[c][pair_rows[q]]], axis=0)),
                      0.0).astype(BF16) for q in range(n_pairs)] for c in chunks]
    yield HEAD_STAGE

    inv = [eye + l_ab[c] for c in chunks]
    power = [l_ab[c].astype(BF16) for c in chunks]
    for _ in range(int(math.log2(CHUNK)) - 1):
        power = [_dot(power[c], power[c]).astype(BF16) for c in chunks]
        yield HEAD_STAGE
        inv = [inv[c] + _dot(inv[c].astype(BF16), power[c]) for c in chunks]
        yield HEAD_STAGE

    x = [jnp.concatenate([a_p[c], _dot(l_ak[c], v_p[c]).astype(BF16)], axis=1) for c in chunks]
    yield HEAD_STAGE
    wu = [_dot(inv[c].astype(BF16), x[c]).astype(BF16) for c in chunks]
    yield HEAD_STAGE

    zero_pad = jnp.zeros((2 * CHUNK, LANES), BF16)
    for c in chunks:
        gn = _dot_tn(wu[c], stack(b_end, c).astype(BF16))
        n_s[c] = gn[LANES:] + _dot_tn(v_p[c], stack(k_end, c).astype(BF16))
        g_s[c] = jnp.where(head_mask, jnp.concatenate([gn[:LANES]] * n_pairs, axis=0), 0.0).astype(BF16)
        dec_s[c] = jnp.broadcast_to(jnp.exp(cum[(c + 1) * CHUNK - 1:(c + 1) * CHUNK, :]), (SUBLANES, W))
        yield HEAD_STAGE
        re_q, y0_q = [], []
        for q in range(n_pairs):
            rhs = jnp.concatenate([wu[c][pair_rows[q]],
                                   jnp.concatenate([zero_pad, v_p[c][pair_rows[q]]], axis=1)], axis=0)
            ry = _dot(l_r[c][q], rhs)
            re_q.append(rows_of(r_e, c)[:, pair_lanes[q]] + ry[:, :LANES])
            y0_q.append(ry[:, LANES:])
        re_s[pl.ds(c * CHUNK, CHUNK), :] = jnp.concatenate(re_q, axis=1).astype(BF16)
        y0_s[pl.ds(c * CHUNK, CHUNK), :] = jnp.concatenate(y0_q, axis=1)
        yield HEAD_STAGE
    bonus_s[...] = _dot((r * k * r_k).astype(BF16), seg_ones) * v
    gate_s[...] = _dot(mg_ref[...], g2_ref[...])
    yield HEAD_STAGE


def _split_dot_left(e, x):
    hi = x.astype(BF16)
    lo = (x - hi.astype(F32)).astype(BF16)
    return _dot(e, hi) + _dot(e, lo)


def rwkv_scan(rkv, mids, ups, v_mix, chan, seq, *, bt=256, groups=4):
    _, T, D = rkv.shape
    groups = min(groups, D // MXU_DIM)
    W = groups * MXU_DIM
    bt = min(bt, seq)
    assert D % W == 0 and seq % bt == 0 and bt % CHUNK == 0
    has_vlora = v_mix is not None
    nb = seq // bt

    def row_in(b, t):
        return b * nb + jnp.minimum(t, nb - 1)

    def stacked(p):
        return pl.BlockSpec((None, bt, W), lambda b, gidx, t: (p, row_in(b, t), gidx))

    def mid(a):
        return pl.BlockSpec((bt, a.shape[1]), lambda b, gidx, t: (row_in(b, t), 0))

    def up(a):
        return pl.BlockSpec((a.shape[0], W), lambda b, gidx, t: (0, gidx))

    ins = [rkv, rkv, rkv, *mids, *ups]
    specs = [stacked(0), stacked(1), stacked(2)] + [mid(a) for a in mids] + [up(a) for a in ups]
    if has_vlora:
        mv, v2, first_rkv = v_mix
        ins += [mv, v2, first_rkv]
        specs += [mid(mv), up(v2), stacked(2)]
    ins.append(chan)
    specs.append(pl.BlockSpec((SUBLANES, W), lambda b, gidx, t: (0, gidx)))
    nc = bt // CHUNK
    G = MXU_DIM
    stash = [pltpu.VMEM((2, groups, nc, G, G), BF16),
             pltpu.VMEM((2, groups, nc, 2 * HEAD, G), F32),
             pltpu.VMEM((2, groups, bt, G), BF16),
             pltpu.VMEM((2, groups, bt, G), F32),
             pltpu.VMEM((2, groups, nc, SUBLANES, G), F32),
             pltpu.VMEM((2, groups, bt, G), F32),
             pltpu.VMEM((2, groups, bt, G), F32)]
    return pl.pallas_call(
        functools.partial(_scan_body, n_chunks=nc, has_vlora=has_vlora, groups=groups),
        grid=(T // seq, D // W, nb + 1),
        in_specs=specs,
        out_specs=pl.BlockSpec((bt, W), lambda b, gidx, t: (b * nb + jnp.maximum(t - 1, 0), gidx)),
        out_shape=jax.ShapeDtypeStruct((T, D), BF16),
        scratch_shapes=[pltpu.VMEM((groups, 2 * HEAD, G), F32)] + stash,
        compiler_params=_params("parallel", "parallel", "arbitrary"),
        name="rwkv_scan",
    )(*ins)


def _bias_body(bmap_ref, rb_ref, o_ref):
    h = pl.program_id(0)
    bmap = bmap_ref[...]
    acc = jnp.full(bmap.shape, -jnp.inf, F32)
    for b in range(REL_BUCKETS):
        acc = jnp.where(bmap == b, rb_ref[b, h], acc)
    o_ref[0, 0] = acc
    kj = lax.broadcasted_iota(jnp.int32, bmap.shape, 1)
    o_ref[1, 0] = jnp.where(kj >= BLOCK, acc, -jnp.inf)


def bias_table(rel_bias, bmap):
    H = rel_bias.shape[1]
    return pl.pallas_call(
        _bias_body,
        grid=(H,),
        in_specs=[pl.BlockSpec((BLOCK, 2 * BLOCK), lambda h: (0, 0)),
                  pl.BlockSpec(memory_space=pltpu.SMEM)],
        out_specs=pl.BlockSpec((2, 1, BLOCK, 2 * BLOCK), lambda h: (0, h, 0, 0)),
        out_shape=jax.ShapeDtypeStruct((2, H, BLOCK, 2 * BLOCK), F32),
        compiler_params=_params("parallel"),
        name="rel_bias_table",
    )(bmap, rel_bias)


def _attn_body(q_ref, kp_ref, kc_ref, vp_ref, vc_ref, bias_ref, sink_ref, o_ref, *, group, n_kv):
    low = lax.broadcasted_iota(jnp.int32, (BLOCK, LANES), 1) < HEAD
    scale = HEAD ** -0.5

    for kv_pair in range(n_kv // 2):
        lanes = pl.ds(kv_pair * LANES, LANES)
        kband = jnp.concatenate([kp_ref[:, lanes], kc_ref[:, lanes]], axis=0)
        vband = jnp.concatenate([vp_ref[:, lanes], vc_ref[:, lanes]], axis=0)
        kswap = pltpu.roll(kband, HEAD, 1)
        vswap = pltpu.roll(vband, HEAD, 1)
        for e in range(2):
            kvh = 2 * kv_pair + e
            k_at = (kband, kswap) if e == 0 else (kswap, kband)
            v_at = (vband, vswap) if e == 0 else (vswap, vband)
            for pair in range(group // 2):
                q_lanes = pl.ds((kvh * group // 2 + pair) * LANES, LANES)
                qp = q_ref[:, q_lanes] * scale
                halves = []
                for half in range(2):
                    h = kvh * group + 2 * pair + half
                    in_half = low if half == 0 else jnp.logical_not(low)
                    qm = jnp.where(in_half, qp, jnp.zeros_like(qp))
                    s = _dot_nt(qm, k_at[half]) + bias_ref[0, h]
                    sink = sink_ref[h]
                    m = jnp.maximum(jnp.max(s, axis=-1, keepdims=True), sink)
                    p = jnp.exp(s - m)
                    denom = jnp.sum(p, axis=-1, keepdims=True) + jnp.exp(sink - m)
                    halves.append(_dot(p.astype(BF16), v_at[half]) / denom)
                o_ref[:, q_lanes] = jnp.where(low, halves[0], halves[1]).astype(o_ref.dtype)


def swa_attention(qkv, bias, sinks, seq, n_q_heads, n_kv_heads):
    T = qkv.shape[0]
    group = n_q_heads // n_kv_heads
    assert group % 2 == 0 and n_kv_heads % 2 == 0 and seq % BLOCK == 0
    nb = seq // BLOCK
    qw, kw = n_q_heads * HEAD, n_kv_heads * HEAD
    assert qw % kw == 0 and kw % LANES == 0
    k_col, v_col = qw // kw, qw // kw + 1

    def cur(col):
        return pl.BlockSpec((BLOCK, kw), lambda b, n: (b * nb + n, col))

    def prev(col):
        return pl.BlockSpec((BLOCK, kw), lambda b, n: (b * nb + jnp.maximum(n - 1, 0), col))

    return pl.pallas_call(
        functools.partial(_attn_body, group=group, n_kv=n_kv_heads),
        grid=(T // seq, nb),
        in_specs=[pl.BlockSpec((BLOCK, qw), lambda b, n: (b * nb + n, 0)),
                  prev(k_col), cur(k_col), prev(v_col), cur(v_col),
                  pl.BlockSpec((1, n_q_heads, BLOCK, 2 * BLOCK), lambda b, n: (jnp.where(n == 0, 1, 0), 0, 0, 0)),
                  pl.BlockSpec(memory_space=pltpu.SMEM)],
        out_specs=pl.BlockSpec((BLOCK, qw), lambda b, n: (b * nb + n, 0)),
        out_shape=jax.ShapeDtypeStruct((T, qw), BF16),
        compiler_params=_params("parallel", "parallel"),
        name="swa_attention",
    )(qkv, qkv, qkv, qkv, qkv, bias, sinks)


def _bucket_map():
    qi = jnp.arange(BLOCK)[:, None]
    kj = jnp.arange(2 * BLOCK)[None, :]
    signed = qi + BLOCK - kj
    dist = jnp.maximum(signed, 0)
    max_exact = REL_BUCKETS // 2
    d_f = jnp.maximum(dist, max_exact).astype(F32)
    large = max_exact + (jnp.log(d_f / max_exact) / math.log(REL_MAX_DIST / max_exact)
                         * (REL_BUCKETS - max_exact)).astype(jnp.int32)
    large = jnp.minimum(large, REL_BUCKETS - 1)
    bucket = jnp.where(dist < max_exact, dist, large)
    return jnp.where((signed >= 0) & (signed < WINDOW), bucket, -1).astype(jnp.int32)


def _ffn_up_body(x_ref, *refs, blocks_per_seq, sub, rsub, n_sub):
    wg_refs, wu_refs = refs[:n_sub], refs[n_sub:2 * n_sub]
    cg_ref, cu_ref, bg_ref, bu_ref, o_ref, tail_g_ref, tail_u_ref, raw_g_ref, raw_u_ref = refs[2 * n_sub:]
    first = pl.program_id(0) % blocks_per_seq == 0
    j = pl.program_id(1)
    bm = x_ref.shape[0]
    H = SUBLANES

    @pl.when((pl.program_id(0) == 0) & (j == 0))
    def _():
        tail_g_ref[...] = jnp.zeros_like(tail_g_ref)
        tail_u_ref[...] = jnp.zeros_like(tail_u_ref)

    def matmul_tile(slot, s, rb, halo_g, halo_u):
        xr = x_ref[pl.ds(rb * rsub, rsub), :]
        raw_g_ref[slot, 0:H, :] = halo_g
        raw_u_ref[slot, 0:H, :] = halo_u
        raw_g_ref[slot, H:, :] = _dot(xr, wg_refs[s][...])
        raw_u_ref[slot, H:, :] = _dot(xr, wu_refs[s][...])
        return raw_g_ref[slot, rsub:, :], raw_u_ref[slot, rsub:, :]

    def conv(raw_ref, slot, c_ref, b_ref, cols):
        h = raw_ref[slot, H:, :]
        h1 = raw_ref[slot, H - 1:H - 1 + rsub, :]
        h2 = raw_ref[slot, H - 2:H - 2 + rsub, :]
        return c_ref[0:1, cols] * h2 + c_ref[1:2, cols] * h1 + c_ref[2:3, cols] * h + b_ref[:, cols]

    def finish_tile(slot, s, rb):
        cols = pl.ds(s * sub, sub)
        gate = conv(raw_g_ref, slot, cg_ref, bg_ref, cols)
        up = conv(raw_u_ref, slot, cu_ref, bu_ref, cols)
        o_ref[pl.ds(rb * rsub, rsub), cols] = (gate * _sigmoid(gate) * up).astype(o_ref.dtype)

    n_rows = bm // rsub
    prev = None
    t = 0
    for s in range(n_sub):
        cols = pl.ds(s * sub, sub)
        halo_g = jnp.where(first, 0.0, tail_g_ref[j, :, cols])
        halo_u = jnp.where(first, 0.0, tail_u_ref[j, :, cols])
        for rb in range(n_rows):
            halo_g, halo_u = matmul_tile(t % 2, s, rb, halo_g, halo_u)
            if prev is not None:
                finish_tile(*prev)
            prev = (t % 2, s, rb)
            t += 1
        tail_g_ref[j, :, cols] = halo_g
        tail_u_ref[j, :, cols] = halo_u
    finish_tile(*prev)


def ffn_up(x, w2, layer, cg, cu, bg, bu, seq, *, bm=1024, bn=512, sub=256, rsub=256):
    T, D = x.shape
    F = w2.shape[2] // 2
    bm, bn = min(bm, seq), min(bn, F)
    sub, rsub = min(sub, bn), min(rsub, bm)
    assert seq % bm == 0 and bn % sub == 0 and F % sub == 0 and bm % rsub == 0 and rsub % SUBLANES == 0
    nf = pl.cdiv(F, bn)
    n_sub = bn // sub
    last = 2 * F // sub - 1

    def wspec(s, half):
        return pl.BlockSpec((None, D, sub),
                            lambda i, j: (layer, 0, jnp.minimum(half * (F // sub) + j * n_sub + s, last)))

    col = lambda rows: pl.BlockSpec((rows, bn), lambda i, j: (0, j))
    return pl.pallas_call(
        functools.partial(_ffn_up_body, blocks_per_seq=seq // bm, sub=sub, rsub=rsub, n_sub=n_sub),
        grid=(T // bm, nf),
        in_specs=[pl.BlockSpec((bm, D), lambda i, j: (i, 0))]
                 + [wspec(s, 0) for s in range(n_sub)] + [wspec(s, 1) for s in range(n_sub)]
                 + [col(CONV_W), col(CONV_W), col(1), col(1)],
        out_specs=pl.BlockSpec((bm, bn), lambda i, j: (i, j)),
        out_shape=jax.ShapeDtypeStruct((T, F), BF16),
        scratch_shapes=[pltpu.VMEM((nf, SUBLANES, bn), F32), pltpu.VMEM((nf, SUBLANES, bn), F32),
                        pltpu.VMEM((2, SUBLANES + rsub, sub), F32), pltpu.VMEM((2, SUBLANES + rsub, sub), F32)],
        compiler_params=_params("arbitrary", "arbitrary"),
        name="ffn_up",
    )(x, *([w2] * (2 * n_sub)), cg, cu, bg, bu)


def _pad_to(x, axis, mult):
    pad = (-x.shape[axis]) % mult
    if pad == 0:
        return x
    widths = [(0, 0)] * x.ndim
    widths[axis] = (0, pad)
    return jnp.pad(x, widths)


WIDE_COL_BLOCK = 1024
SCAN_TIME_BLOCK = 4 * CHUNK


def _lora_weights(w_down, w_up):
    return _pad_to(w_down, 1, LANES).astype(BF16), _pad_to(w_up, 0, LANES).astype(BF16)


def rwkv_layer(x, seq, first_rkv, layer, mix, w_rkv, w_o, w0, w1, w2, a0, a1, a2, g1, g2,
               k_k, k_a, r_k, gn_g, gn_b, v_lora):
    w1, w2 = _lora_weights(w1, w2)
    a1, a2 = _lora_weights(a1, a2)
    g1, g2 = _lora_weights(g1, g2)
    loras = [(MIX_W, w1, "tanh"), (MIX_A, a1, None), (MIX_G, g1, "sigmoid")]
    if v_lora is None:
        v0 = jnp.zeros_like(w0)
    else:
        v0, v1, v2 = v_lora
        v1, v2 = _lora_weights(v1, v2)
        loras.append((MIX_V, v1, None))
    xs, mids = token_shift_mix(x, mix, loras, seq)
    rkv = batched_matmul(xs, w_rkv, F32, nbatch=3, layer=layer, bn=WIDE_COL_BLOCK, name="rwkv_rkv")
    v_mix = None if v_lora is None else (mids[3], v2, first_rkv)
    chan = jnp.stack([w0, a0, k_k, k_a, r_k, gn_g, gn_b, v0])
    y = rwkv_scan(rkv, tuple(mids[:3]), (w2, a2, g2), v_mix, chan, seq, bt=SCAN_TIME_BLOCK)
    return matmul(y, w_o, BF16, layer=layer, bn=WIDE_COL_BLOCK, name="rwkv_out"), rkv


def attn_layer(xb, seq, layer, w_qkv, w_o, sinks, bias):
    n_q = w_o.shape[1] // HEAD
    n_kv = (w_qkv.shape[2] // HEAD - n_q) // 2
    qkv = matmul(xb, w_qkv, BF16, layer=layer, bn=WIDE_COL_BLOCK, name="attn_qkv")
    o = swa_attention(qkv, bias, sinks, seq, n_q, n_kv)
    return matmul(o, w_o, BF16, layer=layer, bn=WIDE_COL_BLOCK, name="attn_out")


FFN_COL_BLOCK = 512
FFN_ROW_BLOCK = 1024
FFN_COL_SUB = MXU_DIM


def ffn_layer(xb, seq, layer, w_up, conv_w, conv_b, w_down):
    f = w_down.shape[1]
    conv_b = conv_b.reshape(1, -1)
    act = ffn_up(xb, w_up, layer, conv_w[:, :f], conv_w[:, f:], conv_b[:, :f], conv_b[:, f:],
                 seq, bm=FFN_ROW_BLOCK, bn=FFN_COL_BLOCK, sub=FFN_COL_SUB)
    return matmul(act, w_down, BF16, layer=layer, bm=512, bn=512, name="ffn_down")


def kernel(x, rwkv_mix, rwkv_w_rkv, rwkv_w_o, rwkv_w0, rwkv_w1, rwkv_w2, rwkv_a0, rwkv_a1, rwkv_a2, rwkv_g1, rwkv_g2, rwkv_k_k, rwkv_k_a, rwkv_r_k, rwkv_gn_g, rwkv_gn_b, rwkv_v0, rwkv_v1, rwkv_v2, attn_w_qkv, attn_w_o, attn_sinks, rel_bias, ffn_w_up, ffn_conv_w, ffn_conv_b, ffn_w_down, ln1_g, ln1_b, ln2_g, ln2_b):
    B, S, D = x.shape
    depth = ln1_g.shape[0]
    xf = x.reshape(B * S, D)
    xb = xf.astype(BF16)
    bias = bias_table(rel_bias, _bucket_map())
    w_rkv, w_ro = rwkv_w_rkv.astype(BF16), rwkv_w_o.astype(BF16)
    w_qkv, w_ao = attn_w_qkv.astype(BF16), attn_w_o.astype(BF16)
    w_up, w_down = ffn_w_up.astype(BF16), ffn_w_down.astype(BF16)
    first_rkv = None
    for i in range(depth):
        j = i // 2
        if i % 2 == 0:
            v_lora = None if j == 0 else (rwkv_v0[j - 1], rwkv_v1[j - 1], rwkv_v2[j - 1])
            h, rkv = rwkv_layer(xf, S, first_rkv, j, rwkv_mix[j], w_rkv, w_ro,
                                rwkv_w0[j], rwkv_w1[j], rwkv_w2[j], rwkv_a0[j],
                                rwkv_a1[j], rwkv_a2[j], rwkv_g1[j], rwkv_g2[j], rwkv_k_k[j],
                                rwkv_k_a[j], rwkv_r_k[j], rwkv_gn_g[j], rwkv_gn_b[j], v_lora)
            if v_lora is None:
                first_rkv = rkv
        else:
            h = attn_layer(xb, S, j, w_qkv, w_ao, attn_sinks[j], bias)
        xf, xb = ln_residual(xf, h, ln1_g[i], ln1_b[i])
        f = ffn_layer(xb, S, i, w_up, ffn_conv_w[i], ffn_conv_b[i], w_down)
        xf, xb = ln_residual(xf, f, ln2_g[i], ln2_b[i])
    return xf.reshape(B, S, D)
```

```python
import functools
import math

import jax
import jax.numpy as jnp
from jax import lax
from jax.experimental import pallas as pl
from jax.experimental.pallas import tpu as pltpu

F32 = jnp.float32
BF16 = jnp.bfloat16

LANES = 128
SUBLANES = 8
MXU_DIM = 256
VMEM_LIMIT_BYTES = 56 * 1024 * 1024

HEAD = 64
HEADS_PER_GROUP = MXU_DIM // HEAD
CHUNK = 64
STACK = HEADS_PER_GROUP * CHUNK

WINDOW = 128
BLOCK = 128
REL_BUCKETS = 32
REL_MAX_DIST = 128
CONV_W = 3
LN_EPS = 1e-5
GN_EPS = HEAD * 1e-5
DEPTH = 4
DEEPNORM_ALPHA = (2 * DEPTH) ** 0.25


def _params(*sem):
    return pltpu.CompilerParams(dimension_semantics=sem, vmem_limit_bytes=VMEM_LIMIT_BYTES)


def _block(n, target, unit):
    if n <= target:
        return n
    b = target - target % unit
    while n % b:
        b -= unit
    return b


def _sigmoid(x):
    return 1.0 / (1.0 + jnp.exp(-x))


def _dot(a, b):
    return jnp.dot(a, b, preferred_element_type=F32)


def _dot_nt(a, b):
    return lax.dot_general(a, b, (((1,), (1,)), ((), ())), preferred_element_type=F32)


def _dot_tn(a, b):
    return lax.dot_general(a, b, (((0,), (0,)), ((), ())), preferred_element_type=F32)


def _mm_body(x_ref, w_ref, o_ref):
    o_ref[...] = _dot(x_ref[...], w_ref[...]).astype(o_ref.dtype)


def matmul(x, w, out_dtype, *, layer, bm=1024, bn=512, name="matmul"):
    M, K = x.shape
    N = w.shape[-1]
    bm, bn = _block(M, bm, SUBLANES), _block(N, bn, LANES)
    assert M % bm == 0 and N % bn == 0
    return pl.pallas_call(
        _mm_body,
        grid=(M // bm, N // bn),
        in_specs=[pl.BlockSpec((bm, K), lambda i, j: (i, 0)),
                  pl.BlockSpec((None, K, bn), lambda i, j: (layer, 0, j))],
        out_specs=pl.BlockSpec((bm, bn), lambda i, j: (i, j)),
        out_shape=jax.ShapeDtypeStruct((M, N), out_dtype),
        compiler_params=_params("parallel", "parallel"),
        name=name,
    )(x, w)


def batched_matmul(x, w, out_dtype, *, nbatch, layer, bm=1024, bn=512, name="bmm"):
    M, K = x.shape[-2:]
    N = w.shape[-1]
    bm, bn = _block(M, bm, SUBLANES), _block(N, bn, LANES)
    assert M % bm == 0 and N % bn == 0
    return pl.pallas_call(
        _mm_body,
        grid=(nbatch, M // bm, N // bn),
        in_specs=[pl.BlockSpec((None, bm, K), lambda p, i, j: (p, i, 0)),
                  pl.BlockSpec((None, None, K, bn), lambda p, i, j: (layer, p, 0, j))],
        out_specs=pl.BlockSpec((None, bm, bn), lambda p, i, j: (p, i, j)),
        out_shape=jax.ShapeDtypeStruct((nbatch, M, N), out_dtype),
        compiler_params=_params("parallel", "parallel", "parallel"),
        name=name,
    )(x, w)


def _ln_body(x_ref, h_ref, g_ref, b_ref, of_ref, ob_ref):
    y = DEEPNORM_ALPHA * x_ref[...] + h_ref[...].astype(F32)
    mu = jnp.mean(y, axis=-1, keepdims=True)
    d = y - mu
    var = jnp.mean(d * d, axis=-1, keepdims=True)
    out = d * lax.rsqrt(var + LN_EPS) * g_ref[...] + b_ref[...]
    of_ref[...] = out
    ob_ref[...] = out.astype(BF16)


def ln_residual(x, h, g, b, *, br=256):
    T, D = x.shape
    br = min(br, T)
    assert T % br == 0
    row = pl.BlockSpec((br, D), lambda i: (i, 0))
    vec = pl.BlockSpec((1, D), lambda i: (0, 0))
    return pl.pallas_call(
        _ln_body,
        grid=(T // br,),
        in_specs=[row, row, vec, vec],
        out_specs=[row, row],
        out_shape=[jax.ShapeDtypeStruct((T, D), F32), jax.ShapeDtypeStruct((T, D), BF16)],
        compiler_params=_params("parallel"),
        name="ln_residual",
    )(x, h, g.reshape(1, D), b.reshape(1, D))


MIX_R, MIX_W, MIX_K, MIX_V, MIX_A, MIX_G = range(6)


def _mix_body(x_ref, xp_ref, mix_ref, *refs, blocks_per_seq, lora_rows, lora_acts):
    n = len(lora_rows)
    w_refs, o_ref, mid_refs, acc_refs = refs[:n], refs[n], refs[n + 1:2 * n + 1], refs[2 * n + 1:]
    i, j = pl.program_id(0), pl.program_id(1)
    x = x_ref[...]
    prev_last = xp_ref[SUBLANES - 1:SUBLANES, :]
    prev_last = jnp.where(i % blocks_per_seq == 0, 0.0, prev_last)
    row = lax.broadcasted_iota(jnp.int32, x.shape, 0)
    shifted = jnp.where(row == 0, prev_last, pltpu.roll(x, 1, 0))
    xx = shifted - x

    def mixed(r):
        return (x + xx * mix_ref[r:r + 1, :]).astype(BF16)

    for p, r in enumerate((MIX_R, MIX_K, MIX_V)):
        o_ref[p] = mixed(r)

    @pl.when(j == 0)
    def _():
        for acc in acc_refs:
            acc[...] = jnp.zeros_like(acc)

    for q in range(n):
        acc_refs[q][...] += _dot(mixed(lora_rows[q]), w_refs[q][...])

    @pl.when(j == pl.num_programs(1) - 1)
    def _():
        for q in range(n):
            acc = acc_refs[q][...]
            if lora_acts[q] == "tanh":
                acc = jnp.tanh(acc)
            elif lora_acts[q] == "sigmoid":
                acc = _sigmoid(acc)
            mid_refs[q][...] = acc.astype(BF16)


def token_shift_mix(x, mix, loras, seq, *, bt=512, bd=1024):
    T, D = x.shape
    bt, bd = min(bt, seq), min(bd, D)
    assert seq % bt == 0 and D % bd == 0 and bt % SUBLANES == 0
    halo = bt // SUBLANES
    widths = [w.shape[1] for _, w, _ in loras]
    outs = pl.pallas_call(
        functools.partial(_mix_body, blocks_per_seq=seq // bt, lora_rows=tuple(r for r, _, _ in loras),
                          lora_acts=tuple(a for _, _, a in loras)),
        grid=(T // bt, D // bd),
        in_specs=[pl.BlockSpec((bt, bd), lambda i, j: (i, j)),
                  pl.BlockSpec((SUBLANES, bd), lambda i, j: (jnp.maximum(i * halo - 1, 0), j)),
                  pl.BlockSpec((6, bd), lambda i, j: (0, j))]
                 + [pl.BlockSpec((bd, n), lambda i, j: (j, 0)) for n in widths],
        out_specs=[pl.BlockSpec((3, bt, bd), lambda i, j: (0, i, j))]
                  + [pl.BlockSpec((bt, n), lambda i, j: (i, 0)) for n in widths],
        out_shape=[jax.ShapeDtypeStruct((3, T, D), BF16)] + [jax.ShapeDtypeStruct((T, n), BF16) for n in widths],
        scratch_shapes=[pltpu.VMEM((bt, n), F32) for n in widths],
        compiler_params=_params("parallel", "arbitrary"),
        name="rwkv_mix",
    )(x, x, mix, *[w for _, w, _ in loras])
    return outs[0], outs[1:]


PREP_STAGE, HEAD_STAGE, TAIL_STAGE = "prep", "head", "tail"
HEAD_STAGES_BEFORE_NEXT_GROUP = 11


def _scan_body(*refs, n_chunks, has_vlora, groups):
    n_in = len(refs) - 9
    ins, o_ref = refs[:n_in], refs[n_in]
    s_ref, stash = refs[n_in + 1], refs[n_in + 2:]
    t = pl.program_id(2)
    last = pl.num_programs(2) - 1
    write_slot = t % 2
    read_slot = 1 - write_slot
    shared = (3, 4, 5, 9) if has_vlora else (3, 4, 5)
    lanes = [pl.ds(g * MXU_DIM, MXU_DIM) for g in range(groups)]

    def heads():
        return [_scan_head(*[ref if i in shared else ref.at[:, lanes[g]] for i, ref in enumerate(ins)],
                           *[ref.at[write_slot, g] for ref in stash], n_chunks=n_chunks, has_vlora=has_vlora)
                for g in range(groups)]

    def tail():
        return _scan_tail([s_ref.at[g] for g in range(groups)],
                          [[ref.at[read_slot, g] for ref in stash] for g in range(groups)],
                          [ins[-1].at[:, lanes[g]] for g in range(groups)],
                          [o_ref.at[:, lanes[g]] for g in range(groups)], n_chunks=n_chunks)

    def run(pending, tail_gen):
        active = []
        while pending or active or tail_gen is not None:
            if pending and all(done >= HEAD_STAGES_BEFORE_NEXT_GROUP for _, done in active):
                active.append([pending.pop(0), 0])
            for entry in list(active):
                try:
                    entry[1] += next(entry[0]) == HEAD_STAGE
                except StopIteration:
                    active.remove(entry)
            if tail_gen is not None and next(tail_gen, None) is None:
                tail_gen = None

    @pl.when(t == 0)
    def _():
        s_ref[...] = jnp.zeros_like(s_ref)
        run(heads(), None)

    @pl.when((t > 0) & (t < last))
    def _():
        run(heads(), tail())

    @pl.when(t == last)
    def _():
        run([], tail())


def _scan_masks():
    W = MXU_DIM
    head_mask = (lax.broadcasted_iota(jnp.int32, (STACK, W), 0) // CHUNK
                 == lax.broadcasted_iota(jnp.int32, (STACK, W), 1) // HEAD)
    seg_ones = (lax.broadcasted_iota(jnp.int32, (W, W), 0) // HEAD
                == lax.broadcasted_iota(jnp.int32, (W, W), 1) // HEAD).astype(BF16)
    return head_mask, seg_ones


def _scan_tail(s_refs, stashes, p_refs, o_refs, *, n_chunks):
    head_mask, seg_ones = _scan_masks()
    groups = range(len(s_refs))
    chunks = range(n_chunks)
    n_pairs = HEADS_PER_GROUP // 2
    states = [s_refs[g][...] for g in groups]
    ys = [[] for _ in groups]
    for c in chunks:
        rows = pl.ds(c * CHUNK, CHUNK)
        for g in groups:
            g_v, n_v, re_v, y0_v, dec_v = stashes[g][:5]
            sb = states[g].astype(BF16)
            s_big = jnp.where(head_mask, jnp.concatenate([sb] * n_pairs, axis=0), jnp.zeros((), BF16))
            ys[g].append(_dot_nt(re_v[rows, :], s_big) + y0_v[rows, :])
            states[g] = states[g] * dec_v[c, 0:1, :] + _dot(sb, g_v[c]) + n_v[c]
        yield TAIL_STAGE
    for g in groups:
        s_refs[g][...] = states[g]
    for g in groups:
        bonus_v, gate_v = stashes[g][5:]
        gn_g = p_refs[g][5:6, :]
        gn_b = p_refs[g][6:7, :]
        y = jnp.concatenate(ys[g], axis=0)
        mu = _dot(y.astype(BF16), seg_ones) * (1.0 / HEAD)
        d = y - mu
        var = _dot((d * d).astype(BF16), seg_ones) * (1.0 / HEAD)
        yield TAIL_STAGE
        yn = d * lax.rsqrt(var + GN_EPS) * gn_g + gn_b
        o_refs[g][...] = ((yn + bonus_v[...]) * gate_v[...]).astype(o_refs[g].dtype)
        yield TAIL_STAGE


def _scan_head(*refs, n_chunks, has_vlora):
    stash_refs = refs[-7:]
    if has_vlora:
        (r_ref, k_ref, v_ref, mw_ref, ma_ref, mg_ref, w2_ref, a2_ref, g2_ref,
         mv_ref, v2_ref, vf_ref, p_ref) = refs[:-7]
    else:
        r_ref, k_ref, v_ref, mw_ref, ma_ref, mg_ref, w2_ref, a2_ref, g2_ref, p_ref = refs[:-7]
    g_s, n_s, re_s, y0_s, dec_s, bonus_s, gate_s = stash_refs

    W = MXU_DIM
    head_mask, seg_ones = _scan_masks()
    rr = lax.broadcasted_iota(jnp.int32, (STACK, STACK), 0)
    cc = lax.broadcasted_iota(jnp.int32, (STACK, STACK), 1)
    same_head = (rr // CHUNK) == (cc // CHUNK)
    strict_lower = same_head & ((rr % CHUNK) > (cc % CHUNK))
    eye = (rr == cc).astype(F32)
    half_mask = ((lax.broadcasted_iota(jnp.int32, (STACK, LANES), 0) // CHUNK) % 2
                 == lax.broadcasted_iota(jnp.int32, (STACK, LANES), 1) // HEAD)
    natural_lower = (lax.broadcasted_iota(jnp.int32, (CHUNK, 4 * CHUNK), 0)
                     >= lax.broadcasted_iota(jnp.int32, (CHUNK, 4 * CHUNK), 1) % CHUNK)
    bt = n_chunks * CHUNK
    tr = lax.broadcasted_iota(jnp.int32, (bt, bt), 0)
    tc = lax.broadcasted_iota(jnp.int32, (bt, bt), 1)
    tri = ((tr // CHUNK == tc // CHUNK) & (tr >= tc)).astype(BF16)

    w0 = p_ref[0:1, :]
    a0 = p_ref[1:2, :]
    k_k = p_ref[2:3, :]
    k_a = p_ref[3:4, :]
    r_k = p_ref[4:5, :]
    v0 = p_ref[7:8, :]

    chunks = range(n_chunks)

    r = r_ref[...]
    k = k_ref[...]
    v = v_ref[...]
    z = w0 + _dot(mw_ref[...], w2_ref[...])
    w_log = -(jnp.maximum(-z, 0.0) + jnp.log(1.0 + jnp.exp(-jnp.abs(z)))) - 0.5
    log_decay = -jnp.exp(w_log)
    yield PREP_STAGE
    a = _sigmoid(a0 + _dot(ma_ref[...], a2_ref[...]))
    if has_vlora:
        v = v + (vf_ref[...] - v) * _sigmoid(v0 + _dot(mv_ref[...], v2_ref[...]))
    yield PREP_STAGE
    kk = k * k_k
    norm = jnp.sqrt(_dot((kk * kk).astype(BF16), seg_ones))
    kk = kk / jnp.maximum(norm, 1e-12)
    k = k * (1.0 + (a - 1.0) * k_a)
    yield PREP_STAGE

    cum = _split_dot_left(tri, log_decay)
    cum_last = jnp.concatenate(
        [jnp.broadcast_to(cum[(c + 1) * CHUNK - 1:(c + 1) * CHUNK, :], (CHUNK, W)) for c in chunks], axis=0)
    e_pos = jnp.exp(cum)
    e_neg = jnp.exp(-cum)
    yield PREP_STAGE
    e_prev = jnp.exp(cum - log_decay)
    e_tail = jnp.exp(cum_last - cum)
    kka = kk * a
    yield PREP_STAGE
    a_e = -kk * e_prev
    r_e = r * e_pos
    b_e = kka * e_neg
    yield PREP_STAGE
    k_e = k * e_neg
    b_end = kka * e_tail
    k_end = k * e_tail

    def rows_of(x, c):
        return x[c * CHUNK:(c + 1) * CHUNK]

    def stack(x, c):
        return jnp.where(head_mask, jnp.concatenate([rows_of(x, c)] * HEADS_PER_GROUP, axis=0), 0.0)

    def stack_pair(x, c):
        xc = rows_of(x, c)
        parts = [xc[:, (h // 2) * LANES:(h // 2 + 1) * LANES] for h in range(HEADS_PER_GROUP)]
        return jnp.where(half_mask, jnp.concatenate(parts, axis=0), 0.0).astype(BF16)

    n_pairs = HEADS_PER_GROUP // 2
    pair_rows = [slice(q * 2 * CHUNK, (q + 1) * 2 * CHUNK) for q in range(n_pairs)]
    pair_lanes = [slice(q * LANES, (q + 1) * LANES) for q in range(n_pairs)]

    yield HEAD_STAGE

    a_p = [stack_pair(a_e, c) for c in chunks]
    b_p = [stack_pair(b_e, c) for c in chunks]
    k_p = [stack_pair(k_e, c) for c in chunks]
    v_p = [stack_pair(v, c) for c in chunks]
    p_a = [_dot_nt(a_p[c], jnp.concatenate([b_p[c], k_p[c]], axis=0)) for c in chunks]
    l_ab = [jnp.where(strict_lower, p_a[c][:, :STACK], 0.0) for c in chunks]
    l_ak = [jnp.where(strict_lower, p_a[c][:, STACK:], 0.0).astype(BF16) for c in chunks]
    yield HEAD_STAGE

    l_r = [[jnp.where(natural_lower,
                      _dot_nt(rows_of(r_e, c)[:, pair_lanes[q]].astype(BF16),
                              jnp.concatenate([b_p[c][pair_rows[q]], k_p[c][pair_rows[q]]], axis=0)),
                      0.0).astype(BF16) for q in range(n_pairs)] for c in chunks]
    yield HEAD_STAGE

    inv = [eye + l_ab[c] for c in chunks]
    power = [l_ab[c].astype(BF16) for c in chunks]
    for _ in range(int(math.log2(CHUNK)) - 1):
        power = [_dot(power[c], power[c]).astype(BF16) for c in chunks]
        yield HEAD_STAGE
        inv = [inv[c] + _dot(inv[c].astype(BF16), power[c]) for c in chunks]
        yield HEAD_STAGE

    x = [jnp.concatenate([a_p[c], _dot(l_ak[c], v_p[c]).astype(BF16)], axis=1) for c in chunks]
    yield HEAD_STAGE
    wu = [_dot(inv[c].astype(BF16), x[c]).astype(BF16) for c in chunks]
    yield HEAD_STAGE

    zero_pad = jnp.zeros((2 * CHUNK, LANES), BF16)
    for c in chunks:
        gn = _dot_tn(wu[c], stack(b_end, c).astype(BF16))
        n_s[c] = gn[LANES:] + _dot_tn(v_p[c], stack(k_end, c).astype(BF16))
        g_s[c] = jnp.where(head_mask, jnp.concatenate([gn[:LANES]] * n_pairs, axis=0), 0.0).astype(BF16)
        dec_s[c] = jnp.broadcast_to(jnp.exp(cum[(c + 1) * CHUNK - 1:(c + 1) * CHUNK, :]), (SUBLANES, W))
        yield HEAD_STAGE
        re_q, y0_q = [], []
        for q in range(n_pairs):
            rhs = jnp.concatenate([wu[c][pair_rows[q]],
                                   jnp.concatenate([zero_pad, v_p[c][pair_rows[q]]], axis=1)], axis=0)
            ry = _dot(l_r[c][q], rhs)
            re_q.append(rows_of(r_e, c)[:, pair_lanes[q]] + ry[:, :LANES])
            y0_q.append(ry[:, LANES:])
        re_s[pl.ds(c * CHUNK, CHUNK), :] = jnp.concatenate(re_q, axis=1).astype(BF16)
        y0_s[pl.ds(c * CHUNK, CHUNK), :] = jnp.concatenate(y0_q, axis=1)
        yield HEAD_STAGE
    bonus_s[...] = _dot((r * k * r_k).astype(BF16), seg_ones) * v
    gate_s[...] = _dot(mg_ref[...], g2_ref[...])
    yield HEAD_STAGE


def _split_dot_left(e, x):
    hi = x.astype(BF16)
    lo = (x - hi.astype(F32)).astype(BF16)
    return _dot(e, hi) + _dot(e, lo)


def rwkv_scan(rkv, mids, ups, v_mix, chan, seq, *, bt=256, groups=4):
    _, T, D = rkv.shape
    groups = min(groups, D // MXU_DIM)
    W = groups * MXU_DIM
    bt = min(bt, seq)
    assert D % W == 0 and seq % bt == 0 and bt % CHUNK == 0
    has_vlora = v_mix is not None
    nb = seq // bt

    def row_in(b, t):
        return b * nb + jnp.minimum(t, nb - 1)

    def stacked(p):
        return pl.BlockSpec((None, bt, W), lambda b, gidx, t: (p, row_in(b, t), gidx))

    def mid(a):
        return pl.BlockSpec((bt, a.shape[1]), lambda b, gidx, t: (row_in(b, t), 0))

    def up(a):
        return pl.BlockSpec((a.shape[0], W), lambda b, gidx, t: (0, gidx))

    ins = [rkv, rkv, rkv, *mids, *ups]
    specs = [stacked(0), stacked(1), stacked(2)] + [mid(a) for a in mids] + [up(a) for a in ups]
    if has_vlora:
        mv, v2, first_rkv = v_mix
        ins += [mv, v2, first_rkv]
        specs += [mid(mv), up(v2), stacked(2)]
    ins.append(chan)
    specs.append(pl.BlockSpec((SUBLANES, W), lambda b, gidx, t: (0, gidx)))
    nc = bt // CHUNK
    G = MXU_DIM
    stash = [pltpu.VMEM((2, groups, nc, G, G), BF16),
             pltpu.VMEM((2, groups, nc, 2 * HEAD, G), F32),
             pltpu.VMEM((2, groups, bt, G), BF16),
             pltpu.VMEM((2, groups, bt, G), F32),
             pltpu.VMEM((2, groups, nc, SUBLANES, G), F32),
             pltpu.VMEM((2, groups, bt, G), F32),
             pltpu.VMEM((2, groups, bt, G), F32)]
    return pl.pallas_call(
        functools.partial(_scan_body, n_chunks=nc, has_vlora=has_vlora, groups=groups),
        grid=(T // seq, D // W, nb + 1),
        in_specs=specs,
        out_specs=pl.BlockSpec((bt, W), lambda b, gidx, t: (b * nb + jnp.maximum(t - 1, 0), gidx)),
        out_shape=jax.ShapeDtypeStruct((T, D), BF16),
        scratch_shapes=[pltpu.VMEM((groups, 2 * HEAD, G), F32)] + stash,
        compiler_params=_params("parallel", "parallel", "arbitrary"),
        name="rwkv_scan",
    )(*ins)


def _bias_body(bmap_ref, rb_ref, o_ref):
    h = pl.program_id(0)
    bmap = bmap_ref[...]
    acc = jnp.full(bmap.shape, -jnp.inf, F32)
    for b in range(REL_BUCKETS):
        acc = jnp.where(bmap == b, rb_ref[b, h], acc)
    o_ref[0, 0] = acc
    kj = lax.broadcasted_iota(jnp.int32, bmap.shape, 1)
    o_ref[1, 0] = jnp.where(kj >= BLOCK, acc, -jnp.inf)


def bias_table(rel_bias, bmap):
    H = rel_bias.shape[1]
    return pl.pallas_call(
        _bias_body,
        grid=(H,),
        in_specs=[pl.BlockSpec((BLOCK, 2 * BLOCK), lambda h: (0, 0)),
                  pl.BlockSpec(memory_space=pltpu.SMEM)],
        out_specs=pl.BlockSpec((2, 1, BLOCK, 2 * BLOCK), lambda h: (0, h, 0, 0)),
        out_shape=jax.ShapeDtypeStruct((2, H, BLOCK, 2 * BLOCK), F32),
        compiler_params=_params("parallel"),
        name="rel_bias_table",
    )(bmap, rel_bias)


def _attn_body(q_ref, kp_ref, kc_ref, vp_ref, vc_ref, bias_ref, sink_ref, o_ref, *, group, n_kv):
    low = lax.broadcasted_iota(jnp.int32, (BLOCK, LANES), 1) < HEAD
    scale = HEAD ** -0.5

    for kv_pair in range(n_kv // 2):
        lanes = pl.ds(kv_pair * LANES, LANES)
        kband = jnp.concatenate([kp_ref[:, lanes], kc_ref[:, lanes]], axis=0)
        vband = jnp.concatenate([vp_ref[:, lanes], vc_ref[:, lanes]], axis=0)
        kswap = pltpu.roll(kband, HEAD, 1)
        vswap = pltpu.roll(vband, HEAD, 1)
        for e in range(2):
            kvh = 2 * kv_pair + e
            k_at = (kband, kswap) if e == 0 else (kswap, kband)
            v_at = (vband, vswap) if e == 0 else (vswap, vband)
            for pair in range(group // 2):
                q_lanes = pl.ds((kvh * group // 2 + pair) * LANES, LANES)
                qp = q_ref[:, q_lanes] * scale
                halves = []
                for half in range(2):
                    h = kvh * group + 2 * pair + half
                    in_half = low if half == 0 else jnp.logical_not(low)
                    qm = jnp.where(in_half, qp, jnp.zeros_like(qp))
                    s = _dot_nt(qm, k_at[half]) + bias_ref[0, h]
                    sink = sink_ref[h]
                    m = jnp.maximum(jnp.max(s, axis=-1, keepdims=True), sink)
                    p = jnp.exp(s - m)
                    denom = jnp.sum(p, axis=-1, keepdims=True) + jnp.exp(sink - m)
                    halves.append(_dot(p.astype(BF16), v_at[half]) / denom)
                o_ref[:, q_lanes] = jnp.where(low, halves[0], halves[1]).astype(o_ref.dtype)


def swa_attention(qkv, bias, sinks, seq, n_q_heads, n_kv_heads):
    T = qkv.shape[0]
    group = n_q_heads // n_kv_heads
    assert group % 2 == 0 and n_kv_heads % 2 == 0 and seq % BLOCK == 0
    nb = seq // BLOCK
    qw, kw = n_q_heads * HEAD, n_kv_heads * HEAD
    assert qw % kw == 0 and kw % LANES == 0
    k_col, v_col = qw // kw, qw // kw + 1

    def cur(col):
        return pl.BlockSpec((BLOCK, kw), lambda b, n: (b * nb + n, col))

    def prev(col):
        return pl.BlockSpec((BLOCK, kw), lambda b, n: (b * nb + jnp.maximum(n - 1, 0), col))

    return pl.pallas_call(
        functools.partial(_attn_body, group=group, n_kv=n_kv_heads),
        grid=(T // seq, nb),
        in_specs=[pl.BlockSpec((BLOCK, qw), lambda b, n: (b * nb + n, 0)),
                  prev(k_col), cur(k_col), prev(v_col), cur(v_col),
                  pl.BlockSpec((1, n_q_heads, BLOCK, 2 * BLOCK), lambda b, n: (jnp.where(n == 0, 1, 0), 0, 0, 0)),
                  pl.BlockSpec(memory_space=pltpu.SMEM)],
        out_specs=pl.BlockSpec((BLOCK, qw), lambda b, n: (b * nb + n, 0)),
        out_shape=jax.ShapeDtypeStruct((T, qw), BF16),
        compiler_params=_params("parallel", "parallel"),
        name="swa_attention",
    )(qkv, qkv, qkv, qkv, qkv, bias, sinks)


def _bucket_map():
    qi = jnp.arange(BLOCK)[:, None]
    kj = jnp.arange(2 * BLOCK)[None, :]
    signed = qi + BLOCK - kj
    dist = jnp.maximum(signed, 0)
    max_exact = REL_BUCKETS // 2
    d_f = jnp.maximum(dist, max_exact).astype(F32)
    large = max_exact + (jnp.log(d_f / max_exact) / math.log(REL_MAX_DIST / max_exact)
                         * (REL_BUCKETS - max_exact)).astype(jnp.int32)
    large = jnp.minimum(large, REL_BUCKETS - 1)
    bucket = jnp.where(dist < max_exact, dist, large)
    return jnp.where((signed >= 0) & (signed < WINDOW), bucket, -1).astype(jnp.int32)


def _ffn_up_body(x_ref, *refs, blocks_per_seq, sub, rsub, n_sub):
    wg_refs, wu_refs = refs[:n_sub], refs[n_sub:2 * n_sub]
    cg_ref, cu_ref, bg_ref, bu_ref, o_ref, tail_g_ref, tail_u_ref, raw_g_ref, raw_u_ref = refs[2 * n_sub:]
    first = pl.program_id(0) % blocks_per_seq == 0
    j = pl.program_id(1)
    bm = x_ref.shape[0]
    H = SUBLANES

    @pl.when((pl.program_id(0) == 0) & (j == 0))
    def _():
        tail_g_ref[...] = jnp.zeros_like(tail_g_ref)
        tail_u_ref[...] = jnp.zeros_like(tail_u_ref)

    def matmul_tile(slot, s, rb, halo_g, halo_u):
        xr = x_ref[pl.ds(rb * rsub, rsub), :]
        raw_g_ref[slot, 0:H, :] = halo_g
        raw_u_ref[slot, 0:H, :] = halo_u
        raw_g_ref[slot, H:, :] = _dot(xr, wg_refs[s][...])
        raw_u_ref[slot, H:, :] = _dot(xr, wu_refs[s][...])
        return raw_g_ref[slot, rsub:, :], raw_u_ref[slot, rsub:, :]

    def conv(raw_ref, slot, c_ref, b_ref, cols):
        h = raw_ref[slot, H:, :]
        h1 = raw_ref[slot, H - 1:H - 1 + rsub, :]
        h2 = raw_ref[slot, H - 2:H - 2 + rsub, :]
        return c_ref[0:1, cols] * h2 + c_ref[1:2, cols] * h1 + c_ref[2:3, cols] * h + b_ref[:, cols]

    def finish_tile(slot, s, rb):
        cols = pl.ds(s * sub, sub)
        gate = conv(raw_g_ref, slot, cg_ref, bg_ref, cols)
        up = conv(raw_u_ref, slot, cu_ref, bu_ref, cols)
        o_ref[pl.ds(rb * rsub, rsub), cols] = (gate * _sigmoid(gate) * up).astype(o_ref.dtype)

    n_rows = bm // rsub
    prev = None
    t = 0
    for s in range(n_sub):
        cols = pl.ds(s * sub, sub)
        halo_g = jnp.where(first, 0.0, tail_g_ref[j, :, cols])
        halo_u = jnp.where(first, 0.0, tail_u_ref[j, :, cols])
        for rb in range(n_rows):
            halo_g, halo_u = matmul_tile(t % 2, s, rb, halo_g, halo_u)
            if prev is not None:
                finish_tile(*prev)
            prev = (t % 2, s, rb)
            t += 1
        tail_g_ref[j, :, cols] = halo_g
        tail_u_ref[j, :, cols] = halo_u
    finish_tile(*prev)


def ffn_up(x, w2, layer, cg, cu, bg, bu, seq, *, bm=1024, bn=512, sub=256, rsub=256):
    T, D = x.shape
    F = w2.shape[2] // 2
    bm, bn = min(bm, seq), min(bn, F)
    sub, rsub = min(sub, bn), min(rsub, bm)
    assert seq % bm == 0 and bn % sub == 0 and F % sub == 0 and bm % rsub == 0 and rsub % SUBLANES == 0
    nf = pl.cdiv(F, bn)
    n_sub = bn // sub
    last = 2 * F // sub - 1

    def wspec(s, half):
        return pl.BlockSpec((None, D, sub),
                            lambda i, j: (layer, 0, jnp.minimum(half * (F // sub) + j * n_sub + s, last)))

    col = lambda rows: pl.BlockSpec((rows, bn), lambda i, j: (0, j))
    return pl.pallas_call(
        functools.partial(_ffn_up_body, blocks_per_seq=seq // bm, sub=sub, rsub=rsub, n_sub=n_sub),
        grid=(T // bm, nf),
        in_specs=[pl.BlockSpec((bm, D), lambda i, j: (i, 0))]
                 + [wspec(s, 0) for s in range(n_sub)] + [wspec(s, 1) for s in range(n_sub)]
                 + [col(CONV_W), col(CONV_W), col(1), col(1)],
        out_specs=pl.BlockSpec((bm, bn), lambda i, j: (i, j)),
        out_shape=jax.ShapeDtypeStruct((T, F), BF16),
        scratch_shapes=[pltpu.VMEM((nf, SUBLANES, bn), F32), pltpu.VMEM((nf, SUBLANES, bn), F32),
                        pltpu.VMEM((2, SUBLANES + rsub, sub), F32), pltpu.VMEM((2, SUBLANES + rsub, sub), F32)],
        compiler_params=_params("arbitrary", "arbitrary"),
        name="ffn_up",
    )(x, *([w2] * (2 * n_sub)), cg, cu, bg, bu)


def _pad_to(x, axis, mult):
    pad = (-x.shape[axis]) % mult
    if pad == 0:
        return x
    widths = [(0, 0)] * x.ndim
    widths[axis] = (0, pad)
    return jnp.pad(x, widths)


WIDE_COL_BLOCK = 1024
SCAN_TIME_BLOCK = 4 * CHUNK


def _lora_weights(w_down, w_up):
    return _pad_to(w_down, 1, LANES).astype(BF16), _pad_to(w_up, 0, LANES).astype(BF16)


def rwkv_layer(x, seq, first_rkv, layer, mix, w_rkv, w_o, w0, w1, w2, a0, a1, a2, g1, g2,
               k_k, k_a, r_k, gn_g, gn_b, v_lora):
    w1, w2 = _lora_weights(w1, w2)
    a1, a2 = _lora_weights(a1, a2)
    g1, g2 = _lora_weights(g1, g2)
    loras = [(MIX_W, w1, "tanh"), (MIX_A, a1, None), (MIX_G, g1, "sigmoid")]
    if v_lora is None:
        v0 = jnp.zeros_like(w0)
    else:
        v0, v1, v2 = v_lora
        v1, v2 = _lora_weights(v1, v2)
        loras.append((MIX_V, v1, None))
    xs, mids = token_shift_mix(x, mix, loras, seq)
    rkv = batched_matmul(xs, w_rkv, F32, nbatch=3, layer=layer, bn=WIDE_COL_BLOCK, name="rwkv_rkv")
    v_mix = None if v_lora is None else (mids[3], v2, first_rkv)
    chan = jnp.stack([w0, a0, k_k, k_a, r_k, gn_g, gn_b, v0])
    y = rwkv_scan(rkv, tuple(mids[:3]), (w2, a2, g2), v_mix, chan, seq, bt=SCAN_TIME_BLOCK)
    return matmul(y, w_o, BF16, layer=layer, bn=WIDE_COL_BLOCK, name="rwkv_out"), rkv


def attn_layer(xb, seq, layer, w_qkv, w_o, sinks, bias):
    n_q = w_o.shape[1] // HEAD
    n_kv = (w_qkv.shape[2] // HEAD - n_q) // 2
    qkv = matmul(xb, w_qkv, BF16, layer=layer, bn=WIDE_COL_BLOCK, name="attn_qkv")
    o = swa_attention(qkv, bias, sinks, seq, n_q, n_kv)
    return matmul(o, w_o, BF16, layer=layer, bn=WIDE_COL_BLOCK, name="attn_out")


FFN_COL_BLOCK = 512
FFN_ROW_BLOCK = 1024
FFN_COL_SUB = MXU_DIM


def ffn_layer(xb, seq, layer, w_up, conv_w, conv_b, w_down):
    f = w_down.shape[1]
    conv_b = conv_b.reshape(1, -1)
    act = ffn_up(xb, w_up, layer, conv_w[:, :f], conv_w[:, f:], conv_b[:, :f], conv_b[:, f:],
                 seq, bm=FFN_ROW_BLOCK, bn=FFN_COL_BLOCK, sub=FFN_COL_SUB)
    return matmul(act, w_down, BF16, layer=layer, bm=512, bn=512, name="ffn_down")


def kernel(x, rwkv_mix, rwkv_w_rkv, rwkv_w_o, rwkv_w0, rwkv_w1, rwkv_w2, rwkv_a0, rwkv_a1, rwkv_a2, rwkv_g1, rwkv_g2, rwkv_k_k, rwkv_k_a, rwkv_r_k, rwkv_gn_g, rwkv_gn_b, rwkv_v0, rwkv_v1, rwkv_v2, attn_w_qkv, attn_w_o, attn_sinks, rel_bias, ffn_w_up, ffn_conv_w, ffn_conv_b, ffn_w_down, ln1_g, ln1_b, ln2_g, ln2_b):
    B, S, D = x.shape
    depth = ln1_g.shape[0]
    xf = x.reshape(B * S, D)
    xb = xf.astype(BF16)
    bias = bias_table(rel_bias, _bucket_map())
    w_rkv, w_ro = rwkv_w_rkv.astype(BF16), rwkv_w_o.astype(BF16)
    w_qkv, w_ao = attn_w_qkv.astype(BF16), attn_w_o.astype(BF16)
    w_up, w_down = ffn_w_up.astype(BF16), ffn_w_down.astype(BF16)
    first_rkv = None
    for i in range(depth):
        j = i // 2
        if i % 2 == 0:
            v_lora = None if j == 0 else (rwkv_v0[j - 1], rwkv_v1[j - 1], rwkv_v2[j - 1])
            h, rkv = rwkv_layer(xf, S, first_rkv, j, rwkv_mix[j], w_rkv, w_ro,
                                rwkv_w0[j], rwkv_w1[j], rwkv_w2[j], rwkv_a0[j],
                                rwkv_a1[j], rwkv_a2[j], rwkv_g1[j], rwkv_g2[j], rwkv_k_k[j],
                                rwkv_k_a[j], rwkv_r_k[j], rwkv_gn_g[j], rwkv_gn_b[j], v_lora)
            if v_lora is None:
                first_rkv = rkv
        else:
            h = attn_layer(xb, S, j, w_qkv, w_ao, attn_sinks[j], bias)
        xf, xb = ln_residual(xf, h, ln1_g[i], ln1_b[i])
        f = ffn_layer(xb, S, i, w_up, ffn_conv_w[i], ffn_conv_b[i], w_down)
        xf, xb = ln_residual(xf, f, ln2_g[i], ln2_b[i])
    return xf.reshape(B, S, D)
```

```python
import functools
import math

import jax
import jax.numpy as jnp
from jax import lax
from jax.experimental import pallas as pl
from jax.experimental.pallas import tpu as pltpu

F32 = jnp.float32
BF16 = jnp.bfloat16

LANES = 128
SUBLANES = 8
MXU_DIM = 256
VMEM_LIMIT_BYTES = 56 * 1024 * 1024

HEAD = 64
HEADS_PER_GROUP = MXU_DIM // HEAD
CHUNK = 64
STACK = HEADS_PER_GROUP * CHUNK

WINDOW = 128
BLOCK = 128
REL_BUCKETS = 32
REL_MAX_DIST = 128
CONV_W = 3
LN_EPS = 1e-5
GN_EPS = HEAD * 1e-5
DEPTH = 4
DEEPNORM_ALPHA = (2 * DEPTH) ** 0.25


def _params(*sem):
    return pltpu.CompilerParams(dimension_semantics=sem, vmem_limit_bytes=VMEM_LIMIT_BYTES)


def _block(n, target, unit):
    if n <= target:
        return n
    b = target - target % unit
    while n % b:
        b -= unit
    return b


def _sigmoid(x):
    return 1.0 / (1.0 + jnp.exp(-x))


def _dot(a, b):
    return jnp.dot(a, b, preferred_element_type=F32)


def _dot_nt(a, b):
    return lax.dot_general(a, b, (((1,), (1,)), ((), ())), preferred_element_type=F32)


def _dot_tn(a, b):
    return lax.dot_general(a, b, (((0,), (0,)), ((), ())), preferred_element_type=F32)


def _mm_body(x_ref, w_ref, o_ref):
    o_ref[...] = _dot(x_ref[...], w_ref[...]).astype(o_ref.dtype)


def matmul(x, w, out_dtype, *, layer, bm=1024, bn=512, name="matmul"):
    M, K = x.shape
    N = w.shape[-1]
    bm, bn = _block(M, bm, SUBLANES), _block(N, bn, LANES)
    assert M % bm == 0 and N % bn == 0
    return pl.pallas_call(
        _mm_body,
        grid=(M // bm, N // bn),
        in_specs=[pl.BlockSpec((bm, K), lambda i, j: (i, 0)),
                  pl.BlockSpec((None, K, bn), lambda i, j: (layer, 0, j))],
        out_specs=pl.BlockSpec((bm, bn), lambda i, j: (i, j)),
        out_shape=jax.ShapeDtypeStruct((M, N), out_dtype),
        compiler_params=_params("parallel", "parallel"),
        name=name,
    )(x, w)


def batched_matmul(x, w, out_dtype, *, nbatch, layer, bm=1024, bn=512, name="bmm"):
    M, K = x.shape[-2:]
    N = w.shape[-1]
    bm, bn = _block(M, bm, SUBLANES), _block(N, bn, LANES)
    assert M % bm == 0 and N % bn == 0
    return pl.pallas_call(
        _mm_body,
        grid=(nbatch, M // bm, N // bn),
        in_specs=[pl.BlockSpec((None, bm, K), lambda p, i, j: (p, i, 0)),
                  pl.BlockSpec((None, None, K, bn), lambda p, i, j: (layer, p, 0, j))],
        out_specs=pl.BlockSpec((None, bm, bn), lambda p, i, j: (p, i, j)),
        out_shape=jax.ShapeDtypeStruct((nbatch, M, N), out_dtype),
        compiler_params=_params("parallel", "parallel", "parallel"),
        name=name,
    )(x, w)


def _ln_body(x_ref, h_ref, g_ref, b_ref, of_ref, ob_ref):
    y = DEEPNORM_ALPHA * x_ref[...] + h_ref[...].astype(F32)
    mu = jnp.mean(y, axis=-1, keepdims=True)
    d = y - mu
    var = jnp.mean(d * d, axis=-1, keepdims=True)
    out = d * lax.rsqrt(var + LN_EPS) * g_ref[...] + b_ref[...]
    of_ref[...] = out
    ob_ref[...] = out.astype(BF16)


def ln_residual(x, h, g, b, *, br=256):
    T, D = x.shape
    br = min(br, T)
    assert T % br == 0
    row = pl.BlockSpec((br, D), lambda i: (i, 0))
    vec = pl.BlockSpec((1, D), lambda i: (0, 0))
    return pl.pallas_call(
        _ln_body,
        grid=(T // br,),
        in_specs=[row, row, vec, vec],
        out_specs=[row, row],
        out_shape=[jax.ShapeDtypeStruct((T, D), F32), jax.ShapeDtypeStruct((T, D), BF16)],
        compiler_params=_params("parallel"),
        name="ln_residual",
    )(x, h, g.reshape(1, D), b.reshape(1, D))


MIX_R, MIX_W, MIX_K, MIX_V, MIX_A, MIX_G = range(6)


def _mix_body(x_ref, xp_ref, mix_ref, *refs, blocks_per_seq, lora_rows, lora_acts):
    n = len(lora_rows)
    w_refs, o_ref, mid_refs, acc_refs = refs[:n], refs[n], refs[n + 1:2 * n + 1], refs[2 * n + 1:]
    i, j = pl.program_id(0), pl.program_id(1)
    x = x_ref[...]
    prev_last = xp_ref[SUBLANES - 1:SUBLANES, :]
    prev_last = jnp.where(i % blocks_per_seq == 0, 0.0, prev_last)
    row = lax.broadcasted_iota(jnp.int32, x.shape, 0)
    shifted = jnp.where(row == 0, prev_last, pltpu.roll(x, 1, 0))
    xx = shifted - x

    def mixed(r):
        return (x + xx * mix_ref[r:r + 1, :]).astype(BF16)

    for p, r in enumerate((MIX_R, MIX_K, MIX_V)):
        o_ref[p] = mixed(r)

    @pl.when(j == 0)
    def _():
        for acc in acc_refs:
            acc[...] = jnp.zeros_like(acc)

    for q in range(n):
        acc_refs[q][...] += _dot(mixed(lora_rows[q]), w_refs[q][...])

    @pl.when(j == pl.num_programs(1) - 1)
    def _():
        for q in range(n):
            acc = acc_refs[q][...]
            if lora_acts[q] == "tanh":
                acc = jnp.tanh(acc)
            elif lora_acts[q] == "sigmoid":
                acc = _sigmoid(acc)
            mid_refs[q][...] = acc.astype(BF16)


def token_shift_mix(x, mix, loras, seq, *, bt=512, bd=1024):
    T, D = x.shape
    bt, bd = min(bt, seq), min(bd, D)
    assert seq % bt == 0 and D % bd == 0 and bt % SUBLANES == 0
    halo = bt // SUBLANES
    widths = [w.shape[1] for _, w, _ in loras]
    outs = pl.pallas_call(
        functools.partial(_mix_body, blocks_per_seq=seq // bt, lora_rows=tuple(r for r, _, _ in loras),
                          lora_acts=tuple(a for _, _, a in loras)),
        grid=(T // bt, D // bd),
        in_specs=[pl.BlockSpec((bt, bd), lambda i, j: (i, j)),
                  pl.BlockSpec((SUBLANES, bd), lambda i, j: (jnp.maximum(i * halo - 1, 0), j)),
                  pl.BlockSpec((6, bd), lambda i, j: (0, j))]
                 + [pl.BlockSpec((bd, n), lambda i, j: (j, 0)) for n in widths],
        out_specs=[pl.BlockSpec((3, bt, bd), lambda i, j: (0, i, j))]
                  + [pl.BlockSpec((bt, n), lambda i, j: (i, 0)) for n in widths],
        out_shape=[jax.ShapeDtypeStruct((3, T, D), BF16)] + [jax.ShapeDtypeStruct((T, n), BF16) for n in widths],
        scratch_shapes=[pltpu.VMEM((bt, n), F32) for n in widths],
        compiler_params=_params("parallel", "arbitrary"),
        name="rwkv_mix",
    )(x, x, mix, *[w for _, w, _ in loras])
    return outs[0], outs[1:]


PREP_STAGE, HEAD_STAGE, TAIL_STAGE = "prep", "head", "tail"
HEAD_STAGES_BEFORE_NEXT_GROUP = 11


def _scan_body(*refs, n_chunks, has_vlora, groups):
    n_in = len(refs) - 9
    ins, o_ref = refs[:n_in], refs[n_in]
    s_ref, stash = refs[n_in + 1], refs[n_in + 2:]
    t = pl.program_id(2)
    last = pl.num_programs(2) - 1
    write_slot = t % 2
    read_slot = 1 - write_slot
    shared = ((3, 4, 5, 9) if has_vlora else (3, 4, 5)) + (n_in - 2,)
    lanes = [pl.ds(g * MXU_DIM, MXU_DIM) for g in range(groups)]

    def heads():
        return [_scan_head(*[ref if i in shared else ref.at[:, lanes[g]] for i, ref in enumerate(ins)],
                           *[ref.at[write_slot, g] for ref in stash], n_chunks=n_chunks, has_vlora=has_vlora)
                for g in range(groups)]

    def tail():
        return _scan_tail([s_ref.at[g] for g in range(groups)],
                          [[ref.at[read_slot, g] for ref in stash] for g in range(groups)],
                          [ins[-1].at[:, lanes[g]] for g in range(groups)],
                          [o_ref.at[:, lanes[g]] for g in range(groups)], n_chunks=n_chunks)

    def run(pending, tail_gen):
        active = []
        while pending or active or tail_gen is not None:
            if pending and all(done >= HEAD_STAGES_BEFORE_NEXT_GROUP for _, done in active):
                active.append([pending.pop(0), 0])
            for entry in list(active):
                try:
                    entry[1] += next(entry[0]) == HEAD_STAGE
                except StopIteration:
                    active.remove(entry)
            if tail_gen is not None and next(tail_gen, None) is None:
                tail_gen = None

    @pl.when(t == 0)
    def _():
        s_ref[...] = jnp.zeros_like(s_ref)
        run(heads(), None)

    @pl.when((t > 0) & (t < last))
    def _():
        run(heads(), tail())

    @pl.when(t == last)
    def _():
        run([], tail())


def _scan_masks():
    W = MXU_DIM
    head_mask = (lax.broadcasted_iota(jnp.int32, (STACK, W), 0) // CHUNK
                 == lax.broadcasted_iota(jnp.int32, (STACK, W), 1) // HEAD)
    seg_ones = (lax.broadcasted_iota(jnp.int32, (W, W), 0) // HEAD
                == lax.broadcasted_iota(jnp.int32, (W, W), 1) // HEAD).astype(BF16)
    return head_mask, seg_ones


def _scan_tail(s_refs, stashes, p_refs, o_refs, *, n_chunks):
    head_mask, seg_ones = _scan_masks()
    groups = range(len(s_refs))
    chunks = range(n_chunks)
    n_pairs = HEADS_PER_GROUP // 2
    states = [s_refs[g][...] for g in groups]
    ys = [[] for _ in groups]
    for c in chunks:
        rows = pl.ds(c * CHUNK, CHUNK)
        for g in groups:
            g_v, n_v, re_v, y0_v, dec_v = stashes[g][:5]
            sb = states[g].astype(BF16)
            s_big = jnp.where(head_mask, jnp.concatenate([sb] * n_pairs, axis=0), jnp.zeros((), BF16))
            ys[g].append(_dot_nt(re_v[rows, :], s_big) + y0_v[rows, :])
            states[g] = states[g] * dec_v[c, 0:1, :] + _dot(sb, g_v[c]) + n_v[c]
        yield TAIL_STAGE
    for g in groups:
        s_refs[g][...] = states[g]
    for g in groups:
        bonus_v, gate_v = stashes[g][5:]
        gn_g = p_refs[g][5:6, :]
        gn_b = p_refs[g][6:7, :]
        y = jnp.concatenate(ys[g], axis=0)
        mu = _dot(y.astype(BF16), seg_ones) * (1.0 / HEAD)
        d = y - mu
        var = _dot((d * d).astype(BF16), seg_ones) * (1.0 / HEAD)
        yield TAIL_STAGE
        yn = d * lax.rsqrt(var + GN_EPS) * gn_g + gn_b
        o_refs[g][...] = ((yn + bonus_v[...]) * gate_v[...]).astype(o_refs[g].dtype)
        yield TAIL_STAGE


def _scan_head(*refs, n_chunks, has_vlora):
    stash_refs = refs[-7:]
    if has_vlora:
        (r_ref, k_ref, v_ref, mw_ref, ma_ref, mg_ref, w2_ref, a2_ref, g2_ref,
         mv_ref, v2_ref, vf_ref, lvl_ref, p_ref) = refs[:-7]
    else:
        r_ref, k_ref, v_ref, mw_ref, ma_ref, mg_ref, w2_ref, a2_ref, g2_ref, lvl_ref, p_ref = refs[:-7]
    g_s, n_s, re_s, y0_s, dec_s, bonus_s, gate_s = stash_refs

    W = MXU_DIM
    head_mask, seg_ones = _scan_masks()
    rr = lax.broadcasted_iota(jnp.int32, (STACK, STACK), 0)
    cc = lax.broadcasted_iota(jnp.int32, (STACK, STACK), 1)
    same_head = (rr // CHUNK) == (cc // CHUNK)
    strict_lower = same_head & ((rr % CHUNK) > (cc % CHUNK))
    eye = (rr == cc).astype(F32)
    half_mask = ((lax.broadcasted_iota(jnp.int32, (STACK, LANES), 0) // CHUNK) % 2
                 == lax.broadcasted_iota(jnp.int32, (STACK, LANES), 1) // HEAD)
    natural_lower = (lax.broadcasted_iota(jnp.int32, (CHUNK, 4 * CHUNK), 0)
                     >= lax.broadcasted_iota(jnp.int32, (CHUNK, 4 * CHUNK), 1) % CHUNK)
    bt = n_chunks * CHUNK
    tr = lax.broadcasted_iota(jnp.int32, (bt, bt), 0)
    tc = lax.broadcasted_iota(jnp.int32, (bt, bt), 1)
    tri = ((tr // CHUNK == tc // CHUNK) & (tr >= tc)).astype(BF16)

    w0 = p_ref[0:1, :]
    a0 = p_ref[1:2, :]
    k_k = p_ref[2:3, :]
    k_a = p_ref[3:4, :]
    r_k = p_ref[4:5, :]
    v0 = p_ref[7:8, :]

    chunks = range(n_chunks)

    r = r_ref[...]
    k = k_ref[...]
    v = v_ref[...]
    z = w0 + _dot(mw_ref[...], w2_ref[...])
    w_log = -(jnp.maximum(-z, 0.0) + jnp.log(1.0 + jnp.exp(-jnp.abs(z)))) - 0.5
    log_decay = -jnp.exp(w_log)
    yield PREP_STAGE
    a = _sigmoid(a0 + _dot(ma_ref[...], a2_ref[...]))
    if has_vlora:
        v = v + (vf_ref[...] - v) * _sigmoid(v0 + _dot(mv_ref[...], v2_ref[...]))
    yield PREP_STAGE
    kk = k * k_k
    norm = jnp.sqrt(_dot((kk * kk).astype(BF16), seg_ones))
    kk = kk / jnp.maximum(norm, 1e-12)
    k = k * (1.0 + (a - 1.0) * k_a)
    yield PREP_STAGE

    cum = _split_dot_left(tri, log_decay)
    cum_last = jnp.concatenate(
        [jnp.broadcast_to(cum[(c + 1) * CHUNK - 1:(c + 1) * CHUNK, :], (CHUNK, W)) for c in chunks], axis=0)
    e_pos = jnp.exp(cum)
    e_neg = jnp.exp(-cum)
    yield PREP_STAGE
    e_prev = jnp.exp(cum - log_decay)
    e_tail = jnp.exp(cum_last - cum)
    kka = kk * a
    yield PREP_STAGE
    a_e = -kk * e_prev
    r_e = r * e_pos
    b_e = kka * e_neg
    yield PREP_STAGE
    k_e = k * e_neg
    b_end = kka * e_tail
    k_end = k * e_tail

    def rows_of(x, c):
        return x[c * CHUNK:(c + 1) * CHUNK]

    def stack(x, c):
        return jnp.where(head_mask, jnp.concatenate([rows_of(x, c)] * HEADS_PER_GROUP, axis=0), 0.0)

    def stack_pair(x, c):
        xc = rows_of(x, c)
        parts = [xc[:, (h // 2) * LANES:(h // 2 + 1) * LANES] for h in range(HEADS_PER_GROUP)]
        return jnp.where(half_mask, jnp.concatenate(parts, axis=0), 0.0).astype(BF16)

    n_pairs = HEADS_PER_GROUP // 2
    pair_rows = [slice(q * 2 * CHUNK, (q + 1) * 2 * CHUNK) for q in range(n_pairs)]
    pair_lanes = [slice(q * LANES, (q + 1) * LANES) for q in range(n_pairs)]

    yield HEAD_STAGE

    a_p = [stack_pair(a_e, c) for c in chunks]
    b_p = [stack_pair(b_e, c) for c in chunks]
    k_p = [stack_pair(k_e, c) for c in chunks]
    v_p = [stack_pair(v, c) for c in chunks]
    p_a = [_dot_nt(a_p[c], jnp.concatenate([b_p[c], k_p[c]], axis=0)) for c in chunks]
    l_ab = [jnp.where(strict_lower, p_a[c][:, :STACK], 0.0) for c in chunks]
    l_ak = [jnp.where(strict_lower, p_a[c][:, STACK:], 0.0).astype(BF16) for c in chunks]
    yield HEAD_STAGE

    l_r = [[jnp.where(natural_lower,
                      _dot_nt(rows_of(r_e, c)[:, pair_lanes[q]].astype(BF16),
                              jnp.concatenate([b_p[c][pair_rows[q]], k_p[c][pair_rows[q]]], axis=0)),
                      0.0).astype(BF16) for q in range(n_pairs)] for c in chunks]
    yield HEAD_STAGE

    l_b = [l_ab[c].astype(BF16) for c in chunks]
    inv = [eye + (l_b[c] * lvl_ref[0]).astype(F32) for c in chunks]
    for level in range(1, int(math.log2(CHUNK))):
        half = [_dot(l_b[c] * lvl_ref[level], inv[c].astype(BF16)).astype(BF16) for c in chunks]
        yield HEAD_STAGE
        inv = [inv[c] + _dot(inv[c].astype(BF16), half[c]) for c in chunks]
        yield HEAD_STAGE

    x = [jnp.concatenate([a_p[c], _dot(l_ak[c], v_p[c]).astype(BF16)], axis=1) for c in chunks]
    yield HEAD_STAGE
    wu = [_dot(inv[c].astype(BF16), x[c]).astype(BF16) for c in chunks]
    yield HEAD_STAGE

    zero_pad = jnp.zeros((2 * CHUNK, LANES), BF16)
    for c in chunks:
        gn = _dot_tn(wu[c], stack(b_end, c).astype(BF16))
        n_s[c] = gn[LANES:] + _dot_tn(v_p[c], stack(k_end, c).astype(BF16))
        g_s[c] = jnp.where(head_mask, jnp.concatenate([gn[:LANES]] * n_pairs, axis=0), 0.0).astype(BF16)
        dec_s[c] = jnp.broadcast_to(jnp.exp(cum[(c + 1) * CHUNK - 1:(c + 1) * CHUNK, :]), (SUBLANES, W))
        yield HEAD_STAGE
        re_q, y0_q = [], []
        for q in range(n_pairs):
            rhs = jnp.concatenate([wu[c][pair_rows[q]],
                                   jnp.concatenate([zero_pad, v_p[c][pair_rows[q]]], axis=1)], axis=0)
            ry = _dot(l_r[c][q], rhs)
            re_q.append(rows_of(r_e, c)[:, pair_lanes[q]] + ry[:, :LANES])
            y0_q.append(ry[:, LANES:])
        re_s[pl.ds(c * CHUNK, CHUNK), :] = jnp.concatenate(re_q, axis=1).astype(BF16)
        y0_s[pl.ds(c * CHUNK, CHUNK), :] = jnp.concatenate(y0_q, axis=1)
        yield HEAD_STAGE
    bonus_s[...] = _dot((r * k * r_k).astype(BF16), seg_ones) * v
    gate_s[...] = _dot(mg_ref[...], g2_ref[...])
    yield HEAD_STAGE


def _doubling_level_masks():
    idx = jnp.arange(STACK)
    head, time = idx // CHUNK, idx % CHUNK
    same_head = head[:, None] == head[None, :]
    masks = []
    for k in range(int(math.log2(CHUNK))):
        m = 2 ** k
        same_pair = (time[:, None] // (2 * m)) == (time[None, :] // (2 * m))
        masks.append(same_head & same_pair & ((time[:, None] // m) % 2 == 1) & ((time[None, :] // m) % 2 == 0))
    return jnp.stack(masks).astype(BF16)


def _split_dot_left(e, x):
    hi = x.astype(BF16)
    lo = (x - hi.astype(F32)).astype(BF16)
    return _dot(e, hi) + _dot(e, lo)


def rwkv_scan(rkv, mids, ups, v_mix, chan, seq, *, bt=256, groups=4):
    _, T, D = rkv.shape
    groups = min(groups, D // MXU_DIM)
    W = groups * MXU_DIM
    bt = min(bt, seq)
    assert D % W == 0 and seq % bt == 0 and bt % CHUNK == 0
    has_vlora = v_mix is not None
    nb = seq // bt

    def row_in(b, t):
        return b * nb + jnp.minimum(t, nb - 1)

    def stacked(p):
        return pl.BlockSpec((None, bt, W), lambda b, gidx, t: (p, row_in(b, t), gidx))

    def mid(a):
        return pl.BlockSpec((bt, a.shape[1]), lambda b, gidx, t: (row_in(b, t), 0))

    def up(a):
        return pl.BlockSpec((a.shape[0], W), lambda b, gidx, t: (0, gidx))

    ins = [rkv, rkv, rkv, *mids, *ups]
    specs = [stacked(0), stacked(1), stacked(2)] + [mid(a) for a in mids] + [up(a) for a in ups]
    if has_vlora:
        mv, v2, first_rkv = v_mix
        ins += [mv, v2, first_rkv]
        specs += [mid(mv), up(v2), stacked(2)]
    levels = _doubling_level_masks()
    ins += [levels, chan]
    specs += [pl.BlockSpec(levels.shape, lambda b, gidx, t: (0, 0, 0)),
              pl.BlockSpec((SUBLANES, W), lambda b, gidx, t: (0, gidx))]
    nc = bt // CHUNK
    G = MXU_DIM
    stash = [pltpu.VMEM((2, groups, nc, G, G), BF16),
             pltpu.VMEM((2, groups, nc, 2 * HEAD, G), F32),
             pltpu.VMEM((2, groups, bt, G), BF16),
             pltpu.VMEM((2, groups, bt, G), F32),
             pltpu.VMEM((2, groups, nc, SUBLANES, G), F32),
             pltpu.VMEM((2, groups, bt, G), F32),
             pltpu.VMEM((2, groups, bt, G), F32)]
    return pl.pallas_call(
        functools.partial(_scan_body, n_chunks=nc, has_vlora=has_vlora, groups=groups),
        grid=(T // seq, D // W, nb + 1),
        in_specs=specs,
        out_specs=pl.BlockSpec((bt, W), lambda b, gidx, t: (b * nb + jnp.maximum(t - 1, 0), gidx)),
        out_shape=jax.ShapeDtypeStruct((T, D), BF16),
        scratch_shapes=[pltpu.VMEM((groups, 2 * HEAD, G), F32)] + stash,
        compiler_params=_params("parallel", "parallel", "arbitrary"),
        name="rwkv_scan",
    )(*ins)


def _bias_body(bmap_ref, rb_ref, o_ref):
    h = pl.program_id(0)
    bmap = bmap_ref[...]
    acc = jnp.full(bmap.shape, -jnp.inf, F32)
    for b in range(REL_BUCKETS):
        acc = jnp.where(bmap == b, rb_ref[b, h], acc)
    o_ref[0, 0] = acc
    kj = lax.broadcasted_iota(jnp.int32, bmap.shape, 1)
    o_ref[1, 0] = jnp.where(kj >= BLOCK, acc, -jnp.inf)


def bias_table(rel_bias, bmap):
    H = rel_bias.shape[1]
    return pl.pallas_call(
        _bias_body,
        grid=(H,),
        in_specs=[pl.BlockSpec((BLOCK, 2 * BLOCK), lambda h: (0, 0)),
                  pl.BlockSpec(memory_space=pltpu.SMEM)],
        out_specs=pl.BlockSpec((2, 1, BLOCK, 2 * BLOCK), lambda h: (0, h, 0, 0)),
        out_shape=jax.ShapeDtypeStruct((2, H, BLOCK, 2 * BLOCK), F32),
        compiler_params=_params("parallel"),
        name="rel_bias_table",
    )(bmap, rel_bias)


def _attn_body(q_ref, kp_ref, kc_ref, vp_ref, vc_ref, bias_ref, sink_ref, o_ref, *, group, n_kv):
    low = lax.broadcasted_iota(jnp.int32, (BLOCK, LANES), 1) < HEAD
    scale = HEAD ** -0.5

    for kv_pair in range(n_kv // 2):
        lanes = pl.ds(kv_pair * LANES, LANES)
        kband = jnp.concatenate([kp_ref[:, lanes], kc_ref[:, lanes]], axis=0)
        vband = jnp.concatenate([vp_ref[:, lanes], vc_ref[:, lanes]], axis=0)
        kswap = pltpu.roll(kband, HEAD, 1)
        vswap = pltpu.roll(vband, HEAD, 1)
        for e in range(2):
            kvh = 2 * kv_pair + e
            k_at = (kband, kswap) if e == 0 else (kswap, kband)
            v_at = (vband, vswap) if e == 0 else (vswap, vband)
            for pair in range(group // 2):
                q_lanes = pl.ds((kvh * group // 2 + pair) * LANES, LANES)
                qp = q_ref[:, q_lanes] * scale
                halves = []
                for half in range(2):
                    h = kvh * group + 2 * pair + half
                    in_half = low if half == 0 else jnp.logical_not(low)
                    qm = jnp.where(in_half, qp, jnp.zeros_like(qp))
                    s = _dot_nt(qm, k_at[half]) + bias_ref[0, h]
                    sink = sink_ref[h]
                    m = jnp.maximum(jnp.max(s, axis=-1, keepdims=True), sink)
                    p = jnp.exp(s - m)
                    denom = jnp.sum(p, axis=-1, keepdims=True) + jnp.exp(sink - m)
                    halves.append(_dot(p.astype(BF16), v_at[half]) / denom)
                o_ref[:, q_lanes] = jnp.where(low, halves[0], halves[1]).astype(o_ref.dtype)


def swa_attention(qkv, bias, sinks, seq, n_q_heads, n_kv_heads):
    T = qkv.shape[0]
    group = n_q_heads // n_kv_heads
    assert group % 2 == 0 and n_kv_heads % 2 == 0 and seq % BLOCK == 0
    nb = seq // BLOCK
    qw, kw = n_q_heads * HEAD, n_kv_heads * HEAD
    assert qw % kw == 0 and kw % LANES == 0
    k_col, v_col = qw // kw, qw // kw + 1

    def cur(col):
        return pl.BlockSpec((BLOCK, kw), lambda b, n: (b * nb + n, col))

    def prev(col):
        return pl.BlockSpec((BLOCK, kw), lambda b, n: (b * nb + jnp.maximum(n - 1, 0), col))

    return pl.pallas_call(
        functools.partial(_attn_body, group=group, n_kv=n_kv_heads),
        grid=(T // seq, nb),
        in_specs=[pl.BlockSpec((BLOCK, qw), lambda b, n: (b * nb + n, 0)),
                  prev(k_col), cur(k_col), prev(v_col), cur(v_col),
                  pl.BlockSpec((1, n_q_heads, BLOCK, 2 * BLOCK), lambda b, n: (jnp.where(n == 0, 1, 0), 0, 0, 0)),
                  pl.BlockSpec(memory_space=pltpu.SMEM)],
        out_specs=pl.BlockSpec((BLOCK, qw), lambda b, n: (b * nb + n, 0)),
        out_shape=jax.ShapeDtypeStruct((T, qw), BF16),
        compiler_params=_params("parallel", "parallel"),
        name="swa_attention",
    )(qkv, qkv, qkv, qkv, qkv, bias, sinks)


def _bucket_map():
    qi = jnp.arange(BLOCK)[:, None]
    kj = jnp.arange(2 * BLOCK)[None, :]
    signed = qi + BLOCK - kj
    dist = jnp.maximum(signed, 0)
    max_exact = REL_BUCKETS // 2
    d_f = jnp.maximum(dist, max_exact).astype(F32)
    large = max_exact + (jnp.log(d_f / max_exact) / math.log(REL_MAX_DIST / max_exact)
                         * (REL_BUCKETS - max_exact)).astype(jnp.int32)
    large = jnp.minimum(large, REL_BUCKETS - 1)
    bucket = jnp.where(dist < max_exact, dist, large)
    return jnp.where((signed >= 0) & (signed < WINDOW), bucket, -1).astype(jnp.int32)


def _ffn_up_body(x_ref, *refs, blocks_per_seq, sub, rsub, n_sub):
    wg_refs, wu_refs = refs[:n_sub], refs[n_sub:2 * n_sub]
    cg_ref, cu_ref, bg_ref, bu_ref, o_ref, tail_g_ref, tail_u_ref, raw_g_ref, raw_u_ref = refs[2 * n_sub:]
    first = pl.program_id(0) % blocks_per_seq == 0
    j = pl.program_id(1)
    bm = x_ref.shape[0]
    H = SUBLANES

    @pl.when((pl.program_id(0) == 0) & (j == 0))
    def _():
        tail_g_ref[...] = jnp.zeros_like(tail_g_ref)
        tail_u_ref[...] = jnp.zeros_like(tail_u_ref)

    def matmul_tile(slot, s, rb, halo_g, halo_u):
        xr = x_ref[pl.ds(rb * rsub, rsub), :]
        raw_g_ref[slot, 0:H, :] = halo_g
        raw_u_ref[slot, 0:H, :] = halo_u
        raw_g_ref[slot, H:, :] = _dot(xr, wg_refs[s][...])
        raw_u_ref[slot, H:, :] = _dot(xr, wu_refs[s][...])
        return raw_g_ref[slot, rsub:, :], raw_u_ref[slot, rsub:, :]

    def conv(raw_ref, slot, c_ref, b_ref, cols):
        h = raw_ref[slot, H:, :]
        h1 = raw_ref[slot, H - 1:H - 1 + rsub, :]
        h2 = raw_ref[slot, H - 2:H - 2 + rsub, :]
        return c_ref[0:1, cols] * h2 + c_ref[1:2, cols] * h1 + c_ref[2:3, cols] * h + b_ref[:, cols]

    def finish_tile(slot, s, rb):
        cols = pl.ds(s * sub, sub)
        gate = conv(raw_g_ref, slot, cg_ref, bg_ref, cols)
        up = conv(raw_u_ref, slot, cu_ref, bu_ref, cols)
        o_ref[pl.ds(rb * rsub, rsub), cols] = (gate * _sigmoid(gate) * up).astype(o_ref.dtype)

    n_rows = bm // rsub
    prev = None
    t = 0
    for s in range(n_sub):
        cols = pl.ds(s * sub, sub)
        halo_g = jnp.where(first, 0.0, tail_g_ref[j, :, cols])
        halo_u = jnp.where(first, 0.0, tail_u_ref[j, :, cols])
        for rb in range(n_rows):
            halo_g, halo_u = matmul_tile(t % 2, s, rb, halo_g, halo_u)
            if prev is not None:
                finish_tile(*prev)
            prev = (t % 2, s, rb)
            t += 1
        tail_g_ref[j, :, cols] = halo_g
        tail_u_ref[j, :, cols] = halo_u
    finish_tile(*prev)


def ffn_up(x, w2, layer, cg, cu, bg, bu, seq, *, bm=1024, bn=512, sub=256, rsub=256):
    T, D = x.shape
    F = w2.shape[2] // 2
    bm, bn = min(bm, seq), min(bn, F)
    sub, rsub = min(sub, bn), min(rsub, bm)
    assert seq % bm == 0 and bn % sub == 0 and F % sub == 0 and bm % rsub == 0 and rsub % SUBLANES == 0
    nf = pl.cdiv(F, bn)
    n_sub = bn // sub
    last = 2 * F // sub - 1

    def wspec(s, half):
        return pl.BlockSpec((None, D, sub),
                            lambda i, j: (layer, 0, jnp.minimum(half * (F // sub) + j * n_sub + s, last)))

    col = lambda rows: pl.BlockSpec((rows, bn), lambda i, j: (0, j))
    return pl.pallas_call(
        functools.partial(_ffn_up_body, blocks_per_seq=seq // bm, sub=sub, rsub=rsub, n_sub=n_sub),
        grid=(T // bm, nf),
        in_specs=[pl.BlockSpec((bm, D), lambda i, j: (i, 0))]
                 + [wspec(s, 0) for s in range(n_sub)] + [wspec(s, 1) for s in range(n_sub)]
                 + [col(CONV_W), col(CONV_W), col(1), col(1)],
        out_specs=pl.BlockSpec((bm, bn), lambda i, j: (i, j)),
        out_shape=jax.ShapeDtypeStruct((T, F), BF16),
        scratch_shapes=[pltpu.VMEM((nf, SUBLANES, bn), F32), pltpu.VMEM((nf, SUBLANES, bn), F32),
                        pltpu.VMEM((2, SUBLANES + rsub, sub), F32), pltpu.VMEM((2, SUBLANES + rsub, sub), F32)],
        compiler_params=_params("arbitrary", "arbitrary"),
        name="ffn_up",
    )(x, *([w2] * (2 * n_sub)), cg, cu, bg, bu)


def _pad_to(x, axis, mult):
    pad = (-x.shape[axis]) % mult
    if pad == 0:
        return x
    widths = [(0, 0)] * x.ndim
    widths[axis] = (0, pad)
    return jnp.pad(x, widths)


WIDE_COL_BLOCK = 1024
SCAN_TIME_BLOCK = 4 * CHUNK


def _lora_weights(w_down, w_up):
    return _pad_to(w_down, 1, LANES).astype(BF16), _pad_to(w_up, 0, LANES).astype(BF16)


def rwkv_layer(x, seq, first_rkv, layer, mix, w_rkv, w_o, w0, w1, w2, a0, a1, a2, g1, g2,
               k_k, k_a, r_k, gn_g, gn_b, v_lora):
    w1, w2 = _lora_weights(w1, w2)
    a1, a2 = _lora_weights(a1, a2)
    g1, g2 = _lora_weights(g1, g2)
    loras = [(MIX_W, w1, "tanh"), (MIX_A, a1, None), (MIX_G, g1, "sigmoid")]
    if v_lora is None:
        v0 = jnp.zeros_like(w0)
    else:
        v0, v1, v2 = v_lora
        v1, v2 = _lora_weights(v1, v2)
        loras.append((MIX_V, v1, None))
    xs, mids = token_shift_mix(x, mix, loras, seq)
    rkv = batched_matmul(xs, w_rkv, F32, nbatch=3, layer=layer, bn=WIDE_COL_BLOCK, name="rwkv_rkv")
    v_mix = None if v_lora is None else (mids[3], v2, first_rkv)
    chan = jnp.stack([w0, a0, k_k, k_a, r_k, gn_g, gn_b, v0])
    y = rwkv_scan(rkv, tuple(mids[:3]), (w2, a2, g2), v_mix, chan, seq, bt=SCAN_TIME_BLOCK)
    return matmul(y, w_o, BF16, layer=layer, bn=WIDE_COL_BLOCK, name="rwkv_out"), rkv


def attn_layer(xb, seq, layer, w_qkv, w_o, sinks, bias):
    n_q = w_o.shape[1] // HEAD
    n_kv = (w_qkv.shape[2] // HEAD - n_q) // 2
    qkv = matmul(xb, w_qkv, BF16, layer=layer, bn=WIDE_COL_BLOCK, name="attn_qkv")
    o = swa_attention(qkv, bias, sinks, seq, n_q, n_kv)
    return matmul(o, w_o, BF16, layer=layer, bn=WIDE_COL_BLOCK, name="attn_out")


FFN_COL_BLOCK = 512
FFN_ROW_BLOCK = 1024
FFN_COL_SUB = MXU_DIM


def ffn_layer(xb, seq, layer, w_up, conv_w, conv_b, w_down):
    f = w_down.shape[1]
    conv_b = conv_b.reshape(1, -1)
    act = ffn_up(xb, w_up, layer, conv_w[:, :f], conv_w[:, f:], conv_b[:, :f], conv_b[:, f:],
                 seq, bm=FFN_ROW_BLOCK, bn=FFN_COL_BLOCK, sub=FFN_COL_SUB)
    return matmul(act, w_down, BF16, layer=layer, bm=512, bn=512, name="ffn_down")


def kernel(x, rwkv_mix, rwkv_w_rkv, rwkv_w_o, rwkv_w0, rwkv_w1, rwkv_w2, rwkv_a0, rwkv_a1, rwkv_a2, rwkv_g1, rwkv_g2, rwkv_k_k, rwkv_k_a, rwkv_r_k, rwkv_gn_g, rwkv_gn_b, rwkv_v0, rwkv_v1, rwkv_v2, attn_w_qkv, attn_w_o, attn_sinks, rel_bias, ffn_w_up, ffn_conv_w, ffn_conv_b, ffn_w_down, ln1_g, ln1_b, ln2_g, ln2_b):
    B, S, D = x.shape
    depth = ln1_g.shape[0]
    xf = x.reshape(B * S, D)
    xb = xf.astype(BF16)
    bias = bias_table(rel_bias, _bucket_map())
    w_rkv, w_ro = rwkv_w_rkv.astype(BF16), rwkv_w_o.astype(BF16)
    w_qkv, w_ao = attn_w_qkv.astype(BF16), attn_w_o.astype(BF16)
    w_up, w_down = ffn_w_up.astype(BF16), ffn_w_down.astype(BF16)
    first_rkv = None
    for i in range(depth):
        j = i // 2
        if i % 2 == 0:
            v_lora = None if j == 0 else (rwkv_v0[j - 1], rwkv_v1[j - 1], rwkv_v2[j - 1])
            h, rkv = rwkv_layer(xf, S, first_rkv, j, rwkv_mix[j], w_rkv, w_ro,
                                rwkv_w0[j], rwkv_w1[j], rwkv_w2[j], rwkv_a0[j],
                                rwkv_a1[j], rwkv_a2[j], rwkv_g1[j], rwkv_g2[j], rwkv_k_k[j],
                                rwkv_k_a[j], rwkv_r_k[j], rwkv_gn_g[j], rwkv_gn_b[j], v_lora)
            if v_lora is None:
                first_rkv = rkv
        else:
            h = attn_layer(xb, S, j, w_qkv, w_ao, attn_sinks[j], bias)
        xf, xb = ln_residual(xf, h, ln1_g[i], ln1_b[i])
        f = ffn_layer(xb, S, i, w_up, ffn_conv_w[i], ffn_conv_b[i], w_down)
        xf, xb = ln_residual(xf, f, ln2_g[i], ln2_b[i])
    return xf.reshape(B, S, D)
```

```python
import functools
import math

import jax
import jax.numpy as jnp
from jax import lax
from jax.experimental import pallas as pl
from jax.experimental.pallas import tpu as pltpu

F32 = jnp.float32
BF16 = jnp.bfloat16

LANES = 128
SUBLANES = 8
MXU_DIM = 256
VMEM_LIMIT_BYTES = 56 * 1024 * 1024

HEAD = 64
HEADS_PER_GROUP = MXU_DIM // HEAD
CHUNK = 64
STACK = HEADS_PER_GROUP * CHUNK

WINDOW = 128
BLOCK = 128
REL_BUCKETS = 32
REL_MAX_DIST = 128
CONV_W = 3
LN_EPS = 1e-5
GN_EPS = HEAD * 1e-5
DEPTH = 4
DEEPNORM_ALPHA = (2 * DEPTH) ** 0.25


def _params(*sem):
    return pltpu.CompilerParams(dimension_semantics=sem, vmem_limit_bytes=VMEM_LIMIT_BYTES)


def _block(n, target, unit):
    if n <= target:
        return n
    b = target - target % unit
    while n % b:
        b -= unit
    return b


def _sigmoid(x):
    return 1.0 / (1.0 + jnp.exp(-x))


def _dot(a, b):
    return jnp.dot(a, b, preferred_element_type=F32)


def _dot_nt(a, b):
    return lax.dot_general(a, b, (((1,), (1,)), ((), ())), preferred_element_type=F32)


def _dot_tn(a, b):
    return lax.dot_general(a, b, (((0,), (0,)), ((), ())), preferred_element_type=F32)


def _mm_body(x_ref, w_ref, o_ref):
    o_ref[...] = _dot(x_ref[...], w_ref[...]).astype(o_ref.dtype)


def matmul(x, w, out_dtype, *, layer, bm=1024, bn=512, name="matmul"):
    M, K = x.shape
    N = w.shape[-1]
    bm, bn = _block(M, bm, SUBLANES), _block(N, bn, LANES)
    assert M % bm == 0 and N % bn == 0
    return pl.pallas_call(
        _mm_body,
        grid=(M // bm, N // bn),
        in_specs=[pl.BlockSpec((bm, K), lambda i, j: (i, 0)),
                  pl.BlockSpec((None, K, bn), lambda i, j: (layer, 0, j))],
        out_specs=pl.BlockSpec((bm, bn), lambda i, j: (i, j)),
        out_shape=jax.ShapeDtypeStruct((M, N), out_dtype),
        compiler_params=_params("parallel", "parallel"),
        name=name,
    )(x, w)


def batched_matmul(x, w, out_dtype, *, nbatch, layer, bm=1024, bn=512, name="bmm"):
    M, K = x.shape[-2:]
    N = w.shape[-1]
    bm, bn = _block(M, bm, SUBLANES), _block(N, bn, LANES)
    assert M % bm == 0 and N % bn == 0
    return pl.pallas_call(
        _mm_body,
        grid=(nbatch, M // bm, N // bn),
        in_specs=[pl.BlockSpec((None, bm, K), lambda p, i, j: (p, i, 0)),
                  pl.BlockSpec((None, None, K, bn), lambda p, i, j: (layer, p, 0, j))],
        out_specs=pl.BlockSpec((None, bm, bn), lambda p, i, j: (p, i, j)),
        out_shape=jax.ShapeDtypeStruct((nbatch, M, N), out_dtype),
        compiler_params=_params("parallel", "parallel", "parallel"),
        name=name,
    )(x, w)


def _ln_body(x_ref, h_ref, g_ref, b_ref, of_ref, ob_ref):
    y = DEEPNORM_ALPHA * x_ref[...] + h_ref[...].astype(F32)
    mu = jnp.mean(y, axis=-1, keepdims=True)
    d = y - mu
    var = jnp.mean(d * d, axis=-1, keepdims=True)
    out = d * lax.rsqrt(var + LN_EPS) * g_ref[...] + b_ref[...]
    of_ref[...] = out
    ob_ref[...] = out.astype(BF16)


def ln_residual(x, h, g, b, *, br=256):
    T, D = x.shape
    br = min(br, T)
    assert T % br == 0
    row = pl.BlockSpec((br, D), lambda i: (i, 0))
    vec = pl.BlockSpec((1, D), lambda i: (0, 0))
    return pl.pallas_call(
        _ln_body,
        grid=(T // br,),
        in_specs=[row, row, vec, vec],
        out_specs=[row, row],
        out_shape=[jax.ShapeDtypeStruct((T, D), F32), jax.ShapeDtypeStruct((T, D), BF16)],
        compiler_params=_params("parallel"),
        name="ln_residual",
    )(x, h, g.reshape(1, D), b.reshape(1, D))


MIX_R, MIX_W, MIX_K, MIX_V, MIX_A, MIX_G = range(6)


def _mix_body(x_ref, xp_ref, mix_ref, *refs, blocks_per_seq, lora_rows, lora_acts):
    n = len(lora_rows)
    w_refs, o_ref, mid_refs, acc_refs = refs[:n], refs[n], refs[n + 1:2 * n + 1], refs[2 * n + 1:]
    i, j = pl.program_id(0), pl.program_id(1)
    x = x_ref[...]
    prev_last = xp_ref[SUBLANES - 1:SUBLANES, :]
    prev_last = jnp.where(i % blocks_per_seq == 0, 0.0, prev_last)
    row = lax.broadcasted_iota(jnp.int32, x.shape, 0)
    shifted = jnp.where(row == 0, prev_last, pltpu.roll(x, 1, 0))
    xx = shifted - x

    def mixed(r):
        return (x + xx * mix_ref[r:r + 1, :]).astype(BF16)

    for p, r in enumerate((MIX_R, MIX_K, MIX_V)):
        o_ref[p] = mixed(r)

    @pl.when(j == 0)
    def _():
        for acc in acc_refs:
            acc[...] = jnp.zeros_like(acc)

    for q in range(n):
        acc_refs[q][...] += _dot(mixed(lora_rows[q]), w_refs[q][...])

    @pl.when(j == pl.num_programs(1) - 1)
    def _():
        for q in range(n):
            acc = acc_refs[q][...]
            if lora_acts[q] == "tanh":
                acc = jnp.tanh(acc)
            elif lora_acts[q] == "sigmoid":
                acc = _sigmoid(acc)
            mid_refs[q][...] = acc.astype(BF16)


def token_shift_mix(x, mix, loras, seq, *, bt=512, bd=1024):
    T, D = x.shape
    bt, bd = min(bt, seq), min(bd, D)
    assert seq % bt == 0 and D % bd == 0 and bt % SUBLANES == 0
    halo = bt // SUBLANES
    widths = [w.shape[1] for _, w, _ in loras]
    outs = pl.pallas_call(
        functools.partial(_mix_body, blocks_per_seq=seq // bt, lora_rows=tuple(r for r, _, _ in loras),
                          lora_acts=tuple(a for _, _, a in loras)),
        grid=(T // bt, D // bd),
        in_specs=[pl.BlockSpec((bt, bd), lambda i, j: (i, j)),
                  pl.BlockSpec((SUBLANES, bd), lambda i, j: (jnp.maximum(i * halo - 1, 0), j)),
                  pl.BlockSpec((6, bd), lambda i, j: (0, j))]
                 + [pl.BlockSpec((bd, n), lambda i, j: (j, 0)) for n in widths],
        out_specs=[pl.BlockSpec((3, bt, bd), lambda i, j: (0, i, j))]
                  + [pl.BlockSpec((bt, n), lambda i, j: (i, 0)) for n in widths],
        out_shape=[jax.ShapeDtypeStruct((3, T, D), BF16)] + [jax.ShapeDtypeStruct((T, n), BF16) for n in widths],
        scratch_shapes=[pltpu.VMEM((bt, n), F32) for n in widths],
        compiler_params=_params("parallel", "arbitrary"),
        name="rwkv_mix",
    )(x, x, mix, *[w for _, w, _ in loras])
    return outs[0], outs[1:]


PREP_STAGE, HEAD_STAGE, TAIL_STAGE = "prep", "head", "tail"
HEAD_STAGES_BEFORE_NEXT_GROUP = 3


def _scan_body(*refs, n_chunks, has_vlora, groups):
    n_in = len(refs) - 9
    ins, o_ref = refs[:n_in], refs[n_in]
    s_ref, stash = refs[n_in + 1], refs[n_in + 2:]
    t = pl.program_id(2)
    last = pl.num_programs(2) - 1
    write_slot = t % 2
    read_slot = 1 - write_slot
    shared = ((3, 4, 5, 9) if has_vlora else (3, 4, 5)) + (n_in - 2,)
    lanes = [pl.ds(g * MXU_DIM, MXU_DIM) for g in range(groups)]

    def heads():
        return [_scan_head(*[ref if i in shared else ref.at[:, lanes[g]] for i, ref in enumerate(ins)],
                           *[ref.at[write_slot, g] for ref in stash], n_chunks=n_chunks, has_vlora=has_vlora)
                for g in range(groups)]

    def tail():
        return _scan_tail([s_ref.at[g] for g in range(groups)],
                          [[ref.at[read_slot, g] for ref in stash] for g in range(groups)],
                          [ins[-1].at[:, lanes[g]] for g in range(groups)],
                          [o_ref.at[:, lanes[g]] for g in range(groups)], n_chunks=n_chunks)

    def run(pending, tail_gen):
        active = []
        while pending or active or tail_gen is not None:
            if pending and all(done >= HEAD_STAGES_BEFORE_NEXT_GROUP for _, done in active):
                active.append([pending.pop(0), 0])
            for entry in list(active):
                try:
                    entry[1] += next(entry[0]) == HEAD_STAGE
                except StopIteration:
                    active.remove(entry)
            if tail_gen is not None and next(tail_gen, None) is None:
                tail_gen = None

    @pl.when(t == 0)
    def _():
        s_ref[...] = jnp.zeros_like(s_ref)
        run(heads(), None)

    @pl.when((t > 0) & (t < last))
    def _():
        run(heads(), tail())

    @pl.when(t == last)
    def _():
        run([], tail())


def _scan_masks():
    W = MXU_DIM
    head_mask = (lax.broadcasted_iota(jnp.int32, (STACK, W), 0) // CHUNK
                 == lax.broadcasted_iota(jnp.int32, (STACK, W), 1) // HEAD)
    seg_ones = (lax.broadcasted_iota(jnp.int32, (W, W), 0) // HEAD
                == lax.broadcasted_iota(jnp.int32, (W, W), 1) // HEAD).astype(BF16)
    return head_mask, seg_ones


def _scan_tail(s_refs, stashes, p_refs, o_refs, *, n_chunks):
    head_mask, seg_ones = _scan_masks()
    groups = range(len(s_refs))
    chunks = range(n_chunks)
    n_pairs = HEADS_PER_GROUP // 2
    states = [s_refs[g][...] for g in groups]
    ys = [[] for _ in groups]
    for c in chunks:
        rows = pl.ds(c * CHUNK, CHUNK)
        for g in groups:
            g_v, n_v, re_v, y0_v, dec_v = stashes[g][:5]
            sb = states[g].astype(BF16)
            s_big = jnp.where(head_mask, jnp.concatenate([sb] * n_pairs, axis=0), jnp.zeros((), BF16))
            ys[g].append(_dot_nt(re_v[rows, :], s_big) + y0_v[rows, :])
            states[g] = states[g] * dec_v[c, 0:1, :] + _dot(sb, g_v[c]) + n_v[c]
        yield TAIL_STAGE
    for g in groups:
        s_refs[g][...] = states[g]
    for g in groups:
        bonus_v, gate_v = stashes[g][5:]
        gn_g = p_refs[g][5:6, :]
        gn_b = p_refs[g][6:7, :]
        y = jnp.concatenate(ys[g], axis=0)
        mu = _dot(y.astype(BF16), seg_ones) * (1.0 / HEAD)
        d = y - mu
        var = _dot((d * d).astype(BF16), seg_ones) * (1.0 / HEAD)
        yield TAIL_STAGE
        yn = d * lax.rsqrt(var + GN_EPS) * gn_g + gn_b
        o_refs[g][...] = ((yn + bonus_v[...]) * gate_v[...]).astype(o_refs[g].dtype)
        yield TAIL_STAGE


def _scan_head(*refs, n_chunks, has_vlora):
    stash_refs = refs[-7:]
    if has_vlora:
        (r_ref, k_ref, v_ref, mw_ref, ma_ref, mg_ref, w2_ref, a2_ref, g2_ref,
         mv_ref, v2_ref, vf_ref, lvl_ref, p_ref) = refs[:-7]
    else:
        r_ref, k_ref, v_ref, mw_ref, ma_ref, mg_ref, w2_ref, a2_ref, g2_ref, lvl_ref, p_ref = refs[:-7]
    g_s, n_s, re_s, y0_s, dec_s, bonus_s, gate_s = stash_refs

    W = MXU_DIM
    head_mask, seg_ones = _scan_masks()
    rr = lax.broadcasted_iota(jnp.int32, (STACK, STACK), 0)
    cc = lax.broadcasted_iota(jnp.int32, (STACK, STACK), 1)
    same_head = (rr // CHUNK) == (cc // CHUNK)
    strict_lower = same_head & ((rr % CHUNK) > (cc % CHUNK))
    eye = (rr == cc).astype(F32)
    half_mask = ((lax.broadcasted_iota(jnp.int32, (STACK, LANES), 0) // CHUNK) % 2
                 == lax.broadcasted_iota(jnp.int32, (STACK, LANES), 1) // HEAD)
    natural_lower = (lax.broadcasted_iota(jnp.int32, (CHUNK, 4 * CHUNK), 0)
                     >= lax.broadcasted_iota(jnp.int32, (CHUNK, 4 * CHUNK), 1) % CHUNK)
    bt = n_chunks * CHUNK
    tr = lax.broadcasted_iota(jnp.int32, (bt, bt), 0)
    tc = lax.broadcasted_iota(jnp.int32, (bt, bt), 1)
    tri = ((tr // CHUNK == tc // CHUNK) & (tr >= tc)).astype(BF16)

    w0 = p_ref[0:1, :]
    a0 = p_ref[1:2, :]
    k_k = p_ref[2:3, :]
    k_a = p_ref[3:4, :]
    r_k = p_ref[4:5, :]
    v0 = p_ref[7:8, :]

    chunks = range(n_chunks)

    r = r_ref[...]
    k = k_ref[...]
    v = v_ref[...]
    z = w0 + _dot(mw_ref[...], w2_ref[...])
    w_log = -(jnp.maximum(-z, 0.0) + jnp.log(1.0 + jnp.exp(-jnp.abs(z)))) - 0.5
    log_decay = -jnp.exp(w_log)
    yield PREP_STAGE
    a = _sigmoid(a0 + _dot(ma_ref[...], a2_ref[...]))
    if has_vlora:
        v = v + (vf_ref[...] - v) * _sigmoid(v0 + _dot(mv_ref[...], v2_ref[...]))
    yield PREP_STAGE
    kk = k * k_k
    norm = jnp.sqrt(_dot((kk * kk).astype(BF16), seg_ones))
    kk = kk / jnp.maximum(norm, 1e-12)
    k = k * (1.0 + (a - 1.0) * k_a)
    yield PREP_STAGE

    cum = _split_dot_left(tri, log_decay)
    cum_last = jnp.concatenate(
        [jnp.broadcast_to(cum[(c + 1) * CHUNK - 1:(c + 1) * CHUNK, :], (CHUNK, W)) for c in chunks], axis=0)
    e_pos = jnp.exp(cum)
    e_neg = jnp.exp(-cum)
    yield PREP_STAGE
    e_prev = jnp.exp(cum - log_decay)
    e_tail = jnp.exp(cum_last - cum)
    kka = kk * a
    yield PREP_STAGE
    a_e = -kk * e_prev
    r_e = r * e_pos
    b_e = kka * e_neg
    yield PREP_STAGE
    k_e = k * e_neg
    b_end = kka * e_tail
    k_end = k * e_tail

    def rows_of(x, c):
        return x[c * CHUNK:(c + 1) * CHUNK]

    def stack(x, c):
        return jnp.where(head_mask, jnp.concatenate([rows_of(x, c)] * HEADS_PER_GROUP, axis=0), 0.0)

    def stack_pair(x, c):
        xc = rows_of(x, c)
        parts = [xc[:, (h // 2) * LANES:(h // 2 + 1) * LANES] for h in range(HEADS_PER_GROUP)]
        return jnp.where(half_mask, jnp.concatenate(parts, axis=0), 0.0).astype(BF16)

    n_pairs = HEADS_PER_GROUP // 2
    pair_rows = [slice(q * 2 * CHUNK, (q + 1) * 2 * CHUNK) for q in range(n_pairs)]
    pair_lanes = [slice(q * LANES, (q + 1) * LANES) for q in range(n_pairs)]

    yield HEAD_STAGE

    a_p = [stack_pair(a_e, c) for c in chunks]
    b_p = [stack_pair(b_e, c) for c in chunks]
    k_p = [stack_pair(k_e, c) for c in chunks]
    v_p = [stack_pair(v, c) for c in chunks]
    p_a = [_dot_nt(a_p[c], jnp.concatenate([b_p[c], k_p[c]], axis=0)) for c in chunks]
    l_ab = [jnp.where(strict_lower, p_a[c][:, :STACK], 0.0) for c in chunks]
    l_ak = [jnp.where(strict_lower, p_a[c][:, STACK:], 0.0).astype(BF16) for c in chunks]
    yield HEAD_STAGE

    l_r = [[jnp.where(natural_lower,
                      _dot_nt(rows_of(r_e, c)[:, pair_lanes[q]].astype(BF16),
                              jnp.concatenate([b_p[c][pair_rows[q]], k_p[c][pair_rows[q]]], axis=0)),
                      0.0).astype(BF16) for q in range(n_pairs)] for c in chunks]
    yield HEAD_STAGE

    l_b = [l_ab[c].astype(BF16) for c in chunks]
    inv = [eye + (l_b[c] * lvl_ref[0]).astype(F32) for c in chunks]
    for level in range(1, int(math.log2(CHUNK))):
        half = [_dot(l_b[c] * lvl_ref[level], inv[c].astype(BF16)).astype(BF16) for c in chunks]
        yield HEAD_STAGE
        inv = [inv[c] + _dot(inv[c].astype(BF16), half[c]) for c in chunks]
        yield HEAD_STAGE

    x = [jnp.concatenate([a_p[c], _dot(l_ak[c], v_p[c]).astype(BF16)], axis=1) for c in chunks]
    yield HEAD_STAGE
    wu = [_dot(inv[c].astype(BF16), x[c]).astype(BF16) for c in chunks]
    yield HEAD_STAGE

    zero_pad = jnp.zeros((2 * CHUNK, LANES), BF16)
    for c in chunks:
        gn = _dot_tn(wu[c], stack(b_end, c).astype(BF16))
        n_s[c] = gn[LANES:] + _dot_tn(v_p[c], stack(k_end, c).astype(BF16))
        g_s[c] = jnp.where(head_mask, jnp.concatenate([gn[:LANES]] * n_pairs, axis=0), 0.0).astype(BF16)
        dec_s[c] = jnp.broadcast_to(jnp.exp(cum[(c + 1) * CHUNK - 1:(c + 1) * CHUNK, :]), (SUBLANES, W))
        yield HEAD_STAGE
        re_q, y0_q = [], []
        for q in range(n_pairs):
            rhs = jnp.concatenate([wu[c][pair_rows[q]],
                                   jnp.concatenate([zero_pad, v_p[c][pair_rows[q]]], axis=1)], axis=0)
            ry = _dot(l_r[c][q], rhs)
            re_q.append(rows_of(r_e, c)[:, pair_lanes[q]] + ry[:, :LANES])
            y0_q.append(ry[:, LANES:])
        re_s[pl.ds(c * CHUNK, CHUNK), :] = jnp.concatenate(re_q, axis=1).astype(BF16)
        y0_s[pl.ds(c * CHUNK, CHUNK), :] = jnp.concatenate(y0_q, axis=1)
        yield HEAD_STAGE
    bonus_s[...] = _dot((r * k * r_k).astype(BF16), seg_ones) * v
    gate_s[...] = _dot(mg_ref[...], g2_ref[...])
    yield HEAD_STAGE


def _doubling_level_masks():
    idx = jnp.arange(STACK)
    head, time = idx // CHUNK, idx % CHUNK
    same_head = head[:, None] == head[None, :]
    masks = []
    for k in range(int(math.log2(CHUNK))):
        m = 2 ** k
        same_pair = (time[:, None] // (2 * m)) == (time[None, :] // (2 * m))
        masks.append(same_head & same_pair & ((time[:, None] // m) % 2 == 1) & ((time[None, :] // m) % 2 == 0))
    return jnp.stack(masks).astype(BF16)


def _split_dot_left(e, x):
    hi = x.astype(BF16)
    lo = (x - hi.astype(F32)).astype(BF16)
    return _dot(e, hi) + _dot(e, lo)


def rwkv_scan(rkv, mids, ups, v_mix, chan, seq, *, bt=256, groups=4):
    _, T, D = rkv.shape
    groups = min(groups, D // MXU_DIM)
    W = groups * MXU_DIM
    bt = min(bt, seq)
    assert D % W == 0 and seq % bt == 0 and bt % CHUNK == 0
    has_vlora = v_mix is not None
    nb = seq // bt

    def row_in(b, t):
        return b * nb + jnp.minimum(t, nb - 1)

    def stacked(p):
        return pl.BlockSpec((None, bt, W), lambda b, gidx, t: (p, row_in(b, t), gidx))

    def mid(a):
        return pl.BlockSpec((bt, a.shape[1]), lambda b, gidx, t: (row_in(b, t), 0))

    def up(a):
        return pl.BlockSpec((a.shape[0], W), lambda b, gidx, t: (0, gidx))

    ins = [rkv, rkv, rkv, *mids, *ups]
    specs = [stacked(0), stacked(1), stacked(2)] + [mid(a) for a in mids] + [up(a) for a in ups]
    if has_vlora:
        mv, v2, first_rkv = v_mix
        ins += [mv, v2, first_rkv]
        specs += [mid(mv), up(v2), stacked(2)]
    levels = _doubling_level_masks()
    ins += [levels, chan]
    specs += [pl.BlockSpec(levels.shape, lambda b, gidx, t: (0, 0, 0)),
              pl.BlockSpec((SUBLANES, W), lambda b, gidx, t: (0, gidx))]
    nc = bt // CHUNK
    G = MXU_DIM
    stash = [pltpu.VMEM((2, groups, nc, G, G), BF16),
             pltpu.VMEM((2, groups, nc, 2 * HEAD, G), F32),
             pltpu.VMEM((2, groups, bt, G), BF16),
             pltpu.VMEM((2, groups, bt, G), F32),
             pltpu.VMEM((2, groups, nc, SUBLANES, G), F32),
             pltpu.VMEM((2, groups, bt, G), F32),
             pltpu.VMEM((2, groups, bt, G), F32)]
    return pl.pallas_call(
        functools.partial(_scan_body, n_chunks=nc, has_vlora=has_vlora, groups=groups),
        grid=(T // seq, D // W, nb + 1),
        in_specs=specs,
        out_specs=pl.BlockSpec((bt, W), lambda b, gidx, t: (b * nb + jnp.maximum(t - 1, 0), gidx)),
        out_shape=jax.ShapeDtypeStruct((T, D), BF16),
        scratch_shapes=[pltpu.VMEM((groups, 2 * HEAD, G), F32)] + stash,
        compiler_params=_params("parallel", "parallel", "arbitrary"),
        name="rwkv_scan",
    )(*ins)


def _bias_body(bmap_ref, rb_ref, o_ref):
    h = pl.program_id(0)
    bmap = bmap_ref[...]
    acc = jnp.full(bmap.shape, -jnp.inf, F32)
    for b in range(REL_BUCKETS):
        acc = jnp.where(bmap == b, rb_ref[b, h], acc)
    o_ref[0, 0] = acc
    kj = lax.broadcasted_iota(jnp.int32, bmap.shape, 1)
    o_ref[1, 0] = jnp.where(kj >= BLOCK, acc, -jnp.inf)


def bias_table(rel_bias, bmap):
    H = rel_bias.shape[1]
    return pl.pallas_call(
        _bias_body,
        grid=(H,),
        in_specs=[pl.BlockSpec((BLOCK, 2 * BLOCK), lambda h: (0, 0)),
                  pl.BlockSpec(memory_space=pltpu.SMEM)],
        out_specs=pl.BlockSpec((2, 1, BLOCK, 2 * BLOCK), lambda h: (0, h, 0, 0)),
        out_shape=jax.ShapeDtypeStruct((2, H, BLOCK, 2 * BLOCK), F32),
        compiler_params=_params("parallel"),
        name="rel_bias_table",
    )(bmap, rel_bias)


def _attn_body(q_ref, kp_ref, kc_ref, vp_ref, vc_ref, bias_ref, sink_ref, o_ref, *, group, n_kv):
    low = lax.broadcasted_iota(jnp.int32, (BLOCK, LANES), 1) < HEAD
    scale = HEAD ** -0.5

    for kv_pair in range(n_kv // 2):
        lanes = pl.ds(kv_pair * LANES, LANES)
        kband = jnp.concatenate([kp_ref[:, lanes], kc_ref[:, lanes]], axis=0)
        vband = jnp.concatenate([vp_ref[:, lanes], vc_ref[:, lanes]], axis=0)
        kswap = pltpu.roll(kband, HEAD, 1)
        vswap = pltpu.roll(vband, HEAD, 1)
        for e in range(2):
            kvh = 2 * kv_pair + e
            k_at = (kband, kswap) if e == 0 else (kswap, kband)
            v_at = (vband, vswap) if e == 0 else (vswap, vband)
            for pair in range(group // 2):
                q_lanes = pl.ds((kvh * group // 2 + pair) * LANES, LANES)
                qp = q_ref[:, q_lanes] * scale
                halves = []
                for half in range(2):
                    h = kvh * group + 2 * pair + half
                    in_half = low if half == 0 else jnp.logical_not(low)
                    qm = jnp.where(in_half, qp, jnp.zeros_like(qp))
                    s = _dot_nt(qm, k_at[half]) + bias_ref[0, h]
                    sink = sink_ref[h]
                    m = jnp.maximum(jnp.max(s, axis=-1, keepdims=True), sink)
                    p = jnp.exp(s - m)
                    denom = jnp.sum(p, axis=-1, keepdims=True) + jnp.exp(sink - m)
                    halves.append(_dot(p.astype(BF16), v_at[half]) / denom)
                o_ref[:, q_lanes] = jnp.where(low, halves[0], halves[1]).astype(o_ref.dtype)


def swa_attention(qkv, bias, sinks, seq, n_q_heads, n_kv_heads):
    T = qkv.shape[0]
    group = n_q_heads // n_kv_heads
    assert group % 2 == 0 and n_kv_heads % 2 == 0 and seq % BLOCK == 0
    nb = seq // BLOCK
    qw, kw = n_q_heads * HEAD, n_kv_heads * HEAD
    assert qw % kw == 0 and kw % LANES == 0
    k_col, v_col = qw // kw, qw // kw + 1

    def cur(col):
        return pl.BlockSpec((BLOCK, kw), lambda b, n: (b * nb + n, col))

    def prev(col):
        return pl.BlockSpec((BLOCK, kw), lambda b, n: (b * nb + jnp.maximum(n - 1, 0), col))

    return pl.pallas_call(
        functools.partial(_attn_body, group=group, n_kv=n_kv_heads),
        grid=(T // seq, nb),
        in_specs=[pl.BlockSpec((BLOCK, qw), lambda b, n: (b * nb + n, 0)),
                  prev(k_col), cur(k_col), prev(v_col), cur(v_col),
                  pl.BlockSpec((1, n_q_heads, BLOCK, 2 * BLOCK), lambda b, n: (jnp.where(n == 0, 1, 0), 0, 0, 0)),
                  pl.BlockSpec(memory_space=pltpu.SMEM)],
        out_specs=pl.BlockSpec((BLOCK, qw), lambda b, n: (b * nb + n, 0)),
        out_shape=jax.ShapeDtypeStruct((T, qw), BF16),
        compiler_params=_params("parallel", "parallel"),
        name="swa_attention",
    )(qkv, qkv, qkv, qkv, qkv, bias, sinks)


def _bucket_map():
    qi = jnp.arange(BLOCK)[:, None]
    kj = jnp.arange(2 * BLOCK)[None, :]
    signed = qi + BLOCK - kj
    dist = jnp.maximum(signed, 0)
    max_exact = REL_BUCKETS // 2
    d_f = jnp.maximum(dist, max_exact).astype(F32)
    large = max_exact + (jnp.log(d_f / max_exact) / math.log(REL_MAX_DIST / max_exact)
                         * (REL_BUCKETS - max_exact)).astype(jnp.int32)
    large = jnp.minimum(large, REL_BUCKETS - 1)
    bucket = jnp.where(dist < max_exact, dist, large)
    return jnp.where((signed >= 0) & (signed < WINDOW), bucket, -1).astype(jnp.int32)


def _ffn_up_body(x_ref, *refs, blocks_per_seq, sub, rsub, n_sub):
    wg_refs, wu_refs = refs[:n_sub], refs[n_sub:2 * n_sub]
    cg_ref, cu_ref, bg_ref, bu_ref, o_ref, tail_g_ref, tail_u_ref, raw_g_ref, raw_u_ref = refs[2 * n_sub:]
    first = pl.program_id(0) % blocks_per_seq == 0
    j = pl.program_id(1)
    bm = x_ref.shape[0]
    H = SUBLANES

    @pl.when((pl.program_id(0) == 0) & (j == 0))
    def _():
        tail_g_ref[...] = jnp.zeros_like(tail_g_ref)
        tail_u_ref[...] = jnp.zeros_like(tail_u_ref)

    def matmul_tile(slot, s, rb, halo_g, halo_u):
        xr = x_ref[pl.ds(rb * rsub, rsub), :]
        raw_g_ref[slot, 0:H, :] = halo_g
        raw_u_ref[slot, 0:H, :] = halo_u
        raw_g_ref[slot, H:, :] = _dot(xr, wg_refs[s][...])
        raw_u_ref[slot, H:, :] = _dot(xr, wu_refs[s][...])
        return raw_g_ref[slot, rsub:, :], raw_u_ref[slot, rsub:, :]

    def conv(raw_ref, slot, c_ref, b_ref, cols):
        h = raw_ref[slot, H:, :]
        h1 = raw_ref[slot, H - 1:H - 1 + rsub, :]
        h2 = raw_ref[slot, H - 2:H - 2 + rsub, :]
        return c_ref[0:1, cols] * h2 + c_ref[1:2, cols] * h1 + c_ref[2:3, cols] * h + b_ref[:, cols]

    def finish_tile(slot, s, rb):
        cols = pl.ds(s * sub, sub)
        gate = conv(raw_g_ref, slot, cg_ref, bg_ref, cols)
        up = conv(raw_u_ref, slot, cu_ref, bu_ref, cols)
        o_ref[pl.ds(rb * rsub, rsub), cols] = (gate * _sigmoid(gate) * up).astype(o_ref.dtype)

    n_rows = bm // rsub
    prev = None
    t = 0
    for s in range(n_sub):
        cols = pl.ds(s * sub, sub)
        halo_g = jnp.where(first, 0.0, tail_g_ref[j, :, cols])
        halo_u = jnp.where(first, 0.0, tail_u_ref[j, :, cols])
        for rb in range(n_rows):
            halo_g, halo_u = matmul_tile(t % 2, s, rb, halo_g, halo_u)
            if prev is not None:
                finish_tile(*prev)
            prev = (t % 2, s, rb)
            t += 1
        tail_g_ref[j, :, cols] = halo_g
        tail_u_ref[j, :, cols] = halo_u
    finish_tile(*prev)


def ffn_up(x, w2, layer, cg, cu, bg, bu, seq, *, bm=1024, bn=512, sub=256, rsub=256):
    T, D = x.shape
    F = w2.shape[2] // 2
    bm, bn = min(bm, seq), min(bn, F)
    sub, rsub = min(sub, bn), min(rsub, bm)
    assert seq % bm == 0 and bn % sub == 0 and F % sub == 0 and bm % rsub == 0 and rsub % SUBLANES == 0
    nf = pl.cdiv(F, bn)
    n_sub = bn // sub
    last = 2 * F // sub - 1

    def wspec(s, half):
        return pl.BlockSpec((None, D, sub),
                            lambda i, j: (layer, 0, jnp.minimum(half * (F // sub) + j * n_sub + s, last)))

    col = lambda rows: pl.BlockSpec((rows, bn), lambda i, j: (0, j))
    return pl.pallas_call(
        functools.partial(_ffn_up_body, blocks_per_seq=seq // bm, sub=sub, rsub=rsub, n_sub=n_sub),
        grid=(T // bm, nf),
        in_specs=[pl.BlockSpec((bm, D), lambda i, j: (i, 0))]
                 + [wspec(s, 0) for s in range(n_sub)] + [wspec(s, 1) for s in range(n_sub)]
                 + [col(CONV_W), col(CONV_W), col(1), col(1)],
        out_specs=pl.BlockSpec((bm, bn), lambda i, j: (i, j)),
        out_shape=jax.ShapeDtypeStruct((T, F), BF16),
        scratch_shapes=[pltpu.VMEM((nf, SUBLANES, bn), F32), pltpu.VMEM((nf, SUBLANES, bn), F32),
                        pltpu.VMEM((2, SUBLANES + rsub, sub), F32), pltpu.VMEM((2, SUBLANES + rsub, sub), F32)],
        compiler_params=_params("arbitrary", "arbitrary"),
        name="ffn_up",
    )(x, *([w2] * (2 * n_sub)), cg, cu, bg, bu)


def _pad_to(x, axis, mult):
    pad = (-x.shape[axis]) % mult
    if pad == 0:
        return x
    widths = [(0, 0)] * x.ndim
    widths[axis] = (0, pad)
    return jnp.pad(x, widths)


WIDE_COL_BLOCK = 1024
SCAN_TIME_BLOCK = 4 * CHUNK


def _lora_weights(w_down, w_up):
    return _pad_to(w_down, 1, LANES).astype(BF16), _pad_to(w_up, 0, LANES).astype(BF16)


def rwkv_layer(x, seq, first_rkv, layer, mix, w_rkv, w_o, w0, w1, w2, a0, a1, a2, g1, g2,
               k_k, k_a, r_k, gn_g, gn_b, v_lora):
    w1, w2 = _lora_weights(w1, w2)
    a1, a2 = _lora_weights(a1, a2)
    g1, g2 = _lora_weights(g1, g2)
    loras = [(MIX_W, w1, "tanh"), (MIX_A, a1, None), (MIX_G, g1, "sigmoid")]
    if v_lora is None:
        v0 = jnp.zeros_like(w0)
    else:
        v0, v1, v2 = v_lora
        v1, v2 = _lora_weights(v1, v2)
        loras.append((MIX_V, v1, None))
    xs, mids = token_shift_mix(x, mix, loras, seq)
    rkv = batched_matmul(xs, w_rkv, F32, nbatch=3, layer=layer, bn=WIDE_COL_BLOCK, name="rwkv_rkv")
    v_mix = None if v_lora is None else (mids[3], v2, first_rkv)
    chan = jnp.stack([w0, a0, k_k, k_a, r_k, gn_g, gn_b, v0])
    y = rwkv_scan(rkv, tuple(mids[:3]), (w2, a2, g2), v_mix, chan, seq, bt=SCAN_TIME_BLOCK)
    return matmul(y, w_o, BF16, layer=layer, bn=WIDE_COL_BLOCK, name="rwkv_out"), rkv


def attn_layer(xb, seq, layer, w_qkv, w_o, sinks, bias):
    n_q = w_o.shape[1] // HEAD
    n_kv = (w_qkv.shape[2] // HEAD - n_q) // 2
    qkv = matmul(xb, w_qkv, BF16, layer=layer, bn=WIDE_COL_BLOCK, name="attn_qkv")
    o = swa_attention(qkv, bias, sinks, seq, n_q, n_kv)
    return matmul(o, w_o, BF16, layer=layer, bn=WIDE_COL_BLOCK, name="attn_out")


FFN_COL_BLOCK = 512
FFN_ROW_BLOCK = 1024
FFN_COL_SUB = MXU_DIM


def ffn_layer(xb, seq, layer, w_up, conv_w, conv_b, w_down):
    f = w_down.shape[1]
    conv_b = conv_b.reshape(1, -1)
    act = ffn_up(xb, w_up, layer, conv_w[:, :f], conv_w[:, f:], conv_b[:, :f], conv_b[:, f:],
                 seq, bm=FFN_ROW_BLOCK, bn=FFN_COL_BLOCK, sub=FFN_COL_SUB)
    return matmul(act, w_down, BF16, layer=layer, bm=512, bn=512, name="ffn_down")


def kernel(x, rwkv_mix, rwkv_w_rkv, rwkv_w_o, rwkv_w0, rwkv_w1, rwkv_w2, rwkv_a0, rwkv_a1, rwkv_a2, rwkv_g1, rwkv_g2, rwkv_k_k, rwkv_k_a, rwkv_r_k, rwkv_gn_g, rwkv_gn_b, rwkv_v0, rwkv_v1, rwkv_v2, attn_w_qkv, attn_w_o, attn_sinks, rel_bias, ffn_w_up, ffn_conv_w, ffn_conv_b, ffn_w_down, ln1_g, ln1_b, ln2_g, ln2_b):
    B, S, D = x.shape
    depth = ln1_g.shape[0]
    xf = x.reshape(B * S, D)
    xb = xf.astype(BF16)
    bias = bias_table(rel_bias, _bucket_map())
    w_rkv, w_ro = rwkv_w_rkv.astype(BF16), rwkv_w_o.astype(BF16)
    w_qkv, w_ao = attn_w_qkv.astype(BF16), attn_w_o.astype(BF16)
    w_up, w_down = ffn_w_up.astype(BF16), ffn_w_down.astype(BF16)
    first_rkv = None
    for i in range(depth):
        j = i // 2
        if i % 2 == 0:
            v_lora = None if j == 0 else (rwkv_v0[j - 1], rwkv_v1[j - 1], rwkv_v2[j - 1])
            h, rkv = rwkv_layer(xf, S, first_rkv, j, rwkv_mix[j], w_rkv, w_ro,
                                rwkv_w0[j], rwkv_w1[j], rwkv_w2[j], rwkv_a0[j],
                                rwkv_a1[j], rwkv_a2[j], rwkv_g1[j], rwkv_g2[j], rwkv_k_k[j],
                                rwkv_k_a[j], rwkv_r_k[j], rwkv_gn_g[j], rwkv_gn_b[j], v_lora)
            if v_lora is None:
                first_rkv = rkv
        else:
            h = attn_layer(xb, S, j, w_qkv, w_ao, attn_sinks[j], bias)
        xf, xb = ln_residual(xf, h, ln1_g[i], ln1_b[i])
        f = ffn_layer(xb, S, i, w_up, ffn_conv_w[i], ffn_conv_b[i], w_down)
        xf, xb = ln_residual(xf, f, ln2_g[i], ln2_b[i])
    return xf.reshape(B, S, D)
```

```python
import functools
import math

import jax
import jax.numpy as jnp
from jax import lax
from jax.experimental import pallas as pl
from jax.experimental.pallas import tpu as pltpu

F32 = jnp.float32
BF16 = jnp.bfloat16

LANES = 128
SUBLANES = 8
MXU_DIM = 256
VMEM_LIMIT_BYTES = 56 * 1024 * 1024

HEAD = 64
HEADS_PER_GROUP = MXU_DIM // HEAD
CHUNK = 64
STACK = HEADS_PER_GROUP * CHUNK

WINDOW = 128
BLOCK = 128
REL_BUCKETS = 32
REL_MAX_DIST = 128
CONV_W = 3
LN_EPS = 1e-5
GN_EPS = HEAD * 1e-5
DEPTH = 4
DEEPNORM_ALPHA = (2 * DEPTH) ** 0.25


def _params(*sem):
    return pltpu.CompilerParams(dimension_semantics=sem, vmem_limit_bytes=VMEM_LIMIT_BYTES)


def _block(n, target, unit):
    if n <= target:
        return n
    b = target - target % unit
    while n % b:
        b -= unit
    return b


def _sigmoid(x):
    return 1.0 / (1.0 + jnp.exp(-x))


def _dot(a, b):
    return jnp.dot(a, b, preferred_element_type=F32)


def _dot_nt(a, b):
    return lax.dot_general(a, b, (((1,), (1,)), ((), ())), preferred_element_type=F32)


def _dot_tn(a, b):
    return lax.dot_general(a, b, (((0,), (0,)), ((), ())), preferred_element_type=F32)


def _mm_body(x_ref, w_ref, o_ref):
    o_ref[...] = _dot(x_ref[...], w_ref[...]).astype(o_ref.dtype)


def matmul(x, w, out_dtype, *, layer, bm=1024, bn=512, name="matmul"):
    M, K = x.shape
    N = w.shape[-1]
    bm, bn = _block(M, bm, SUBLANES), _block(N, bn, LANES)
    assert M % bm == 0 and N % bn == 0
    return pl.pallas_call(
        _mm_body,
        grid=(M // bm, N // bn),
        in_specs=[pl.BlockSpec((bm, K), lambda i, j: (i, 0)),
                  pl.BlockSpec((None, K, bn), lambda i, j: (layer, 0, j))],
        out_specs=pl.BlockSpec((bm, bn), lambda i, j: (i, j)),
        out_shape=jax.ShapeDtypeStruct((M, N), out_dtype),
        compiler_params=_params("parallel", "parallel"),
        name=name,
    )(x, w)


def batched_matmul(x, w, out_dtype, *, nbatch, layer, bm=1024, bn=512, name="bmm"):
    M, K = x.shape[-2:]
    N = w.shape[-1]
    bm, bn = _block(M, bm, SUBLANES), _block(N, bn, LANES)
    assert M % bm == 0 and N % bn == 0
    return pl.pallas_call(
        _mm_body,
        grid=(nbatch, M // bm, N // bn),
        in_specs=[pl.BlockSpec((None, bm, K), lambda p, i, j: (p, i, 0)),
                  pl.BlockSpec((None, None, K, bn), lambda p, i, j: (layer, p, 0, j))],
        out_specs=pl.BlockSpec((None, bm, bn), lambda p, i, j: (p, i, j)),
        out_shape=jax.ShapeDtypeStruct((nbatch, M, N), out_dtype),
        compiler_params=_params("parallel", "parallel", "parallel"),
        name=name,
    )(x, w)


def _ln_body(x_ref, h_ref, g_ref, b_ref, of_ref, ob_ref):
    y = DEEPNORM_ALPHA * x_ref[...] + h_ref[...].astype(F32)
    mu = jnp.mean(y, axis=-1, keepdims=True)
    d = y - mu
    var = jnp.mean(d * d, axis=-1, keepdims=True)
    out = d * lax.rsqrt(var + LN_EPS) * g_ref[...] + b_ref[...]
    of_ref[...] = out
    ob_ref[...] = out.astype(BF16)


def ln_residual(x, h, g, b, *, br=256):
    T, D = x.shape
    br = min(br, T)
    assert T % br == 0
    row = pl.BlockSpec((br, D), lambda i: (i, 0))
    vec = pl.BlockSpec((1, D), lambda i: (0, 0))
    return pl.pallas_call(
        _ln_body,
        grid=(T // br,),
        in_specs=[row, row, vec, vec],
        out_specs=[row, row],
        out_shape=[jax.ShapeDtypeStruct((T, D), F32), jax.ShapeDtypeStruct((T, D), BF16)],
        compiler_params=_params("parallel"),
        name="ln_residual",
    )(x, h, g.reshape(1, D), b.reshape(1, D))


MIX_R, MIX_W, MIX_K, MIX_V, MIX_A, MIX_G = range(6)


def _mix_body(x_ref, xp_ref, mix_ref, *refs, blocks_per_seq, lora_rows, lora_acts):
    n = len(lora_rows)
    w_refs, o_ref, mid_refs, acc_refs = refs[:n], refs[n], refs[n + 1:2 * n + 1], refs[2 * n + 1:]
    i, j = pl.program_id(0), pl.program_id(1)
    x = x_ref[...]
    prev_last = xp_ref[SUBLANES - 1:SUBLANES, :]
    prev_last = jnp.where(i % blocks_per_seq == 0, 0.0, prev_last)
    row = lax.broadcasted_iota(jnp.int32, x.shape, 0)
    shifted = jnp.where(row == 0, prev_last, pltpu.roll(x, 1, 0))
    xx = shifted - x

    def mixed(r):
        return (x + xx * mix_ref[r:r + 1, :]).astype(BF16)

    for p, r in enumerate((MIX_R, MIX_K, MIX_V)):
        o_ref[p] = mixed(r)

    @pl.when(j == 0)
    def _():
        for acc in acc_refs:
            acc[...] = jnp.zeros_like(acc)

    for q in range(n):
        acc_refs[q][...] += _dot(mixed(lora_rows[q]), w_refs[q][...])

    @pl.when(j == pl.num_programs(1) - 1)
    def _():
        for q in range(n):
            acc = acc_refs[q][...]
            if lora_acts[q] == "tanh":
                acc = jnp.tanh(acc)
            elif lora_acts[q] == "sigmoid":
                acc = _sigmoid(acc)
            mid_refs[q][...] = acc.astype(BF16)


def token_shift_mix(x, mix, loras, seq, *, bt=512, bd=1024):
    T, D = x.shape
    bt, bd = min(bt, seq), min(bd, D)
    assert seq % bt == 0 and D % bd == 0 and bt % SUBLANES == 0
    halo = bt // SUBLANES
    widths = [w.shape[1] for _, w, _ in loras]
    outs = pl.pallas_call(
        functools.partial(_mix_body, blocks_per_seq=seq // bt, lora_rows=tuple(r for r, _, _ in loras),
                          lora_acts=tuple(a for _, _, a in loras)),
        grid=(T // bt, D // bd),
        in_specs=[pl.BlockSpec((bt, bd), lambda i, j: (i, j)),
                  pl.BlockSpec((SUBLANES, bd), lambda i, j: (jnp.maximum(i * halo - 1, 0), j)),
                  pl.BlockSpec((6, bd), lambda i, j: (0, j))]
                 + [pl.BlockSpec((bd, n), lambda i, j: (j, 0)) for n in widths],
        out_specs=[pl.BlockSpec((3, bt, bd), lambda i, j: (0, i, j))]
                  + [pl.BlockSpec((bt, n), lambda i, j: (i, 0)) for n in widths],
        out_shape=[jax.ShapeDtypeStruct((3, T, D), BF16)] + [jax.ShapeDtypeStruct((T, n), BF16) for n in widths],
        scratch_shapes=[pltpu.VMEM((bt, n), F32) for n in widths],
        compiler_params=_params("parallel", "arbitrary"),
        name="rwkv_mix",
    )(x, x, mix, *[w for _, w, _ in loras])
    return outs[0], outs[1:]


PREP_STAGE, HEAD_STAGE, TAIL_STAGE = "prep", "head", "tail"
HEAD_STAGES_BEFORE_NEXT_GROUP = 3


def _scan_body(*refs, n_chunks, has_vlora, groups):
    n_in = len(refs) - 9
    ins, o_ref = refs[:n_in], refs[n_in]
    s_ref, stash = refs[n_in + 1], refs[n_in + 2:]
    t = pl.program_id(2)
    last = pl.num_programs(2) - 1
    write_slot = t % 2
    read_slot = 1 - write_slot
    shared = ((3, 4, 5, 9) if has_vlora else (3, 4, 5)) + (n_in - 2,)
    lanes = [pl.ds(g * MXU_DIM, MXU_DIM) for g in range(groups)]

    def heads():
        return [_scan_head(*[ref if i in shared else ref.at[:, lanes[g]] for i, ref in enumerate(ins)],
                           *[ref.at[write_slot, g] for ref in stash], n_chunks=n_chunks, has_vlora=has_vlora)
                for g in range(groups)]

    def tail():
        return _scan_tail([s_ref.at[g] for g in range(groups)],
                          [[ref.at[read_slot, g] for ref in stash] for g in range(groups)],
                          [ins[-1].at[:, lanes[g]] for g in range(groups)],
                          [o_ref.at[:, lanes[g]] for g in range(groups)], n_chunks=n_chunks)

    def run(pending, tail_gen):
        active = []
        while pending or active or tail_gen is not None:
            if pending and all(done >= HEAD_STAGES_BEFORE_NEXT_GROUP for _, done in active):
                active.append([pending.pop(0), 0])
            for entry in list(active):
                try:
                    entry[1] += next(entry[0]) == HEAD_STAGE
                except StopIteration:
                    active.remove(entry)
            if tail_gen is not None and next(tail_gen, None) is None:
                tail_gen = None

    @pl.when(t == 0)
    def _():
        s_ref[...] = jnp.zeros_like(s_ref)
        run(heads(), None)

    @pl.when((t > 0) & (t < last))
    def _():
        run(heads(), tail())

    @pl.when(t == last)
    def _():
        run([], tail())


def _scan_masks():
    W = MXU_DIM
    head_mask = (lax.broadcasted_iota(jnp.int32, (STACK, W), 0) // CHUNK
                 == lax.broadcasted_iota(jnp.int32, (STACK, W), 1) // HEAD)
    seg_ones = (lax.broadcasted_iota(jnp.int32, (W, W), 0) // HEAD
                == lax.broadcasted_iota(jnp.int32, (W, W), 1) // HEAD).astype(BF16)
    return head_mask, seg_ones


def _scan_tail(s_refs, stashes, p_refs, o_refs, *, n_chunks):
    head_mask, seg_ones = _scan_masks()
    groups = range(len(s_refs))
    chunks = range(n_chunks)
    n_pairs = HEADS_PER_GROUP // 2
    states = [s_refs[g][...] for g in groups]
    ys = [[] for _ in groups]
    for c in chunks:
        rows = pl.ds(c * CHUNK, CHUNK)
        for g in groups:
            g_v, n_v, re_v, y0_v, dec_v = stashes[g][:5]
            sb = states[g].astype(BF16)
            s_big = jnp.where(head_mask, jnp.concatenate([sb] * n_pairs, axis=0), jnp.zeros((), BF16))
            ys[g].append(_dot_nt(re_v[rows, :], s_big) + y0_v[rows, :])
            states[g] = states[g] * dec_v[c, 0:1, :] + _dot(sb, g_v[c]) + n_v[c]
        yield TAIL_STAGE
    for g in groups:
        s_refs[g][...] = states[g]
    for g in groups:
        bonus_v, gate_v = stashes[g][5:]
        gn_g = p_refs[g][5:6, :]
        gn_b = p_refs[g][6:7, :]
        y = jnp.concatenate(ys[g], axis=0)
        mu = _dot(y.astype(BF16), seg_ones) * (1.0 / HEAD)
        d = y - mu
        var = _dot((d * d).astype(BF16), seg_ones) * (1.0 / HEAD)
        yield TAIL_STAGE
        yn = d * lax.rsqrt(var + GN_EPS) * gn_g + gn_b
        o_refs[g][...] = ((yn + bonus_v[...]) * gate_v[...]).astype(o_refs[g].dtype)
        yield TAIL_STAGE


def _scan_head(*refs, n_chunks, has_vlora):
    stash_refs = refs[-7:]
    if has_vlora:
        (r_ref, k_ref, v_ref, mw_ref, ma_ref, mg_ref, w2_ref, a2_ref, g2_ref,
         mv_ref, v2_ref, vf_ref, lvl_ref, p_ref) = refs[:-7]
    else:
        r_ref, k_ref, v_ref, mw_ref, ma_ref, mg_ref, w2_ref, a2_ref, g2_ref, lvl_ref, p_ref = refs[:-7]
    g_s, n_s, re_s, y0_s, dec_s, bonus_s, gate_s = stash_refs

    W = MXU_DIM
    head_mask, seg_ones = _scan_masks()
    rr = lax.broadcasted_iota(jnp.int32, (STACK, STACK), 0)
    cc = lax.broadcasted_iota(jnp.int32, (STACK, STACK), 1)
    same_head = (rr // CHUNK) == (cc // CHUNK)
    strict_lower = same_head & ((rr % CHUNK) > (cc % CHUNK))
    eye = (rr == cc).astype(F32)
    half_mask = ((lax.broadcasted_iota(jnp.int32, (STACK, LANES), 0) // CHUNK) % 2
                 == lax.broadcasted_iota(jnp.int32, (STACK, LANES), 1) // HEAD)
    natural_lower = (lax.broadcasted_iota(jnp.int32, (CHUNK, 4 * CHUNK), 0)
                     >= lax.broadcasted_iota(jnp.int32, (CHUNK, 4 * CHUNK), 1) % CHUNK)
    bt = n_chunks * CHUNK
    tr = lax.broadcasted_iota(jnp.int32, (bt, bt), 0)
    tc = lax.broadcasted_iota(jnp.int32, (bt, bt), 1)
    tri = ((tr // CHUNK == tc // CHUNK) & (tr >= tc)).astype(BF16)

    w0 = p_ref[0:1, :]
    a0 = p_ref[1:2, :]
    k_k = p_ref[2:3, :]
    k_a = p_ref[3:4, :]
    r_k = p_ref[4:5, :]
    v0 = p_ref[7:8, :]

    chunks = range(n_chunks)

    r = r_ref[...]
    k = k_ref[...]
    v = v_ref[...]
    z = w0 + _dot(mw_ref[...], w2_ref[...])
    w_log = -(jnp.maximum(-z, 0.0) + jnp.log(1.0 + jnp.exp(-jnp.abs(z)))) - 0.5
    log_decay = -jnp.exp(w_log)
    yield PREP_STAGE
    a = _sigmoid(a0 + _dot(ma_ref[...], a2_ref[...]))
    if has_vlora:
        v = v + (vf_ref[...] - v) * _sigmoid(v0 + _dot(mv_ref[...], v2_ref[...]))
    yield PREP_STAGE
    kk = k * k_k
    norm = jnp.sqrt(_dot((kk * kk).astype(BF16), seg_ones))
    kk = kk / jnp.maximum(norm, 1e-12)
    k = k * (1.0 + (a - 1.0) * k_a)
    yield PREP_STAGE

    cum = _split_dot_left(tri, log_decay)
    cum_last = jnp.concatenate(
        [jnp.broadcast_to(cum[(c + 1) * CHUNK - 1:(c + 1) * CHUNK, :], (CHUNK, W)) for c in chunks], axis=0)
    e_pos = jnp.exp(cum)
    e_neg = jnp.exp(-cum)
    yield PREP_STAGE
    e_prev = jnp.exp(cum - log_decay)
    e_tail = jnp.exp(cum_last - cum)
    kka = kk * a
    yield PREP_STAGE
    a_e = -kk * e_prev
    r_e = r * e_pos
    b_e = kka * e_neg
    yield PREP_STAGE
    k_e = k * e_neg
    b_end = kka * e_tail
    k_end = k * e_tail

    def rows_of(x, c):
        return x[c * CHUNK:(c + 1) * CHUNK]

    def stack(x, c):
        return jnp.where(head_mask, jnp.concatenate([rows_of(x, c)] * HEADS_PER_GROUP, axis=0), 0.0)

    def stack_pair(x, c):
        xc = rows_of(x, c)
        parts = [xc[:, (h // 2) * LANES:(h // 2 + 1) * LANES] for h in range(HEADS_PER_GROUP)]
        return jnp.where(half_mask, jnp.concatenate(parts, axis=0), 0.0).astype(BF16)

    n_pairs = HEADS_PER_GROUP // 2
    pair_rows = [slice(q * 2 * CHUNK, (q + 1) * 2 * CHUNK) for q in range(n_pairs)]
    pair_lanes = [slice(q * LANES, (q + 1) * LANES) for q in range(n_pairs)]

    yield HEAD_STAGE

    a_p = [stack_pair(a_e, c) for c in chunks]
    b_p = [stack_pair(b_e, c) for c in chunks]
    k_p = [stack_pair(k_e, c) for c in chunks]
    v_p = [stack_pair(v, c) for c in chunks]
    p_a = [_dot_nt(a_p[c], jnp.concatenate([b_p[c], k_p[c]], axis=0)) for c in chunks]
    l_ab = [jnp.where(strict_lower, p_a[c][:, :STACK], 0.0) for c in chunks]
    l_ak = [jnp.where(strict_lower, p_a[c][:, STACK:], 0.0).astype(BF16) for c in chunks]
    yield HEAD_STAGE

    l_r = [[jnp.where(natural_lower,
                      _dot_nt(rows_of(r_e, c)[:, pair_lanes[q]].astype(BF16),
                              jnp.concatenate([b_p[c][pair_rows[q]], k_p[c][pair_rows[q]]], axis=0)),
                      0.0).astype(BF16) for q in range(n_pairs)] for c in chunks]
    yield HEAD_STAGE

    l_b = [l_ab[c].astype(BF16) for c in chunks]
    inv = [eye + (l_b[c] * lvl_ref[0]).astype(F32) for c in chunks]
    for level in range(1, int(math.log2(CHUNK))):
        half = [_dot(l_b[c] * lvl_ref[level], inv[c].astype(BF16)).astype(BF16) for c in chunks]
        yield HEAD_STAGE
        inv = [inv[c] + _dot(inv[c].astype(BF16), half[c]) for c in chunks]
        yield HEAD_STAGE

    x = [jnp.concatenate([a_p[c], _dot(l_ak[c], v_p[c]).astype(BF16)], axis=1) for c in chunks]
    yield HEAD_STAGE
    wu = [_dot(inv[c].astype(BF16), x[c]).astype(BF16) for c in chunks]
    yield HEAD_STAGE

    zero_pad = jnp.zeros((2 * CHUNK, LANES), BF16)
    for c in chunks:
        gn = _dot_tn(wu[c], stack(b_end, c).astype(BF16))
        n_s[c] = gn[LANES:] + _dot_tn(v_p[c], stack(k_end, c).astype(BF16))
        g_s[c] = jnp.where(head_mask, jnp.concatenate([gn[:LANES]] * n_pairs, axis=0), 0.0).astype(BF16)
        dec_s[c] = jnp.broadcast_to(jnp.exp(cum[(c + 1) * CHUNK - 1:(c + 1) * CHUNK, :]), (SUBLANES, W))
        yield HEAD_STAGE
        re_q, y0_q = [], []
        for q in range(n_pairs):
            rhs = jnp.concatenate([wu[c][pair_rows[q]],
                                   jnp.concatenate([zero_pad, v_p[c][pair_rows[q]]], axis=1)], axis=0)
            ry = _dot(l_r[c][q], rhs)
            re_q.append(rows_of(r_e, c)[:, pair_lanes[q]] + ry[:, :LANES])
            y0_q.append(ry[:, LANES:])
        re_s[pl.ds(c * CHUNK, CHUNK), :] = jnp.concatenate(re_q, axis=1).astype(BF16)
        y0_s[pl.ds(c * CHUNK, CHUNK), :] = jnp.concatenate(y0_q, axis=1)
        yield HEAD_STAGE
    bonus_s[...] = _dot((r * k * r_k).astype(BF16), seg_ones) * v
    gate_s[...] = _dot(mg_ref[...], g2_ref[...])
    yield HEAD_STAGE


def _doubling_level_masks():
    idx = jnp.arange(STACK)
    head, time = idx // CHUNK, idx % CHUNK
    same_head = head[:, None] == head[None, :]
    masks = []
    for k in range(int(math.log2(CHUNK))):
        m = 2 ** k
        same_pair = (time[:, None] // (2 * m)) == (time[None, :] // (2 * m))
        masks.append(same_head & same_pair & ((time[:, None] // m) % 2 == 1) & ((time[None, :] // m) % 2 == 0))
    return jnp.stack(masks).astype(BF16)


def _split_dot_left(e, x):
    hi = x.astype(BF16)
    lo = (x - hi.astype(F32)).astype(BF16)
    return _dot(e, hi) + _dot(e, lo)


def rwkv_scan(rkv, mids, ups, v_mix, chan, seq, *, bt=256, groups=4):
    _, T, D = rkv.shape
    groups = min(groups, D // MXU_DIM)
    W = groups * MXU_DIM
    bt = min(bt, seq)
    assert D % W == 0 and seq % bt == 0 and bt % CHUNK == 0
    has_vlora = v_mix is not None
    nb = seq // bt

    def row_in(b, t):
        return b * nb + jnp.minimum(t, nb - 1)

    def stacked(p):
        return pl.BlockSpec((None, bt, W), lambda b, gidx, t: (p, row_in(b, t), gidx))

    def mid(a):
        return pl.BlockSpec((bt, a.shape[1]), lambda b, gidx, t: (row_in(b, t), 0))

    def up(a):
        return pl.BlockSpec((a.shape[0], W), lambda b, gidx, t: (0, gidx))

    ins = [rkv, rkv, rkv, *mids, *ups]
    specs = [stacked(0), stacked(1), stacked(2)] + [mid(a) for a in mids] + [up(a) for a in ups]
    if has_vlora:
        mv, v2, first_rkv = v_mix
        ins += [mv, v2, first_rkv]
        specs += [mid(mv), up(v2), stacked(2)]
    levels = _doubling_level_masks()
    ins += [levels, chan]
    specs += [pl.BlockSpec(levels.shape, lambda b, gidx, t: (0, 0, 0)),
              pl.BlockSpec((SUBLANES, W), lambda b, gidx, t: (0, gidx))]
    nc = bt // CHUNK
    G = MXU_DIM
    stash = [pltpu.VMEM((2, groups, nc, G, G), BF16),
             pltpu.VMEM((2, groups, nc, 2 * HEAD, G), F32),
             pltpu.VMEM((2, groups, bt, G), BF16),
             pltpu.VMEM((2, groups, bt, G), F32),
             pltpu.VMEM((2, groups, nc, SUBLANES, G), F32),
             pltpu.VMEM((2, groups, bt, G), F32),
             pltpu.VMEM((2, groups, bt, G), F32)]
    return pl.pallas_call(
        functools.partial(_scan_body, n_chunks=nc, has_vlora=has_vlora, groups=groups),
        grid=(T // seq, D // W, nb + 1),
        in_specs=specs,
        out_specs=pl.BlockSpec((bt, W), lambda b, gidx, t: (b * nb + jnp.maximum(t - 1, 0), gidx)),
        out_shape=jax.ShapeDtypeStruct((T, D), BF16),
        scratch_shapes=[pltpu.VMEM((groups, 2 * HEAD, G), F32)] + stash,
        compiler_params=_params("parallel", "parallel", "arbitrary"),
        name="rwkv_scan",
    )(*ins)


def _bias_body(bmap_ref, rb_ref, o_ref):
    h = pl.program_id(0)
    bmap = bmap_ref[...]
    acc = jnp.full(bmap.shape, -jnp.inf, F32)
    for b in range(REL_BUCKETS):
        acc = jnp.where(bmap == b, rb_ref[b, h], acc)
    o_ref[0, 0] = acc
    kj = lax.broadcasted_iota(jnp.int32, bmap.shape, 1)
    o_ref[1, 0] = jnp.where(kj >= BLOCK, acc, -jnp.inf)


def bias_table(rel_bias, bmap):
    H = rel_bias.shape[1]
    return pl.pallas_call(
        _bias_body,
        grid=(H,),
        in_specs=[pl.BlockSpec((BLOCK, 2 * BLOCK), lambda h: (0, 0)),
                  pl.BlockSpec(memory_space=pltpu.SMEM)],
        out_specs=pl.BlockSpec((2, 1, BLOCK, 2 * BLOCK), lambda h: (0, h, 0, 0)),
        out_shape=jax.ShapeDtypeStruct((2, H, BLOCK, 2 * BLOCK), F32),
        compiler_params=_params("parallel"),
        name="rel_bias_table",
    )(bmap, rel_bias)


def _attn_body(q_ref, kp_ref, kc_ref, vp_ref, vc_ref, bias_ref, sink_ref, o_ref, *, group, n_kv):
    low = lax.broadcasted_iota(jnp.int32, (BLOCK // 2, LANES), 1) < HEAD
    scale = HEAD ** -0.5

    for kv_pair in range(n_kv // 2):
        lanes = pl.ds(kv_pair * LANES, LANES)
        kband = jnp.concatenate([kp_ref[:, lanes], kc_ref[:, lanes]], axis=0)
        vband = jnp.concatenate([vp_ref[:, lanes], vc_ref[:, lanes]], axis=0)
        kswap = pltpu.roll(kband, HEAD, 1)
        vswap = pltpu.roll(vband, HEAD, 1)
        for e in range(2):
            kvh = 2 * kv_pair + e
            k_at = (kband, kswap) if e == 0 else (kswap, kband)
            v_at = (vband, vswap) if e == 0 else (vswap, vband)
            for pair in range(group // 2):
                q_lanes = pl.ds((kvh * group // 2 + pair) * LANES, LANES)
                for rows in (pl.ds(0, BLOCK // 2), pl.ds(BLOCK // 2, BLOCK // 2)):
                    qp = q_ref[rows, q_lanes] * scale
                    halves = []
                    for half in range(2):
                        h = kvh * group + 2 * pair + half
                        in_half = low if half == 0 else jnp.logical_not(low)
                        qm = jnp.where(in_half, qp, jnp.zeros_like(qp))
                        s = _dot_nt(qm, k_at[half]) + bias_ref[0, h, rows, :]
                        sink = sink_ref[h]
                        m = jnp.maximum(jnp.max(s, axis=-1, keepdims=True), sink)
                        p = jnp.exp(s - m)
                        denom = jnp.sum(p, axis=-1, keepdims=True) + jnp.exp(sink - m)
                        halves.append(_dot(p.astype(BF16), v_at[half]) / denom)
                    o_ref[rows, q_lanes] = jnp.where(low, halves[0], halves[1]).astype(o_ref.dtype)


def swa_attention(qkv, bias, sinks, seq, n_q_heads, n_kv_heads):
    T = qkv.shape[0]
    group = n_q_heads // n_kv_heads
    assert group % 2 == 0 and n_kv_heads % 2 == 0 and seq % BLOCK == 0
    nb = seq // BLOCK
    qw, kw = n_q_heads * HEAD, n_kv_heads * HEAD
    assert qw % kw == 0 and kw % LANES == 0
    k_col, v_col = qw // kw, qw // kw + 1

    def cur(col):
        return pl.BlockSpec((BLOCK, kw), lambda b, n: (b * nb + n, col))

    def prev(col):
        return pl.BlockSpec((BLOCK, kw), lambda b, n: (b * nb + jnp.maximum(n - 1, 0), col))

    return pl.pallas_call(
        functools.partial(_attn_body, group=group, n_kv=n_kv_heads),
        grid=(T // seq, nb),
        in_specs=[pl.BlockSpec((BLOCK, qw), lambda b, n: (b * nb + n, 0)),
                  prev(k_col), cur(k_col), prev(v_col), cur(v_col),
                  pl.BlockSpec((1, n_q_heads, BLOCK, 2 * BLOCK), lambda b, n: (jnp.where(n == 0, 1, 0), 0, 0, 0)),
                  pl.BlockSpec(memory_space=pltpu.SMEM)],
        out_specs=pl.BlockSpec((BLOCK, qw), lambda b, n: (b * nb + n, 0)),
        out_shape=jax.ShapeDtypeStruct((T, qw), BF16),
        compiler_params=_params("parallel", "parallel"),
        name="swa_attention",
    )(qkv, qkv, qkv, qkv, qkv, bias, sinks)


def _bucket_map():
    qi = jnp.arange(BLOCK)[:, None]
    kj = jnp.arange(2 * BLOCK)[None, :]
    signed = qi + BLOCK - kj
    dist = jnp.maximum(signed, 0)
    max_exact = REL_BUCKETS // 2
    d_f = jnp.maximum(dist, max_exact).astype(F32)
    large = max_exact + (jnp.log(d_f / max_exact) / math.log(REL_MAX_DIST / max_exact)
                         * (REL_BUCKETS - max_exact)).astype(jnp.int32)
    large = jnp.minimum(large, REL_BUCKETS - 1)
    bucket = jnp.where(dist < max_exact, dist, large)
    return jnp.where((signed >= 0) & (signed < WINDOW), bucket, -1).astype(jnp.int32)


def _ffn_up_body(x_ref, *refs, blocks_per_seq, sub, rsub, n_sub):
    wg_refs, wu_refs = refs[:n_sub], refs[n_sub:2 * n_sub]
    cg_ref, cu_ref, bg_ref, bu_ref, o_ref, tail_g_ref, tail_u_ref, raw_g_ref, raw_u_ref = refs[2 * n_sub:]
    first = pl.program_id(0) % blocks_per_seq == 0
    j = pl.program_id(1)
    bm = x_ref.shape[0]
    H = SUBLANES

    @pl.when((pl.program_id(0) == 0) & (j == 0))
    def _():
        tail_g_ref[...] = jnp.zeros_like(tail_g_ref)
        tail_u_ref[...] = jnp.zeros_like(tail_u_ref)

    def matmul_tile(slot, s, rb, halo_g, halo_u):
        xr = x_ref[pl.ds(rb * rsub, rsub), :]
        raw_g_ref[slot, 0:H, :] = halo_g
        raw_u_ref[slot, 0:H, :] = halo_u
        raw_g_ref[slot, H:, :] = _dot(xr, wg_refs[s][...])
        raw_u_ref[slot, H:, :] = _dot(xr, wu_refs[s][...])
        return raw_g_ref[slot, rsub:, :], raw_u_ref[slot, rsub:, :]

    def conv(raw_ref, slot, c_ref, b_ref, cols):
        h = raw_ref[slot, H:, :]
        h1 = raw_ref[slot, H - 1:H - 1 + rsub, :]
        h2 = raw_ref[slot, H - 2:H - 2 + rsub, :]
        return c_ref[0:1, cols] * h2 + c_ref[1:2, cols] * h1 + c_ref[2:3, cols] * h + b_ref[:, cols]

    def finish_tile(slot, s, rb):
        cols = pl.ds(s * sub, sub)
        gate = conv(raw_g_ref, slot, cg_ref, bg_ref, cols)
        up = conv(raw_u_ref, slot, cu_ref, bu_ref, cols)
        o_ref[pl.ds(rb * rsub, rsub), cols] = (gate * _sigmoid(gate) * up).astype(o_ref.dtype)

    n_rows = bm // rsub
    prev = None
    t = 0
    for s in range(n_sub):
        cols = pl.ds(s * sub, sub)
        halo_g = jnp.where(first, 0.0, tail_g_ref[j, :, cols])
        halo_u = jnp.where(first, 0.0, tail_u_ref[j, :, cols])
        for rb in range(n_rows):
            halo_g, halo_u = matmul_tile(t % 2, s, rb, halo_g, halo_u)
            if prev is not None:
                finish_tile(*prev)
            prev = (t % 2, s, rb)
            t += 1
        tail_g_ref[j, :, cols] = halo_g
        tail_u_ref[j, :, cols] = halo_u
    finish_tile(*prev)


def ffn_up(x, w2, layer, cg, cu, bg, bu, seq, *, bm=1024, bn=512, sub=256, rsub=256):
    T, D = x.shape
    F = w2.shape[2] // 2
    bm, bn = min(bm, seq), min(bn, F)
    sub, rsub = min(sub, bn), min(rsub, bm)
    assert seq % bm == 0 and bn % sub == 0 and F % sub == 0 and bm % rsub == 0 and rsub % SUBLANES == 0
    nf = pl.cdiv(F, bn)
    n_sub = bn // sub
    last = 2 * F // sub - 1

    def wspec(s, half):
        return pl.BlockSpec((None, D, sub),
                            lambda i, j: (layer, 0, jnp.minimum(half * (F // sub) + j * n_sub + s, last)))

    col = lambda rows: pl.BlockSpec((rows, bn), lambda i, j: (0, j))
    return pl.pallas_call(
        functools.partial(_ffn_up_body, blocks_per_seq=seq // bm, sub=sub, rsub=rsub, n_sub=n_sub),
        grid=(T // bm, nf),
        in_specs=[pl.BlockSpec((bm, D), lambda i, j: (i, 0))]
                 + [wspec(s, 0) for s in range(n_sub)] + [wspec(s, 1) for s in range(n_sub)]
                 + [col(CONV_W), col(CONV_W), col(1), col(1)],
        out_specs=pl.BlockSpec((bm, bn), lambda i, j: (i, j)),
        out_shape=jax.ShapeDtypeStruct((T, F), BF16),
        scratch_shapes=[pltpu.VMEM((nf, SUBLANES, bn), F32), pltpu.VMEM((nf, SUBLANES, bn), F32),
                        pltpu.VMEM((2, SUBLANES + rsub, sub), F32), pltpu.VMEM((2, SUBLANES + rsub, sub), F32)],
        compiler_params=_params("arbitrary", "arbitrary"),
        name="ffn_up",
    )(x, *([w2] * (2 * n_sub)), cg, cu, bg, bu)


def _pad_to(x, axis, mult):
    pad = (-x.shape[axis]) % mult
    if pad == 0:
        return x
    widths = [(0, 0)] * x.ndim
    widths[axis] = (0, pad)
    return jnp.pad(x, widths)


WIDE_COL_BLOCK = 1024
SCAN_TIME_BLOCK = 4 * CHUNK


def _lora_weights(w_down, w_up):
    return _pad_to(w_down, 1, LANES).astype(BF16), _pad_to(w_up, 0, LANES).astype(BF16)


def rwkv_layer(x, seq, first_rkv, layer, mix, w_rkv, w_o, w0, w1, w2, a0, a1, a2, g1, g2,
               k_k, k_a, r_k, gn_g, gn_b, v_lora):
    w1, w2 = _lora_weights(w1, w2)
    a1, a2 = _lora_weights(a1, a2)
    g1, g2 = _lora_weights(g1, g2)
    loras = [(MIX_W, w1, "tanh"), (MIX_A, a1, None), (MIX_G, g1, "sigmoid")]
    if v_lora is None:
        v0 = jnp.zeros_like(w0)
    else:
        v0, v1, v2 = v_lora
        v1, v2 = _lora_weights(v1, v2)
        loras.append((MIX_V, v1, None))
    xs, mids = token_shift_mix(x, mix, loras, seq)
    rkv = batched_matmul(xs, w_rkv, F32, nbatch=3, layer=layer, bn=WIDE_COL_BLOCK, name="rwkv_rkv")
    v_mix = None if v_lora is None else (mids[3], v2, first_rkv)
    chan = jnp.stack([w0, a0, k_k, k_a, r_k, gn_g, gn_b, v0])
    y = rwkv_scan(rkv, tuple(mids[:3]), (w2, a2, g2), v_mix, chan, seq, bt=SCAN_TIME_BLOCK)
    return matmul(y, w_o, BF16, layer=layer, bn=WIDE_COL_BLOCK, name="rwkv_out"), rkv


def attn_layer(xb, seq, layer, w_qkv, w_o, sinks, bias):
    n_q = w_o.shape[1] // HEAD
    n_kv = (w_qkv.shape[2] // HEAD - n_q) // 2
    qkv = matmul(xb, w_qkv, BF16, layer=layer, bn=WIDE_COL_BLOCK, name="attn_qkv")
    o = swa_attention(qkv, bias, sinks, seq, n_q, n_kv)
    return matmul(o, w_o, BF16, layer=layer, bn=WIDE_COL_BLOCK, name="attn_out")


FFN_COL_BLOCK = 512
FFN_ROW_BLOCK = 1024
FFN_COL_SUB = MXU_DIM


def ffn_layer(xb, seq, layer, w_up, conv_w, conv_b, w_down):
    f = w_down.shape[1]
    conv_b = conv_b.reshape(1, -1)
    act = ffn_up(xb, w_up, layer, conv_w[:, :f], conv_w[:, f:], conv_b[:, :f], conv_b[:, f:],
                 seq, bm=FFN_ROW_BLOCK, bn=FFN_COL_BLOCK, sub=FFN_COL_SUB)
    return matmul(act, w_down, BF16, layer=layer, bm=512, bn=512, name="ffn_down")


def kernel(x, rwkv_mix, rwkv_w_rkv, rwkv_w_o, rwkv_w0, rwkv_w1, rwkv_w2, rwkv_a0, rwkv_a1, rwkv_a2, rwkv_g1, rwkv_g2, rwkv_k_k, rwkv_k_a, rwkv_r_k, rwkv_gn_g, rwkv_gn_b, rwkv_v0, rwkv_v1, rwkv_v2, attn_w_qkv, attn_w_o, attn_sinks, rel_bias, ffn_w_up, ffn_conv_w, ffn_conv_b, ffn_w_down, ln1_g, ln1_b, ln2_g, ln2_b):
    B, S, D = x.shape
    depth = ln1_g.shape[0]
    xf = x.reshape(B * S, D)
    xb = xf.astype(BF16)
    bias = bias_table(rel_bias, _bucket_map())
    w_rkv, w_ro = rwkv_w_rkv.astype(BF16), rwkv_w_o.astype(BF16)
    w_qkv, w_ao = attn_w_qkv.astype(BF16), attn_w_o.astype(BF16)
    w_up, w_down = ffn_w_up.astype(BF16), ffn_w_down.astype(BF16)
    first_rkv = None
    for i in range(depth):
        j = i // 2
        if i % 2 == 0:
            v_lora = None if j == 0 else (rwkv_v0[j - 1], rwkv_v1[j - 1], rwkv_v2[j - 1])
            h, rkv = rwkv_layer(xf, S, first_rkv, j, rwkv_mix[j], w_rkv, w_ro,
                                rwkv_w0[j], rwkv_w1[j], rwkv_w2[j], rwkv_a0[j],
                                rwkv_a1[j], rwkv_a2[j], rwkv_g1[j], rwkv_g2[j], rwkv_k_k[j],
                                rwkv_k_a[j], rwkv_r_k[j], rwkv_gn_g[j], rwkv_gn_b[j], v_lora)
            if v_lora is None:
                first_rkv = rkv
        else:
            h = attn_layer(xb, S, j, w_qkv, w_ao, attn_sinks[j], bias)
        xf, xb = ln_residual(xf, h, ln1_g[i], ln1_b[i])
        f = ffn_layer(xb, S, i, w_up, ffn_conv_w[i], ffn_conv_b[i], w_down)
        xf, xb = ln_residual(xf, f, ln2_g[i], ln2_b[i])
    return xf.reshape(B, S, D)
```
